```python
import jax, jax.numpy as jnp
from jax import lax
import numpy as np

D_MODEL = 1024
BATCH = 8
SEQ = 2048
DEPTH = 2
DEC_BATCH = 128
DEC_SEQ = 1
PAST_LEN = 16384
PAGE_SIZE = 128

N_BRANCH = 4
BRANCH_WIDTH = D_MODEL // N_BRANCH
N_HEADS = 4
HEAD_DIM = BRANCH_WIDTH // N_HEADS
D_FF = ((8 * D_MODEL // 3 + 255) // 256) * 256
CHUNK = 64
CONV_WIDTH = 4
LRU_C = 8.0
POOL_WINDOWS = (2, 4, 8, 16)
POOL_GROUP = BRANCH_WIDTH // len(POOL_WINDOWS)
POOL_BUF = max(POOL_WINDOWS) - 1
ROPE_BASE = 10000.0
N_MIX_SLOTS = 11
IN_COLS = N_MIX_SLOTS * BRANCH_WIDTH + N_BRANCH * D_MODEL
EPS = 1e-6
F_FLOOR = 1e-30

kernel_name = 'hybrid_hgrn2_rglru_retnet_pool_decoder_step'


def rms_norm(x, g):
    xf = x.astype(jnp.float32)
    y = xf * lax.rsqrt(jnp.mean(xf * xf, axis=-1, keepdims=True) + EPS)
    return (y * g.astype(jnp.float32)).astype(x.dtype)


def swiglu_ffn(x, g, w_up, w_down):
    a, b = jnp.split(rms_norm(x, g) @ w_up, 2, axis=-1)
    return (jax.nn.silu(a) * b) @ w_down


def to_heads(a):
    B, T, _ = a.shape
    return a.reshape(B, T, N_HEADS, HEAD_DIM).transpose(0, 2, 1, 3)


def from_heads(a):
    B, H, T, Dh = a.shape
    return a.transpose(0, 2, 1, 3).reshape(B, T, H * Dh)


def head_rms_norm(o):
    return o * lax.rsqrt(jnp.mean(o * o, axis=-1, keepdims=True) + EPS)


def head_group_norm(o):
    mu = jnp.mean(o, axis=-1, keepdims=True)
    var = jnp.mean(jnp.square(o - mu), axis=-1, keepdims=True)
    return (o - mu) * lax.rsqrt(var + EPS)


def chunk_linear_recurrence(q, k, v, log_f, s0):
    B, H, T, _ = q.shape
    c = CHUNK if T % CHUNK == 0 else T
    n = T // c

    def blocks(a):
        return a.reshape(B, H, n, c, a.shape[-1]).transpose(2, 0, 1, 3, 4)

    causal = jnp.tril(jnp.ones((c, c), dtype=bool))[None, None, :, :, None]

    def step(S, inp):
        qc, kc, vc, gc = inp
        cum = jnp.cumsum(gc, axis=2)
        o_inter = jnp.einsum('bhtk,bhkv->bhtv', qc * jnp.exp(cum), S)
        diff = cum[:, :, :, None, :] - cum[:, :, None, :, :]
        w = jnp.where(causal, jnp.exp(jnp.minimum(diff, 0.0)), 0.0)
        scores = jnp.sum(qc[:, :, :, None, :] * kc[:, :, None, :, :] * w, axis=-1)
        o = o_inter + jnp.einsum('bhts,bhsv->bhtv', scores, vc)
        last = cum[:, :, -1:, :]
        S = (jnp.exp(last[:, :, 0, :])[..., None] * S
             + jnp.einsum('bhsk,bhsv->bhkv', kc * jnp.exp(last - cum), vc))
        return S, o

    s_final, o = lax.scan(step, s0, (blocks(q), blocks(k), blocks(v), blocks(log_f)))
    o = o.transpose(1, 2, 0, 3, 4).reshape(B, H, T, v.shape[-1])
    return o, s_final


def hgrn2_mixer(uq, uf, ui, ug, lb, norm_g, s0):
    q = jax.nn.silu(uq.astype(jnp.float32))
    z = uf.astype(jnp.float32)
    f = lb + (1.0 - lb) * jax.nn.sigmoid(z)
    log_f = jnp.log(jnp.maximum(f, F_FLOOR))
    k = (1.0 - lb) * jax.nn.sigmoid(-z)
    o, s = chunk_linear_recurrence(to_heads(q), to_heads(k), to_heads(ui.astype(jnp.float32)),
                                   to_heads(log_f), s0.astype(jnp.float32))
    o = from_heads(head_rms_norm(o)) * norm_g * jax.nn.silu(ug.astype(jnp.float32))
    return o, s


def rglru_mixer(ux, uy, conv_w, conv_b, w_r, b_r, w_i, b_i, lam, h0, conv0):
    xf = ux.astype(jnp.float32)
    B, T, W = xf.shape
    xp = jnp.concatenate([conv0.astype(jnp.float32), xf], axis=1)
    xc = conv_b + sum(conv_w[j] * xp[:, j:j + T] for j in range(CONV_WIDTH))
    new_conv = xp[:, T:]
    xh = xc.reshape(B, T, N_HEADS, HEAD_DIM)
    r = jax.nn.sigmoid(jnp.einsum('bthi,hij->bthj', xh, w_r).reshape(B, T, W) + b_r)
    i = jax.nn.sigmoid(jnp.einsum('bthi,hij->bthj', xh, w_i).reshape(B, T, W) + b_i)
    log_a = -LRU_C * r * jax.nn.softplus(-lam)
    a = jnp.exp(log_a)
    b_in = jnp.sqrt(jnp.maximum(-jnp.expm1(2.0 * log_a), 0.0)) * (i * xc)

    def combine(left, right):
        a1, b1 = left
        a2, b2 = right
        return a1 * a2, a2 * b1 + b2

    a_cum, b_cum = lax.associative_scan(combine, (a, b_in), axis=1)
    h = a_cum * h0.astype(jnp.float32)[:, None] + b_cum
    out = h * jax.nn.gelu(uy.astype(jnp.float32))
    return out, h[:, -1], new_conv


def rope(x, pos):
    half = HEAD_DIM // 2
    freq = ROPE_BASE ** (-jnp.arange(half, dtype=jnp.float32) / half)
    ang = pos[:, None] * freq[None, :]
    cos, sin = jnp.cos(ang), jnp.sin(ang)
    x1, x2 = x[..., :half], x[..., half:]
    return jnp.concatenate([x1 * cos - x2 * sin, x1 * sin + x2 * cos], axis=-1)


def retention_mixer(uq, uk, uv, ug, norm_g, s0, pos0):
    B, T, _ = uq.shape
    pos = pos0 + jnp.arange(T, dtype=jnp.float32)
    q = rope(to_heads(uq.astype(jnp.float32)), pos)
    k = rope(to_heads(uk.astype(jnp.float32)), pos) * HEAD_DIM ** -0.5
    v = to_heads(uv.astype(jnp.float32))
    log_gamma = jnp.log1p(-(2.0 ** (-5.0 - jnp.arange(N_HEADS, dtype=jnp.float32))))
    log_f = jnp.broadcast_to(log_gamma[None, :, None, None], (B, N_HEADS, T, 1))
    o, s = chunk_linear_recurrence(q, k, v, log_f, s0.astype(jnp.float32))
    o = from_heads(head_group_norm(o)) * norm_g * jax.nn.silu(ug.astype(jnp.float32))
    return o, s


def pool_mixer(ud, w_pool, scale, buf0, pos0):
    xf = ud.astype(jnp.float32)
    B, T, W = xf.shape
    xp = jnp.concatenate([buf0.astype(jnp.float32), xf], axis=1)
    cs = jnp.concatenate([jnp.zeros((B, 1, W), jnp.float32), jnp.cumsum(xp, axis=1)], axis=1)
    end = cs[:, POOL_BUF + 1:]
    pos = pos0 + jnp.arange(T, dtype=jnp.float32)
    outs = []
    for gi, win in enumerate(POOL_WINDOWS):
        sl = slice(gi * POOL_GROUP, (gi + 1) * POOL_GROUP)
        start = cs[:, POOL_BUF + 1 - win:POOL_BUF + 1 - win + T, sl]
        count = jnp.minimum(float(win), pos + 1.0)[None, :, None]
        pooled = (end[..., sl] - start) / count
        outs.append((pooled - xf[..., sl]) @ w_pool[gi])
    return jnp.concatenate(outs, axis=-1) * scale, xp[:, T:]


def trunk(x, s_hgrn, s_lru, s_conv, s_ret, s_pool, pos0, params):
    (lb_logits, ffn1_norm, ffn1_up, ffn1_down, mix_norm, w_in, hgrn_norm, conv_w, conv_b,
     w_rgate, b_rgate, w_igate, b_igate, lru_lambda, ret_norm, w_pool, pool_scale,
     w_branch, w_o, ffn2_norm, ffn2_up, ffn2_down, final_norm) = params
    B, T, _ = x.shape
    lb_soft = jax.nn.softmax(lb_logits.astype(jnp.float32), axis=0)
    lower_bounds = jnp.cumsum(lb_soft, axis=0) - lb_soft[0:1]
    new_states = ([], [], [], [], [])
    for l in range(DEPTH):
        x = x + 0.5 * swiglu_ffn(x, ffn1_norm[l], ffn1_up[l], ffn1_down[l])
        u = rms_norm(x, mix_norm[l]) @ w_in[l]
        aq, af, ai, ag, bx, by, cq, ck, cv, cg, dx = jnp.split(
            u[..., :N_MIX_SLOTS * BRANCH_WIDTH], N_MIX_SLOTS, axis=-1)
        gates = jax.nn.sigmoid(u[..., N_MIX_SLOTS * BRANCH_WIDTH:].astype(jnp.float32)
                               ).reshape(B, T, N_BRANCH, D_MODEL)
        o_a, st_a = hgrn2_mixer(aq, af, ai, ag, lower_bounds[l], hgrn_norm[l], s_hgrn[l])
        o_b, st_b, st_c = rglru_mixer(bx, by, conv_w[l], conv_b[l], w_rgate[l], b_rgate[l],
                                      w_igate[l], b_igate[l], lru_lambda[l], s_lru[l], s_conv[l])
        o_c, st_r = retention_mixer(cq, ck, cv, cg, ret_norm[l], s_ret[l], pos0)
        o_d, st_p = pool_mixer(dx, w_pool[l], pool_scale[l], s_pool[l], pos0)
        branches = (o_a, o_b, o_c, o_d)
        merged = sum(gates[:, :, b] * (branches[b] @ w_branch[l, b]) for b in range(N_BRANCH))
        x = x + (merged @ w_o[l]).astype(x.dtype)
        x = x + 0.5 * swiglu_ffn(x, ffn2_norm[l], ffn2_up[l], ffn2_down[l])
        for lst, st in zip(new_states, (st_a, st_b, st_c, st_r, st_p)):
            lst.append(st)
    y = rms_norm(x, final_norm)
    return (y, jnp.stack(new_states[0]), jnp.stack(new_states[1]), jnp.stack(new_states[2]),
            jnp.stack(new_states[3]), jnp.stack(new_states[4]))


def setup_inputs(seed: int = 0) -> dict:
    key = jax.random.key(seed)
    ks = iter(jax.random.split(key, 48))
    f32 = jnp.float32

    def nrm(shape, scale):
        return scale * jax.random.normal(next(ks), shape, f32)

    def gain(shape):
        return 1.0 + 0.05 * jax.random.normal(next(ks), shape, f32)

    W, H, Dh, L = BRANCH_WIDTH, N_HEADS, HEAD_DIM, DEPTH
    u = jax.random.uniform(next(ks), (L, W), f32, 0.9, 0.999)
    sig = u ** (1.0 / LRU_C)
    lru_lambda = jnp.log(sig) - jnp.log1p(-sig)
    return {
        'x_prompt': nrm((BATCH, SEQ, D_MODEL), 1.0),
        'x_sample': nrm((DEC_BATCH, DEC_SEQ, D_MODEL), 1.0),
        'state_hgrn': nrm((L, DEC_BATCH, H, Dh, Dh), 0.3),
        'state_rglru': nrm((L, DEC_BATCH, W), 0.5),
        'state_conv': nrm((L, DEC_BATCH, CONV_WIDTH - 1, W), 1.0),
        'state_retention': nrm((L, DEC_BATCH, H, Dh, Dh), 1.0),
        'state_pool': nrm((L, DEC_BATCH, POOL_BUF, W), 1.0),
        'lb_logits': nrm((L, W), 1.0),
        'ffn1_norm': gain((L, D_MODEL)),
        'ffn1_up': nrm((L, D_MODEL, 2 * D_FF), D_MODEL ** -0.5),
        'ffn1_down': nrm((L, D_FF, D_MODEL), D_FF ** -0.5),
        'mix_norm': gain((L, D_MODEL)),
        'w_in': nrm((L, D_MODEL, IN_COLS), D_MODEL ** -0.5),
        'hgrn_norm': gain((L, W)),
        'conv_w': nrm((L, CONV_WIDTH, W), CONV_WIDTH ** -0.5),
        'conv_b': nrm((L, W), 0.02),
        'w_rgate': nrm((L, H, Dh, Dh), Dh ** -0.5),
        'b_rgate': nrm((L, W), 0.02),
        'w_igate': nrm((L, H, Dh, Dh), Dh ** -0.5),
        'b_igate': nrm((L, W), 0.02),
        'lru_lambda': lru_lambda,
        'ret_norm': gain((L, W)),
        'w_pool': nrm((L, len(POOL_WINDOWS), POOL_GROUP, POOL_GROUP), POOL_GROUP ** -0.5),
        'pool_scale': gain((L, W)),
        'w_branch': nrm((L, N_BRANCH, W, D_MODEL), W ** -0.5),
        'w_o': nrm((L, D_MODEL, D_MODEL), D_MODEL ** -0.5),
        'ffn2_norm': gain((L, D_MODEL)),
        'ffn2_up': nrm((L, D_MODEL, 2 * D_FF), D_MODEL ** -0.5),
        'ffn2_down': nrm((L, D_FF, D_MODEL), D_FF ** -0.5),
        'final_norm': gain((D_MODEL,)),
    }


def reference(x_prompt, x_sample, state_hgrn, state_rglru, state_conv, state_retention, state_pool,
              lb_logits, ffn1_norm, ffn1_up, ffn1_down, mix_norm, w_in, hgrn_norm, conv_w, conv_b,
              w_rgate, b_rgate, w_igate, b_igate, lru_lambda, ret_norm, w_pool, pool_scale,
              w_branch, w_o, ffn2_norm, ffn2_up, ffn2_down, final_norm):
    params = (lb_logits, ffn1_norm, ffn1_up, ffn1_down, mix_norm, w_in, hgrn_norm, conv_w, conv_b,
              w_rgate, b_rgate, w_igate, b_igate, lru_lambda, ret_norm, w_pool, pool_scale,
              w_branch, w_o, ffn2_norm, ffn2_up, ffn2_down, final_norm)
    f32 = jnp.float32
    bp = x_prompt.shape[0]
    y_prompt, hgrn_p, rglru_p, conv_p, ret_p, pool_p = trunk(
        x_prompt,
        jnp.zeros((DEPTH, bp, N_HEADS, HEAD_DIM, HEAD_DIM), f32),
        jnp.zeros((DEPTH, bp, BRANCH_WIDTH), f32),
        jnp.zeros((DEPTH, bp, CONV_WIDTH - 1, BRANCH_WIDTH), f32),
        jnp.zeros((DEPTH, bp, N_HEADS, HEAD_DIM, HEAD_DIM), f32),
        jnp.zeros((DEPTH, bp, POOL_BUF, BRANCH_WIDTH), f32),
        0, params)
    y_sample, hgrn_s, rglru_s, conv_s, ret_s, pool_s = trunk(
        x_sample, state_hgrn, state_rglru, state_conv, state_retention, state_pool,
        PAST_LEN, params)
    return (y_prompt, y_sample, hgrn_p, rglru_p, conv_p, ret_p, pool_p,
            hgrn_s, rglru_s, conv_s, ret_s, pool_s)
```

```python
import functools

import numpy as np
import jax
import jax.numpy as jnp
from jax import lax
from jax.experimental import pallas as pl
from jax.experimental.pallas import tpu as pltpu

F32 = jnp.float32
BF16 = jnp.bfloat16

N_HEADS = 4
HEAD_DIM = 64
WIDTH = N_HEADS * HEAD_DIM
N_MIX_SLOTS = 11
MIX_COLS = N_MIX_SLOTS * WIDTH
CHUNK = 64
SUB = 16
PAD = 16
CONV_WIDTH = 4
LRU_C = 8.0
POOL_WINDOWS = (2, 4, 8, 16)
POOL_BUF = 15
ROPE_BASE = 10000.0
EPS = 1e-6
F_FLOOR = 1e-30
PAST_LEN = 16384
VMEM_LIMIT = 56 * 1024 * 1024


def _cparams(*sem):
    return pltpu.CompilerParams(dimension_semantics=sem, vmem_limit_bytes=VMEM_LIMIT)


def _dot(a, b):
    return jnp.dot(a, b, preferred_element_type=F32)


def _dot_nt(a, b):
    return lax.dot_general(a, b, (((1,), (1,)), ((), ())), preferred_element_type=F32)


def _dot_tn(a, b):
    return lax.dot_general(a, b, (((0,), (0,)), ((), ())), preferred_element_type=F32)


def _rms(x, g):
    return x * lax.rsqrt(jnp.mean(x * x, axis=-1, keepdims=True) + EPS) * g


def _sigmoid(x):
    return jax.nn.sigmoid(x)


def _silu(x):
    return x * jax.nn.sigmoid(x)


def _softplus(x):
    return jnp.maximum(x, 0.0) + jnp.log1p(jnp.exp(-jnp.abs(x)))


def _neg_expm1(y):
    t = jnp.tanh(0.5 * y)
    return -2.0 * t / (1.0 - t)


def _split2(x):
    hi = x.astype(BF16)
    lo = (x - hi.astype(F32)).astype(BF16)
    return hi, lo


def _split3(x):
    hi = x.astype(BF16)
    r = x - hi.astype(F32)
    mid = r.astype(BF16)
    lo = (r - mid.astype(F32)).astype(BF16)
    return hi, mid, lo


def _lower_bound(lbl, layer):
    m = jnp.max(lbl, axis=0, keepdims=True)
    e = jnp.exp(lbl - m)
    soft = e / jnp.sum(e, axis=0, keepdims=True)
    acc = soft[0:1]
    for l in range(1, layer + 1):
        acc = acc + soft[l:l + 1]
    return acc - soft[0:1]


def _ffn_kernel(x_ref, g_ref, wa_ref, wb_ref, wd_ref, fg_ref, o_ref, h_scr, acc_scr, *, final_norm):
    j = pl.program_id(1)

    @pl.when(j == 0)
    def _():
        h_scr[...] = _rms(x_ref[...], g_ref[...]).astype(BF16)
        acc_scr[...] = jnp.zeros_like(acc_scr)

    h = h_scr[...]
    a = _dot(h, wa_ref[...])
    b = _dot(h, wb_ref[...])
    act = (_silu(a) * b).astype(BF16)
    acc_scr[...] += _dot(act, wd_ref[...])

    @pl.when(j == pl.num_programs(1) - 1)
    def _():
        y = x_ref[...] + 0.5 * acc_scr[...]
        if final_norm:
            y = _rms(y, fg_ref[...])
        o_ref[...] = y


def _ffn(x, g, w_up, w_down, layer, final_g, *, final_norm, tm, tf):
    n, d = x.shape
    dff = w_down.shape[1]
    nf = dff // tf
    return pl.pallas_call(
        functools.partial(_ffn_kernel, final_norm=final_norm),
        grid=(n // tm, nf),
        in_specs=[
            pl.BlockSpec((tm, d), lambda i, j: (i, 0)),
            pl.BlockSpec((None, 1, d), lambda i, j: (layer, 0, 0)),
            pl.BlockSpec((None, d, tf), lambda i, j: (layer, 0, j)),
            pl.BlockSpec((None, d, tf), lambda i, j: (layer, 0, j + nf)),
            pl.BlockSpec((None, tf, d), lambda i, j: (layer, j, 0)),
            pl.BlockSpec((1, d), lambda i, j: (0, 0)),
        ],
        out_specs=pl.BlockSpec((tm, d), lambda i, j: (i, 0)),
        out_shape=jax.ShapeDtypeStruct((n, d), F32),
        scratch_shapes=[pltpu.VMEM((tm, d), BF16), pltpu.VMEM((tm, d), F32)],
        compiler_params=_cparams("parallel", "arbitrary"),
        name="ffn",
    )(x, g, w_up, w_up, w_down, final_g)


def _inproj_kernel(x_ref, g_ref, w_ref, o_ref):
    h = _rms(x_ref[...], g_ref[...]).astype(BF16)
    o_ref[...] = _dot(h, w_ref[...])


def _inproj(x, g, w_mix, layer, *, tm):
    n, d = x.shape
    cols = w_mix.shape[2]
    return pl.pallas_call(
        _inproj_kernel,
        grid=(n // tm,),
        in_specs=[
            pl.BlockSpec((tm, d), lambda i: (i, 0)),
            pl.BlockSpec((None, 1, d), lambda i: (layer, 0, 0)),
            pl.BlockSpec((None, d, cols), lambda i: (layer, 0, 0)),
        ],
        out_specs=pl.BlockSpec((tm, cols), lambda i: (i, 0)),
        out_shape=jax.ShapeDtypeStruct((n, cols), F32),
        compiler_params=_cparams("parallel"),
        name="inproj",
    )(x, g, w_mix)


def _merge_kernel(x_ref, g_ref, oa_ref, ob_ref, oc_ref, od_ref, wg_ref, wb_ref, wo_ref, o_ref):
    x = x_ref[...]
    d = x.shape[1]
    h = _rms(x, g_ref[...]).astype(BF16)
    merged = None
    for b, br_ref in enumerate((oa_ref, ob_ref, oc_ref, od_ref)):
        gate = _sigmoid(_dot(h, wg_ref[:, b * d:(b + 1) * d]))
        y = _dot(br_ref[...].astype(BF16), wb_ref[b])
        merged = gate * y if merged is None else merged + gate * y
    o_ref[...] = x + _dot(merged.astype(BF16), wo_ref[...])


def _merge(x, g, branches, w_gate, w_branch, w_o, layer, *, tm):
    n, d = x.shape
    w = branches[0].shape[1]
    nb = len(branches)
    return pl.pallas_call(
        _merge_kernel,
        grid=(n // tm,),
        in_specs=[
            pl.BlockSpec((tm, d), lambda i: (i, 0)),
            pl.BlockSpec((None, 1, d), lambda i: (layer, 0, 0)),
        ] + [pl.BlockSpec((tm, w), lambda i: (i, 0)) for _ in range(nb)] + [
            pl.BlockSpec((None, d, nb * d), lambda i: (layer, 0, 0)),
            pl.BlockSpec((None, nb, w, d), lambda i: (layer, 0, 0, 0)),
            pl.BlockSpec((None, d, d), lambda i: (layer, 0, 0)),
        ],
        out_specs=pl.BlockSpec((tm, d), lambda i: (i, 0)),
        out_shape=jax.ShapeDtypeStruct((n, d), F32),
        compiler_params=_cparams("parallel"),
        name="merge",
    )(x, g, *branches, w_gate, w_branch, w_o)


def _head_block_ones():
    h = np.arange(WIDTH) // HEAD_DIM
    return (h[:, None] == h[None, :]).astype(np.float32)


def _log_gamma():
    return np.log1p(-(2.0 ** (-5.0 - np.arange(N_HEADS, dtype=np.float64))))


def _retention_tables():
    lg = np.repeat(_log_gamma(), HEAD_DIM)[None, :]
    t = np.arange(CHUNK, dtype=np.float64)[:, None]
    q_dec = np.exp((t + 1.0) * lg)
    k_dec = np.exp((CHUNK - 1.0 - t) * lg)
    s_dec = np.exp(CHUNK * lg)
    dt = t - t.T
    dmat = np.concatenate(
        [np.where(dt >= 0, np.exp(dt * g), 0.0) for g in _log_gamma()], axis=0)
    return (jnp.asarray(q_dec, F32), jnp.asarray(k_dec, F32), jnp.asarray(s_dec, F32),
            jnp.asarray(dmat, F32))


def _rope_tables(pos):
    half = HEAD_DIM // 2
    freq = ROPE_BASE ** (-jnp.arange(half, dtype=F32) / half)
    ang = pos[:, None] * freq[None, :]
    cos, sin = jnp.cos(ang), jnp.sin(ang)
    cos_h = jnp.concatenate([cos, cos], axis=-1)
    sin_h = jnp.concatenate([-sin, sin], axis=-1)
    return jnp.tile(cos_h, (1, N_HEADS)), jnp.tile(sin_h, (1, N_HEADS))


def _block_diag(w):
    h, dh, _ = w.shape
    eye = jnp.eye(h, dtype=w.dtype)
    return (eye[:, None, :, None] * w[:, :, None, :]).reshape(h * dh, h * dh)


V_HNORM, V_CW0, V_CB, V_BR, V_BI, V_LAM, V_RNORM, V_PSCALE = 0, 1, 5, 6, 7, 8, 9, 10
N_VEC_ROWS = 16


def _swap_halves(x):
    half = HEAD_DIM // 2
    lane = lax.broadcasted_iota(jnp.int32, (1, 128), 1)
    first = (lane % HEAD_DIM) < half
    parts = []
    for c in range(x.shape[1] // 128):
        xc = x[:, c * 128:(c + 1) * 128]
        parts.append(jnp.where(first, pltpu.roll(xc, 128 - half, 1), pltpu.roll(xc, half, 1)))
    return jnp.concatenate(parts, axis=1)


def _mixer_prompt_kernel(u_ref, lbl_ref, vec_ref, wr_ref, wi_ref, wp_ref, cos_ref, sin_ref,
                         ind_ref, tri_ref, qdec_ref, kdec_ref, sdec_ref, dmat_ref,
                         oa_ref, ob_ref, oc_ref, od_ref, sh_ref, sl_ref, scv_ref, sr_ref, spl_ref,
                         hst_scr, rst_scr, hl_scr, xcv_scr, xpl_scr, kp_scr, cp_scr, vp_scr, oh_scr,
                         orr_scr, *, layer, tc, pos0):
    t_idx = pl.program_id(1)
    nchunk = tc // CHUNK

    @pl.when(t_idx == 0)
    def _():
        hst_scr[...] = jnp.zeros_like(hst_scr)
        rst_scr[...] = jnp.zeros_like(rst_scr)
        hl_scr[...] = jnp.zeros_like(hl_scr)
        xcv_scr[0:PAD, :] = jnp.zeros((PAD, WIDTH), F32)
        xpl_scr[0:PAD, :] = jnp.zeros((PAD, WIDTH), F32)
        kp_scr[0:PAD, :] = jnp.zeros((PAD, WIDTH), F32)
        cp_scr[0:PAD, :] = jnp.zeros((PAD, WIDTH), F32)
        vp_scr[0:PAD, :] = jnp.zeros((PAD, WIDTH), F32)

    def slot(s):
        return u_ref[:, s * WIDTH:(s + 1) * WIDTH]

    def vec(r):
        return vec_ref[r:r + 1, :]

    ind = ind_ref[...]
    maskbd = ind.astype(F32)
    lane = lax.broadcasted_iota(jnp.int32, (1, WIDTH), 1)
    hmask = [(lane // HEAD_DIM == h).astype(F32) for h in range(N_HEADS)]
    row = lax.broadcasted_iota(jnp.int32, (tc, WIDTH), 0)

    def head_sum(x):
        hi, lo = _split2(x)
        return _dot(hi, ind) + _dot(lo, ind)

    lb = _lower_bound(lbl_ref[...], layer)
    uq, z, v = slot(0), slot(1), slot(2)
    q = _silu(uq)
    f = lb + (1.0 - lb) * _sigmoid(z)
    logf = jnp.log(jnp.maximum(f, F_FLOOR))
    k = (1.0 - lb) * _sigmoid(-z)
    tri = tri_ref[...]
    l_hi, l_mid, l_lo = _split3(logf)
    cum = _dot(tri, l_hi) + _dot(tri, l_mid) + _dot(tri, l_lo)

    kp_scr[PAD:PAD + tc, :] = k
    cp_scr[PAD:PAD + tc, :] = cum
    vp_scr[PAD:PAD + tc, :] = v
    rowmod = row % SUB
    o_diag = jnp.zeros((tc, WIDTH), F32)
    for dlt in range(SUB):
        kd = kp_scr[PAD - dlt:PAD - dlt + tc, :]
        cd = cp_scr[PAD - dlt:PAD - dlt + tc, :]
        vd = vp_scr[PAD - dlt:PAD - dlt + tc, :]
        w = jnp.where(rowmod >= dlt, q * kd * jnp.exp(jnp.minimum(cum - cd, 0.0)), 0.0)
        o_diag = o_diag + _dot(w.astype(BF16), ind) * vd
    oh_scr[...] = o_diag

    v_bf = v.astype(BF16)
    for n in range(nchunk):
        r0 = n * CHUNK
        cum_c = cum[r0:r0 + CHUNK]
        last = cum_c[CHUNK - 1:CHUNK]
        st = hst_scr[...]
        qe = (q[r0:r0 + CHUNK] * jnp.exp(cum_c)).astype(BF16)
        o_inter = _dot_nt(qe, st.astype(BF16))
        oh_scr[r0:r0 + CHUNK, :] += o_inter
        khat = (k[r0:r0 + CHUNK] * jnp.exp(last - cum_c)).astype(BF16)
        upd = _dot_tn(v_bf[r0:r0 + CHUNK], khat)
        hst_scr[...] = st * jnp.exp(last) + maskbd * upd
        for blk in range(1, CHUNK // SUB):
            r = r0 + blk * SUB
            c_ref = cum[r - 1:r]
            qt = q[r:r + SUB] * jnp.exp(jnp.minimum(cum[r:r + SUB] - c_ref, 0.0))
            kt = k[r0:r] * jnp.exp(jnp.minimum(c_ref - cum[r0:r], 0.0))
            qbd = jnp.concatenate([qt * hmask[h] for h in range(N_HEADS)], axis=0).astype(BF16)
            sc = _dot_nt(qbd, kt.astype(BF16))
            of = _dot(sc.astype(BF16), v_bf[r0:r])
            o_off = of[0:SUB] * hmask[0]
            for h in range(1, N_HEADS):
                o_off = o_off + of[h * SUB:(h + 1) * SUB] * hmask[h]
            oh_scr[r:r + SUB, :] += o_off
    o_h = oh_scr[...]
    ms = head_sum(o_h * o_h) * (1.0 / HEAD_DIM)
    oa_ref[...] = (o_h * lax.rsqrt(ms + EPS) * vec(V_HNORM) * _silu(slot(3))).astype(oa_ref.dtype)

    ux = slot(4)
    xcv_scr[PAD:PAD + tc, :] = ux
    xc = vec(V_CB)
    for j in range(CONV_WIDTH):
        off = PAD - (CONV_WIDTH - 1) + j
        xc = xc + vec(V_CW0 + j) * xcv_scr[off:off + tc, :]
    xc_bf = xc.astype(BF16)
    rg = _sigmoid(_dot(xc_bf, wr_ref[...]) + vec(V_BR))
    ig = _sigmoid(_dot(xc_bf, wi_ref[...]) + vec(V_BI))
    log_a = -LRU_C * rg * _softplus(-vec(V_LAM))
    a_cum = jnp.exp(log_a)
    b_cum = jnp.sqrt(jnp.maximum(_neg_expm1(2.0 * log_a), 0.0)) * (ig * xc)
    shift = 1
    while shift < tc:
        keep = row >= shift
        a_sh = pltpu.roll(a_cum, shift, 0)
        b_sh = pltpu.roll(b_cum, shift, 0)
        b_cum = jnp.where(keep, a_cum * b_sh + b_cum, b_cum)
        a_cum = jnp.where(keep, a_cum * a_sh, a_cum)
        shift *= 2
    hseq = a_cum * hl_scr[0:1, :] + b_cum
    hl_scr[0:1, :] = hseq[tc - 1:tc]
    ob_ref[...] = (hseq * jax.nn.gelu(slot(5))).astype(ob_ref.dtype)
    xcv_scr[0:PAD, :] = xcv_scr[tc:tc + PAD, :]

    cosv, sinv = cos_ref[...], sin_ref[...]
    cq, ck = slot(6), slot(7)
    qr = cq * cosv + _swap_halves(cq) * sinv
    kr = (ck * cosv + _swap_halves(ck) * sinv) * (HEAD_DIM ** -0.5)
    rv_bf = slot(8).astype(BF16)
    qdec, kdec, sdec, dmat = qdec_ref[...], kdec_ref[...], sdec_ref[...], dmat_ref[...]
    for n in range(nchunk):
        r0 = n * CHUNK
        q_c, k_c, v_c = qr[r0:r0 + CHUNK], kr[r0:r0 + CHUNK], rv_bf[r0:r0 + CHUNK]
        qbd = jnp.concatenate([q_c * hmask[h] for h in range(N_HEADS)], axis=0).astype(BF16)
        sc = _dot_nt(qbd, k_c.astype(BF16)) * dmat
        of = _dot(sc.astype(BF16), v_c)
        o_c = of[0:CHUNK] * hmask[0]
        for h in range(1, N_HEADS):
            o_c = o_c + of[h * CHUNK:(h + 1) * CHUNK] * hmask[h]
        s_ret = rst_scr[...]
        o_c = o_c + _dot((q_c * qdec).astype(BF16), s_ret.astype(BF16))
        orr_scr[r0:r0 + CHUNK, :] = o_c
        upd = _dot_tn((k_c * kdec).astype(BF16), v_c)
        rst_scr[...] = s_ret * sdec + maskbd * upd
    o_r = orr_scr[...]
    mu = head_sum(o_r) * (1.0 / HEAD_DIM)
    dev = o_r - mu
    var = head_sum(dev * dev) * (1.0 / HEAD_DIM)
    oc_ref[...] = (dev * lax.rsqrt(var + EPS) * vec(V_RNORM) * _silu(slot(9))).astype(oc_ref.dtype)

    ud = slot(10)
    xpl_scr[PAD:PAD + tc, :] = ud
    wsum = ud
    sums = []
    nxt = 1
    for win in POOL_WINDOWS:
        while nxt < win:
            wsum = wsum + xpl_scr[PAD - nxt:PAD - nxt + tc, :]
            nxt += 1
        sums.append(wsum)
    grp = lane // (WIDTH // len(POOL_WINDOWS))
    sel = sums[-1]
    winl = jnp.full((1, WIDTH), float(POOL_WINDOWS[-1]), F32)
    for gi in range(len(POOL_WINDOWS) - 2, -1, -1):
        sel = jnp.where(grp == gi, sums[gi], sel)
        winl = jnp.where(grp == gi, float(POOL_WINDOWS[gi]), winl)
    pos = (row + t_idx * tc).astype(F32) + float(pos0)
    pooled = sel / jnp.minimum(winl, pos + 1.0)
    od = _dot((pooled - ud).astype(BF16), wp_ref[...]) * vec(V_PSCALE)
    od_ref[...] = od.astype(od_ref.dtype)
    xpl_scr[0:PAD, :] = xpl_scr[tc:tc + PAD, :]

    @pl.when(t_idx == pl.num_programs(1) - 1)
    def _():
        sh_ref[...] = hst_scr[...]
        sr_ref[...] = rst_scr[...]
        sl_ref[...] = hl_scr[0:1, :]
        scv_ref[...] = xcv_scr[PAD + tc - (CONV_WIDTH - 1):PAD + tc, :]
        spl_ref[...] = xpl_scr[PAD + tc - POOL_BUF:PAD + tc, :]


def _mixer_prompt(u, lb_logits, vecs, wr_bd, wi_bd, wp_bd, cos_t, sin_t, layer, *, batch, seq, tc, pos0):
    nt = seq // tc
    ind = jnp.asarray(_head_block_ones(), BF16)
    cid = np.arange(tc) // CHUNK
    tri = jnp.asarray((cid[:, None] == cid[None, :]) & (np.arange(tc)[:, None] >= np.arange(tc)[None, :]), BF16)
    qdec, kdec, sdec, dmat = _retention_tables()
    depth = lb_logits.shape[0]

    def const(shape):
        return pl.BlockSpec(shape, lambda b, t: tuple(0 for _ in shape))

    row_spec = pl.BlockSpec((tc, WIDTH), lambda b, t: (b * nt + t, 0))
    tab_spec = pl.BlockSpec((tc, WIDTH), lambda b, t: (t, 0))

    def state_spec(rows):
        return pl.BlockSpec((None, rows, WIDTH), lambda b, t: (b, 0, 0))

    n = batch * seq
    outs = pl.pallas_call(
        functools.partial(_mixer_prompt_kernel, layer=layer, tc=tc, pos0=pos0),
        grid=(batch, nt),
        in_specs=[
            pl.BlockSpec((tc, MIX_COLS), lambda b, t: (b * nt + t, 0)),
            const((depth, WIDTH)),
            pl.BlockSpec((None, N_VEC_ROWS, WIDTH), lambda b, t: (layer, 0, 0)),
            pl.BlockSpec((None, WIDTH, WIDTH), lambda b, t: (layer, 0, 0)),
            pl.BlockSpec((None, WIDTH, WIDTH), lambda b, t: (layer, 0, 0)),
            pl.BlockSpec((None, WIDTH, WIDTH), lambda b, t: (layer, 0, 0)),
            tab_spec, tab_spec,
            const((WIDTH, WIDTH)), const((tc, tc)),
            const((CHUNK, WIDTH)), const((CHUNK, WIDTH)), const((1, WIDTH)),
            const((N_HEADS * CHUNK, CHUNK)),
        ],
        out_specs=[row_spec, row_spec, row_spec, row_spec,
                   state_spec(WIDTH), state_spec(1), state_spec(CONV_WIDTH - 1), state_spec(WIDTH),
                   state_spec(POOL_BUF)],
        out_shape=[jax.ShapeDtypeStruct((n, WIDTH), BF16)] * 4 + [
            jax.ShapeDtypeStruct((batch, WIDTH, WIDTH), F32),
            jax.ShapeDtypeStruct((batch, 1, WIDTH), F32),
            jax.ShapeDtypeStruct((batch, CONV_WIDTH - 1, WIDTH), F32),
            jax.ShapeDtypeStruct((batch, WIDTH, WIDTH), F32),
            jax.ShapeDtypeStruct((batch, POOL_BUF, WIDTH), F32),
        ],
        scratch_shapes=[
            pltpu.VMEM((WIDTH, WIDTH), F32), pltpu.VMEM((WIDTH, WIDTH), F32), pltpu.VMEM((8, WIDTH), F32),
            pltpu.VMEM((PAD + tc, WIDTH), F32), pltpu.VMEM((PAD + tc, WIDTH), F32),
            pltpu.VMEM((PAD + tc, WIDTH), F32), pltpu.VMEM((PAD + tc, WIDTH), F32),
            pltpu.VMEM((PAD + tc, WIDTH), F32),
            pltpu.VMEM((tc, WIDTH), F32), pltpu.VMEM((tc, WIDTH), F32),
        ],
        compiler_params=_cparams("parallel", "arbitrary"),
        name="mixer_prompt",
    )(u, lb_logits, vecs, wr_bd, wi_bd, wp_bd, cos_t, sin_t, ind, tri, qdec, kdec, sdec, dmat)
    return outs


def _diag_blocks(s):
    return jnp.stack([s[:, h * HEAD_DIM:(h + 1) * HEAD_DIM, h * HEAD_DIM:(h + 1) * HEAD_DIM]
                      for h in range(N_HEADS)], axis=1)


def _mixer_sample_kernel(u_ref, u3_ref, ut_ref, lbl_ref, lblt_ref, vec_ref, vec3_ref, wr_ref, wi_ref,
                         wp_ref, cost_ref, sint_ref, sh_ref, sl_ref, scv_ref, sr_ref, spl_ref,
                         oa_ref, ob_ref, oc_ref, od_ref, nh_ref, nl_ref, ncv_ref, nr_ref, npl_ref,
                         *, layer, bb, pos0):
    def slot(s):
        return u_ref[:, s * WIDTH:(s + 1) * WIDTH]

    def vec(r):
        return vec_ref[r:r + 1, :]

    lblt = lblt_ref[...]
    m = jnp.max(lblt, axis=1, keepdims=True)
    e = jnp.exp(lblt - m)
    soft = e / jnp.sum(e, axis=1, keepdims=True)
    acc = soft[:, 0:1]
    for l in range(1, layer + 1):
        acc = acc + soft[:, l:l + 1]
    lb_col = acc - soft[:, 0:1]
    aq_t = _silu(ut_ref[0 * WIDTH:1 * WIDTH, :])
    z_t = ut_ref[1 * WIDTH:2 * WIDTH, :]
    f_t = lb_col + (1.0 - lb_col) * _sigmoid(z_t)
    f_t = jnp.maximum(f_t, F_FLOOR)
    k_t = (1.0 - lb_col) * _sigmoid(-z_t)

    def swap_rows(x):
        half = HEAD_DIM // 2
        parts = []
        for h in range(N_HEADS):
            parts.append(x[h * HEAD_DIM + half:(h + 1) * HEAD_DIM])
            parts.append(x[h * HEAD_DIM:h * HEAD_DIM + half])
        return jnp.concatenate(parts, axis=0)

    cost, sint = cost_ref[...], sint_ref[...]
    cq_t, ck_t = ut_ref[2 * WIDTH:3 * WIDTH, :], ut_ref[3 * WIDTH:4 * WIDTH, :]
    rq_t = cq_t * cost + swap_rows(cq_t) * sint
    rk_t = (ck_t * cost + swap_rows(ck_t) * sint) * (HEAD_DIM ** -0.5)
    gam = [float(np.exp(g)) for g in _log_gamma()]

    for b in range(bb):
        for h in range(N_HEADS):
            rows = slice(h * HEAD_DIM, (h + 1) * HEAD_DIM)
            s_new = f_t[rows, b:b + 1] * sh_ref[b, h] + k_t[rows, b:b + 1] * u3_ref[b, 2 * N_HEADS + h:2 * N_HEADS + h + 1, :]
            nh_ref[b, h] = s_new
            oa_ref[b, h:h + 1, :] = jnp.sum(aq_t[rows, b:b + 1] * s_new, axis=0, keepdims=True)
            r_new = gam[h] * sr_ref[b, h] + rk_t[rows, b:b + 1] * u3_ref[b, 8 * N_HEADS + h:8 * N_HEADS + h + 1, :]
            nr_ref[b, h] = r_new
            oc_ref[b, h:h + 1, :] = jnp.sum(rq_t[rows, b:b + 1] * r_new, axis=0, keepdims=True)

    o_a = oa_ref[...]
    ms = jnp.mean(o_a * o_a, axis=-1, keepdims=True)
    ug3 = u3_ref[:, 3 * N_HEADS:4 * N_HEADS, :]
    oa_ref[...] = o_a * lax.rsqrt(ms + EPS) * vec3_ref[0] * _silu(ug3)
    o_c = oc_ref[...]
    mu = jnp.mean(o_c, axis=-1, keepdims=True)
    dev = o_c - mu
    var = jnp.mean(dev * dev, axis=-1, keepdims=True)
    cg3 = u3_ref[:, 9 * N_HEADS:10 * N_HEADS, :]
    oc_ref[...] = dev * lax.rsqrt(var + EPS) * vec3_ref[1] * _silu(cg3)

    ux = slot(4)
    xc = vec(V_CB) + vec(V_CW0 + CONV_WIDTH - 1) * ux
    for j in range(CONV_WIDTH - 1):
        xc = xc + vec(V_CW0 + j) * scv_ref[:, j * WIDTH:(j + 1) * WIDTH]
    xc_bf = xc.astype(BF16)
    rg = _sigmoid(_dot(xc_bf, wr_ref[...]) + vec(V_BR))
    ig = _sigmoid(_dot(xc_bf, wi_ref[...]) + vec(V_BI))
    log_a = -LRU_C * rg * _softplus(-vec(V_LAM))
    a = jnp.exp(log_a)
    b_in = jnp.sqrt(jnp.maximum(_neg_expm1(2.0 * log_a), 0.0)) * (ig * xc)
    hnew = a * sl_ref[...] + b_in
    nl_ref[...] = hnew
    ob_ref[...] = hnew * jax.nn.gelu(slot(5))
    ncv_ref[:, 0:(CONV_WIDTH - 2) * WIDTH] = scv_ref[:, WIDTH:(CONV_WIDTH - 1) * WIDTH]
    ncv_ref[:, (CONV_WIDTH - 2) * WIDTH:] = ux

    ud = slot(10)
    wsum = ud
    sums = []
    nxt = 1
    for win in POOL_WINDOWS:
        while nxt < win:
            wsum = wsum + spl_ref[:, (POOL_BUF - nxt) * WIDTH:(POOL_BUF - nxt + 1) * WIDTH]
            nxt += 1
        sums.append(wsum)
    lane = lax.broadcasted_iota(jnp.int32, (1, WIDTH), 1)
    grp = lane // (WIDTH // len(POOL_WINDOWS))
    sel = sums[-1]
    winl = jnp.full((1, WIDTH), float(POOL_WINDOWS[-1]), F32)
    for gi in range(len(POOL_WINDOWS) - 2, -1, -1):
        sel = jnp.where(grp == gi, sums[gi], sel)
        winl = jnp.where(grp == gi, float(POOL_WINDOWS[gi]), winl)
    pooled = sel / jnp.minimum(winl, float(pos0) + 1.0)
    od_ref[...] = _dot((pooled - ud).astype(BF16), wp_ref[...]) * vec(V_PSCALE)
    npl_ref[:, 0:(POOL_BUF - 1) * WIDTH] = spl_ref[:, WIDTH:POOL_BUF * WIDTH]
    npl_ref[:, (POOL_BUF - 1) * WIDTH:] = ud


def _mixer_sample(u, lb_logits, vecs, vecs3, wr_bd, wi_bd, wp_bd, cos_t, sin_t, s_hgrn, s_lru, s_conv,
                  s_ret, s_pool, layer, *, bb, pos0):
    nb = u.shape[0]
    ng = nb // bb
    depth = lb_logits.shape[0]
    u3 = u.reshape(nb, N_MIX_SLOTS * N_HEADS, HEAD_DIM)
    cols = jnp.concatenate([u[:, 0:2 * WIDTH], u[:, 6 * WIDTH:8 * WIDTH]], axis=1)
    ut = cols.reshape(ng, bb, 4 * WIDTH).transpose(0, 2, 1)
    cost = jnp.broadcast_to(cos_t.reshape(WIDTH, 1), (WIDTH, bb))
    sint = jnp.broadcast_to(sin_t.reshape(WIDTH, 1), (WIDTH, bb))
    conv2 = s_conv.reshape(nb, (CONV_WIDTH - 1) * WIDTH)
    pool2 = s_pool.reshape(nb, POOL_BUF * WIDTH)

    def const(shape):
        return pl.BlockSpec(shape, lambda i: tuple(0 for _ in shape))

    def rows(width):
        return pl.BlockSpec((bb, width), lambda i: (i, 0))

    st4 = pl.BlockSpec((bb, N_HEADS, HEAD_DIM, HEAD_DIM), lambda i: (i, 0, 0, 0))
    o3 = pl.BlockSpec((bb, N_HEADS, HEAD_DIM), lambda i: (i, 0, 0))
    outs = pl.pallas_call(
        functools.partial(_mixer_sample_kernel, layer=layer, bb=bb, pos0=pos0),
        grid=(ng,),
        in_specs=[
            rows(MIX_COLS),
            pl.BlockSpec((bb, N_MIX_SLOTS * N_HEADS, HEAD_DIM), lambda i: (i, 0, 0)),
            pl.BlockSpec((None, 4 * WIDTH, bb), lambda i: (i, 0, 0)),
            const((depth, WIDTH)), const((WIDTH, depth)),
            pl.BlockSpec((None, N_VEC_ROWS, WIDTH), lambda i: (layer, 0, 0)),
            pl.BlockSpec((None, 2, N_HEADS, HEAD_DIM), lambda i: (layer, 0, 0, 0)),
            pl.BlockSpec((None, WIDTH, WIDTH), lambda i: (layer, 0, 0)),
            pl.BlockSpec((None, WIDTH, WIDTH), lambda i: (layer, 0, 0)),
            pl.BlockSpec((None, WIDTH, WIDTH), lambda i: (layer, 0, 0)),
            const((WIDTH, bb)), const((WIDTH, bb)),
            st4, rows(WIDTH), rows((CONV_WIDTH - 1) * WIDTH), st4, rows(POOL_BUF * WIDTH),
        ],
        out_specs=[o3, rows(WIDTH), o3, rows(WIDTH),
                   st4, rows(WIDTH), rows((CONV_WIDTH - 1) * WIDTH), st4, rows(POOL_BUF * WIDTH)],
        out_shape=[
            jax.ShapeDtypeStruct((nb, N_HEADS, HEAD_DIM), F32), jax.ShapeDtypeStruct((nb, WIDTH), F32),
            jax.ShapeDtypeStruct((nb, N_HEADS, HEAD_DIM), F32), jax.ShapeDtypeStruct((nb, WIDTH), F32),
            jax.ShapeDtypeStruct(s_hgrn.shape, F32), jax.ShapeDtypeStruct((nb, WIDTH), F32),
            jax.ShapeDtypeStruct(conv2.shape, F32), jax.ShapeDtypeStruct(s_ret.shape, F32),
            jax.ShapeDtypeStruct(pool2.shape, F32),
        ],
        compiler_params=_cparams("parallel"),
        name="mixer_sample",
    )(u, u3, ut, lb_logits, lb_logits.T, vecs, vecs3, wr_bd, wi_bd, wp_bd, cost, sint,
      s_hgrn, s_lru, conv2, s_ret, pool2)
    o_a, o_b, o_c, o_d, n_h, n_l, n_cv, n_r, n_pl = outs
    return (o_a.reshape(nb, WIDTH), o_b, o_c.reshape(nb, WIDTH), o_d, n_h, n_l,
            n_cv.reshape(s_conv.shape), n_r, n_pl.reshape(s_pool.shape))


def _pick_tile(n, pref):
    t = min(n, pref)
    while n % t:
        t //= 2
    return t


def kernel(x_prompt, x_sample, state_hgrn, state_rglru, state_conv, state_retention, state_pool, lb_logits, ffn1_norm, ffn1_up, ffn1_down, mix_norm, w_in, hgrn_norm, conv_w, conv_b, w_rgate, b_rgate, w_igate, b_igate, lru_lambda, ret_norm, w_pool, pool_scale, w_branch, w_o, ffn2_norm, ffn2_up, ffn2_down, final_norm):
    batch, seq, d = x_prompt.shape
    nb, dec_seq, _ = x_sample.shape
    assert dec_seq == 1
    depth = w_in.shape[0]
    past_len = PAST_LEN
    dff = ffn1_down.shape[1]

    up1, dn1 = ffn1_up.astype(BF16), ffn1_down.astype(BF16)
    up2, dn2 = ffn2_up.astype(BF16), ffn2_down.astype(BF16)
    w_mix = w_in[:, :, :MIX_COLS].astype(BF16)
    w_gate = w_in[:, :, MIX_COLS:].astype(BF16)
    w_br, w_out = w_branch.astype(BF16), w_o.astype(BF16)
    wr_bd = jax.vmap(_block_diag)(w_rgate).astype(BF16)
    wi_bd = jax.vmap(_block_diag)(w_igate).astype(BF16)
    wp_bd = jax.vmap(_block_diag)(w_pool).astype(BF16)
    vecs = jnp.concatenate(
        [hgrn_norm[:, None], conv_w, conv_b[:, None], b_rgate[:, None], b_igate[:, None],
         lru_lambda[:, None], ret_norm[:, None], pool_scale[:, None],
         jnp.zeros((depth, N_VEC_ROWS - 11, WIDTH), F32)], axis=1)
    vecs3 = jnp.stack([hgrn_norm, ret_norm], axis=1).reshape(depth, 2, N_HEADS, HEAD_DIM)
    n1 = ffn1_norm[:, None, :]
    n2 = ffn2_norm[:, None, :]
    nm = mix_norm[:, None, :]
    fg = final_norm[None, :]

    cos_p, sin_p = _rope_tables(jnp.arange(seq, dtype=F32))
    cos_s, sin_s = _rope_tables(past_len + jnp.arange(dec_seq, dtype=F32))

    tf = 256 if dff % 256 == 0 else dff
    xp = x_prompt.reshape(batch * seq, d)
    xs = x_sample.reshape(nb * dec_seq, d)
    tm_p = _pick_tile(batch * seq, 1024)
    tm_s = _pick_tile(nb, 128)
    tc = _pick_tile(seq, 256)

    st_p = [[] for _ in range(5)]
    st_s = [[] for _ in range(5)]
    for l in range(depth):
        last = l == depth - 1
        xp = _ffn(xp, n1, up1, dn1, l, fg, final_norm=False, tm=tm_p, tf=tf)
        u_p = _inproj(xp, nm, w_mix, l, tm=_pick_tile(batch * seq, 512))
        oa, ob, oc, od, sh, sl, scv, sr, spl = _mixer_prompt(
            u_p, lb_logits, vecs, wr_bd, wi_bd, wp_bd, cos_p, sin_p, l, batch=batch, seq=seq, tc=tc, pos0=0)
        xp = _merge(xp, nm, (oa, ob, oc, od), w_gate, w_br, w_out, l, tm=_pick_tile(batch * seq, 256))
        xp = _ffn(xp, n2, up2, dn2, l, fg, final_norm=last, tm=tm_p, tf=tf)
        for lst, s in zip(st_p, (jnp.swapaxes(_diag_blocks(sh), 2, 3), sl[:, 0], scv, _diag_blocks(sr), spl)):
            lst.append(s)
        xs = _ffn(xs, n1, up1, dn1, l, fg, final_norm=False, tm=tm_s, tf=tf)
        u_s = _inproj(xs, nm, w_mix, l, tm=tm_s)
        oa, ob, oc, od, nh, nl, ncv, nr, npl = _mixer_sample(
            u_s, lb_logits, vecs, vecs3, wr_bd, wi_bd, wp_bd, cos_s, sin_s, state_hgrn[l], state_rglru[l],
            state_conv[l], state_retention[l], state_pool[l], l, bb=8, pos0=past_len)
        xs = _merge(xs, nm, (oa, ob, oc, od), w_gate, w_br, w_out, l, tm=tm_s)
        xs = _ffn(xs, n2, up2, dn2, l, fg, final_norm=last, tm=tm_s, tf=tf)
        for lst, s in zip(st_s, (nh, nl, ncv, nr, npl)):
            lst.append(s)

    y_p = xp.reshape(batch, seq, d)
    y_s = xs.reshape(nb, dec_seq, d)
    return (y_p, y_s) + tuple(jnp.stack(s) for s in st_p) + tuple(jnp.stack(s) for s in st_s)
```

```python
import functools

import numpy as np
import jax
import jax.numpy as jnp
from jax import lax
from jax.experimental import pallas as pl
from jax.experimental.pallas import tpu as pltpu

F32 = jnp.float32
BF16 = jnp.bfloat16

N_HEADS = 4
HEAD_DIM = 64
WIDTH = N_HEADS * HEAD_DIM
N_MIX_SLOTS = 11
MIX_COLS = N_MIX_SLOTS * WIDTH
CHUNK = 64
SUB = 16
PAD = 16
CONV_WIDTH = 4
LRU_C = 8.0
POOL_WINDOWS = (2, 4, 8, 16)
POOL_BUF = 15
ROPE_BASE = 10000.0
EPS = 1e-6
F_FLOOR = 1e-30
PAST_LEN = 16384
VMEM_LIMIT = 56 * 1024 * 1024
FFN_ROWS = 512


def _cparams(*sem):
    return pltpu.CompilerParams(dimension_semantics=sem, vmem_limit_bytes=VMEM_LIMIT)


def _dot(a, b):
    return jnp.dot(a, b, preferred_element_type=F32)


def _dot_nt(a, b):
    return lax.dot_general(a, b, (((1,), (1,)), ((), ())), preferred_element_type=F32)


def _dot_tn(a, b):
    return lax.dot_general(a, b, (((0,), (0,)), ((), ())), preferred_element_type=F32)


def _rms(x, g):
    return x * lax.rsqrt(jnp.mean(x * x, axis=-1, keepdims=True) + EPS) * g


def _sigmoid(x):
    return jax.nn.sigmoid(x)


def _silu(x):
    return x * jax.nn.sigmoid(x)


def _softplus(x):
    return jnp.maximum(x, 0.0) + jnp.log1p(jnp.exp(-jnp.abs(x)))


def _neg_expm1(y):
    t = jnp.tanh(0.5 * y)
    return -2.0 * t / (1.0 - t)


def _split2(x):
    hi = x.astype(BF16)
    lo = (x - hi.astype(F32)).astype(BF16)
    return hi, lo


def _split3(x):
    hi = x.astype(BF16)
    r = x - hi.astype(F32)
    mid = r.astype(BF16)
    lo = (r - mid.astype(F32)).astype(BF16)
    return hi, mid, lo


def _lower_bound(lbl, layer):
    m = jnp.max(lbl, axis=0, keepdims=True)
    e = jnp.exp(lbl - m)
    soft = e / jnp.sum(e, axis=0, keepdims=True)
    acc = soft[0:1]
    for l in range(1, layer + 1):
        acc = acc + soft[l:l + 1]
    return acc - soft[0:1]


def _ffn_kernel(x_ref, g_ref, wu_ref, wd_ref, fg_ref, o_ref, *, final_norm):
    x = x_ref[...]
    dff = wd_ref.shape[0]
    h = _rms(x, g_ref[...]).astype(BF16)
    a = _dot(h, wu_ref[:, :dff])
    b = _dot(h, wu_ref[:, dff:])
    act = (_silu(a) * b).astype(BF16)
    y = x + 0.5 * _dot(act, wd_ref[...])
    if final_norm:
        y = _rms(y, fg_ref[...])
    o_ref[...] = y


def _resident(shape, index_map):
    return pl.BlockSpec(shape, index_map, pipeline_mode=pl.Buffered(1))


def _ffn(x, g, w_up, w_down, layer, final_g, *, final_norm, tm):
    n, d = x.shape
    dff = w_down.shape[1]
    return pl.pallas_call(
        functools.partial(_ffn_kernel, final_norm=final_norm),
        grid=(n // tm,),
        in_specs=[
            pl.BlockSpec((tm, d), lambda i: (i, 0)),
            _resident((None, 1, d), lambda i: (layer, 0, 0)),
            _resident((None, d, 2 * dff), lambda i: (layer, 0, 0)),
            _resident((None, dff, d), lambda i: (layer, 0, 0)),
            _resident((1, d), lambda i: (0, 0)),
        ],
        out_specs=pl.BlockSpec((tm, d), lambda i: (i, 0)),
        out_shape=jax.ShapeDtypeStruct((n, d), F32),
        compiler_params=_cparams("parallel"),
        name="ffn",
    )(x, g, w_up, w_down, final_g)


def _inproj_kernel(x_ref, g_ref, w_ref, o_ref):
    h = _rms(x_ref[...], g_ref[...]).astype(BF16)
    o_ref[...] = _dot(h, w_ref[...])


def _inproj(x, g, w_mix, layer, *, tm):
    n, d = x.shape
    cols = w_mix.shape[2]
    return pl.pallas_call(
        _inproj_kernel,
        grid=(n // tm,),
        in_specs=[
            pl.BlockSpec((tm, d), lambda i: (i, 0)),
            _resident((None, 1, d), lambda i: (layer, 0, 0)),
            _resident((None, d, cols), lambda i: (layer, 0, 0)),
        ],
        out_specs=pl.BlockSpec((tm, cols), lambda i: (i, 0)),
        out_shape=jax.ShapeDtypeStruct((n, cols), F32),
        compiler_params=_cparams("parallel"),
        name="inproj",
    )(x, g, w_mix)


def _merge_kernel(x_ref, g_ref, oa_ref, ob_ref, oc_ref, od_ref, wg_ref, wb_ref, wo_ref, o_ref):
    x = x_ref[...]
    d = x.shape[1]
    h = _rms(x, g_ref[...]).astype(BF16)
    merged = None
    for b, br_ref in enumerate((oa_ref, ob_ref, oc_ref, od_ref)):
        gate = _sigmoid(_dot(h, wg_ref[:, b * d:(b + 1) * d]))
        y = _dot(br_ref[...].astype(BF16), wb_ref[b])
        merged = gate * y if merged is None else merged + gate * y
    o_ref[...] = x + _dot(merged.astype(BF16), wo_ref[...])


def _merge(x, g, branches, w_gate, w_branch, w_o, layer, *, tm):
    n, d = x.shape
    w = branches[0].shape[1]
    nb = len(branches)
    return pl.pallas_call(
        _merge_kernel,
        grid=(n // tm,),
        in_specs=[
            pl.BlockSpec((tm, d), lambda i: (i, 0)),
            _resident((None, 1, d), lambda i: (layer, 0, 0)),
        ] + [pl.BlockSpec((tm, w), lambda i: (i, 0)) for _ in range(nb)] + [
            _resident((None, d, nb * d), lambda i: (layer, 0, 0)),
            _resident((None, nb, w, d), lambda i: (layer, 0, 0, 0)),
            _resident((None, d, d), lambda i: (layer, 0, 0)),
        ],
        out_specs=pl.BlockSpec((tm, d), lambda i: (i, 0)),
        out_shape=jax.ShapeDtypeStruct((n, d), F32),
        compiler_params=_cparams("parallel"),
        name="merge",
    )(x, g, *branches, w_gate, w_branch, w_o)


def _head_block_ones():
    h = np.arange(WIDTH) // HEAD_DIM
    return (h[:, None] == h[None, :]).astype(np.float32)


def _log_gamma():
    return np.log1p(-(2.0 ** (-5.0 - np.arange(N_HEADS, dtype=np.float64))))


def _retention_tables():
    lg = np.repeat(_log_gamma(), HEAD_DIM)[None, :]
    t = np.arange(CHUNK, dtype=np.float64)[:, None]
    q_dec = np.exp((t + 1.0) * lg)
    k_dec = np.exp((CHUNK - 1.0 - t) * lg)
    s_dec = np.exp(CHUNK * lg)
    dt = t - t.T
    dmat = np.concatenate(
        [np.where(dt >= 0, np.exp(dt * g), 0.0) for g in _log_gamma()], axis=0)
    return (jnp.asarray(q_dec, F32), jnp.asarray(k_dec, F32), jnp.asarray(s_dec, F32),
            jnp.asarray(dmat, F32))


def _rope_tables(pos):
    half = HEAD_DIM // 2
    freq = ROPE_BASE ** (-jnp.arange(half, dtype=F32) / half)
    ang = pos[:, None] * freq[None, :]
    cos, sin = jnp.cos(ang), jnp.sin(ang)
    cos_h = jnp.concatenate([cos, cos], axis=-1)
    sin_h = jnp.concatenate([-sin, sin], axis=-1)
    return jnp.tile(cos_h, (1, N_HEADS)), jnp.tile(sin_h, (1, N_HEADS))


def _block_diag(w):
    h, dh, _ = w.shape
    eye = jnp.eye(h, dtype=w.dtype)
    return (eye[:, None, :, None] * w[:, :, None, :]).reshape(h * dh, h * dh)


V_HNORM, V_CW0, V_CB, V_BR, V_BI, V_LAM, V_RNORM, V_PSCALE = 0, 1, 5, 6, 7, 8, 9, 10
N_VEC_ROWS = 16


def _swap_halves(x):
    half = HEAD_DIM // 2
    lane = lax.broadcasted_iota(jnp.int32, (1, 128), 1)
    first = (lane % HEAD_DIM) < half
    parts = []
    for c in range(x.shape[1] // 128):
        xc = x[:, c * 128:(c + 1) * 128]
        parts.append(jnp.where(first, pltpu.roll(xc, 128 - half, 1), pltpu.roll(xc, half, 1)))
    return jnp.concatenate(parts, axis=1)


def _mixer_prompt_kernel(u_ref, lbl_ref, vec_ref, wr_ref, wi_ref, wp_ref, cos_ref, sin_ref,
                         ind_ref, tri_ref, qdec_ref, kdec_ref, sdec_ref, dmat_ref,
                         oa_ref, ob_ref, oc_ref, od_ref, sh_ref, sl_ref, scv_ref, sr_ref, spl_ref,
                         hst_scr, rst_scr, hl_scr, xcv_scr, xpl_scr, kp_scr, cp_scr, vp_scr, oh_scr,
                         orr_scr, *, layer, tc, pos0):
    t_idx = pl.program_id(1)
    nchunk = tc // CHUNK

    @pl.when(t_idx == 0)
    def _():
        hst_scr[...] = jnp.zeros_like(hst_scr)
        rst_scr[...] = jnp.zeros_like(rst_scr)
        hl_scr[...] = jnp.zeros_like(hl_scr)
        xcv_scr[0:PAD, :] = jnp.zeros((PAD, WIDTH), F32)
        xpl_scr[0:PAD, :] = jnp.zeros((PAD, WIDTH), F32)
        kp_scr[0:PAD, :] = jnp.zeros((PAD, WIDTH), F32)
        cp_scr[0:PAD, :] = jnp.zeros((PAD, WIDTH), F32)
        vp_scr[0:PAD, :] = jnp.zeros((PAD, WIDTH), F32)

    def slot(s):
        return u_ref[:, s * WIDTH:(s + 1) * WIDTH]

    def vec(r):
        return vec_ref[r:r + 1, :]

    ind = ind_ref[...]
    maskbd = ind.astype(F32)
    lane = lax.broadcasted_iota(jnp.int32, (1, WIDTH), 1)
    hmask = [(lane // HEAD_DIM == h).astype(F32) for h in range(N_HEADS)]
    row = lax.broadcasted_iota(jnp.int32, (tc, WIDTH), 0)

    def head_sum(x):
        hi, lo = _split2(x)
        return _dot(hi, ind) + _dot(lo, ind)

    lb = _lower_bound(lbl_ref[...], layer)
    uq, z, v = slot(0), slot(1), slot(2)
    q = _silu(uq)
    f = lb + (1.0 - lb) * _sigmoid(z)
    logf = jnp.log(jnp.maximum(f, F_FLOOR))
    k = (1.0 - lb) * _sigmoid(-z)
    tri = tri_ref[...]
    l_hi, l_mid, l_lo = _split3(logf)
    cum = _dot(tri, l_hi) + _dot(tri, l_mid) + _dot(tri, l_lo)

    kp_scr[PAD:PAD + tc, :] = k
    cp_scr[PAD:PAD + tc, :] = cum
    vp_scr[PAD:PAD + tc, :] = v
    rowmod = row % SUB
    o_diag = jnp.zeros((tc, WIDTH), F32)
    for dlt in range(SUB):
        kd = kp_scr[PAD - dlt:PAD - dlt + tc, :]
        cd = cp_scr[PAD - dlt:PAD - dlt + tc, :]
        vd = vp_scr[PAD - dlt:PAD - dlt + tc, :]
        w = jnp.where(rowmod >= dlt, q * kd * jnp.exp(jnp.minimum(cum - cd, 0.0)), 0.0)
        o_diag = o_diag + _dot(w.astype(BF16), ind) * vd
    oh_scr[...] = o_diag

    v_bf = v.astype(BF16)
    for n in range(nchunk):
        r0 = n * CHUNK
        cum_c = cum[r0:r0 + CHUNK]
        last = cum_c[CHUNK - 1:CHUNK]
        st = hst_scr[...]
        qe = (q[r0:r0 + CHUNK] * jnp.exp(cum_c)).astype(BF16)
        o_inter = _dot_nt(qe, st.astype(BF16))
        oh_scr[r0:r0 + CHUNK, :] += o_inter
        khat = (k[r0:r0 + CHUNK] * jnp.exp(last - cum_c)).astype(BF16)
        upd = _dot_tn(v_bf[r0:r0 + CHUNK], khat)
        hst_scr[...] = st * jnp.exp(last) + maskbd * upd
        for blk in range(1, CHUNK // SUB):
            r = r0 + blk * SUB
            c_ref = cum[r - 1:r]
            qt = q[r:r + SUB] * jnp.exp(jnp.minimum(cum[r:r + SUB] - c_ref, 0.0))
            kt = k[r0:r] * jnp.exp(jnp.minimum(c_ref - cum[r0:r], 0.0))
            qbd = jnp.concatenate([qt * hmask[h] for h in range(N_HEADS)], axis=0).astype(BF16)
            sc = _dot_nt(qbd, kt.astype(BF16))
            of = _dot(sc.astype(BF16), v_bf[r0:r])
            o_off = of[0:SUB] * hmask[0]
            for h in range(1, N_HEADS):
                o_off = o_off + of[h * SUB:(h + 1) * SUB] * hmask[h]
            oh_scr[r:r + SUB, :] += o_off
    o_h = oh_scr[...]
    ms = head_sum(o_h * o_h) * (1.0 / HEAD_DIM)
    oa_ref[...] = (o_h * lax.rsqrt(ms + EPS) * vec(V_HNORM) * _silu(slot(3))).astype(oa_ref.dtype)

    ux = slot(4)
    xcv_scr[PAD:PAD + tc, :] = ux
    xc = vec(V_CB)
    for j in range(CONV_WIDTH):
        off = PAD - (CONV_WIDTH - 1) + j
        xc = xc + vec(V_CW0 + j) * xcv_scr[off:off + tc, :]
    xc_bf = xc.astype(BF16)
    rg = _sigmoid(_dot(xc_bf, wr_ref[...]) + vec(V_BR))
    ig = _sigmoid(_dot(xc_bf, wi_ref[...]) + vec(V_BI))
    log_a = -LRU_C * rg * _softplus(-vec(V_LAM))
    a_cum = jnp.exp(log_a)
    b_cum = jnp.sqrt(jnp.maximum(_neg_expm1(2.0 * log_a), 0.0)) * (ig * xc)
    shift = 1
    while shift < tc:
        keep = row >= shift
        a_sh = pltpu.roll(a_cum, shift, 0)
        b_sh = pltpu.roll(b_cum, shift, 0)
        b_cum = jnp.where(keep, a_cum * b_sh + b_cum, b_cum)
        a_cum = jnp.where(keep, a_cum * a_sh, a_cum)
        shift *= 2
    hseq = a_cum * hl_scr[0:1, :] + b_cum
    hl_scr[0:1, :] = hseq[tc - 1:tc]
    ob_ref[...] = (hseq * jax.nn.gelu(slot(5))).astype(ob_ref.dtype)
    xcv_scr[0:PAD, :] = xcv_scr[tc:tc + PAD, :]

    cosv, sinv = cos_ref[...], sin_ref[...]
    cq, ck = slot(6), slot(7)
    qr = cq * cosv + _swap_halves(cq) * sinv
    kr = (ck * cosv + _swap_halves(ck) * sinv) * (HEAD_DIM ** -0.5)
    rv_bf = slot(8).astype(BF16)
    qdec, kdec, sdec, dmat = qdec_ref[...], kdec_ref[...], sdec_ref[...], dmat_ref[...]
    for n in range(nchunk):
        r0 = n * CHUNK
        q_c, k_c, v_c = qr[r0:r0 + CHUNK], kr[r0:r0 + CHUNK], rv_bf[r0:r0 + CHUNK]
        qbd = jnp.concatenate([q_c * hmask[h] for h in range(N_HEADS)], axis=0).astype(BF16)
        sc = _dot_nt(qbd, k_c.astype(BF16)) * dmat
        of = _dot(sc.astype(BF16), v_c)
        o_c = of[0:CHUNK] * hmask[0]
        for h in range(1, N_HEADS):
            o_c = o_c + of[h * CHUNK:(h + 1) * CHUNK] * hmask[h]
        s_ret = rst_scr[...]
        o_c = o_c + _dot((q_c * qdec).astype(BF16), s_ret.astype(BF16))
        orr_scr[r0:r0 + CHUNK, :] = o_c
        upd = _dot_tn((k_c * kdec).astype(BF16), v_c)
        rst_scr[...] = s_ret * sdec + maskbd * upd
    o_r = orr_scr[...]
    mu = head_sum(o_r) * (1.0 / HEAD_DIM)
    dev = o_r - mu
    var = head_sum(dev * dev) * (1.0 / HEAD_DIM)
    oc_ref[...] = (dev * lax.rsqrt(var + EPS) * vec(V_RNORM) * _silu(slot(9))).astype(oc_ref.dtype)

    ud = slot(10)
    xpl_scr[PAD:PAD + tc, :] = ud
    wsum = ud
    sums = []
    nxt = 1
    for win in POOL_WINDOWS:
        while nxt < win:
            wsum = wsum + xpl_scr[PAD - nxt:PAD - nxt + tc, :]
            nxt += 1
        sums.append(wsum)
    grp = lane // (WIDTH // len(POOL_WINDOWS))
    sel = sums[-1]
    winl = jnp.full((1, WIDTH), float(POOL_WINDOWS[-1]), F32)
    for gi in range(len(POOL_WINDOWS) - 2, -1, -1):
        sel = jnp.where(grp == gi, sums[gi], sel)
        winl = jnp.where(grp == gi, float(POOL_WINDOWS[gi]), winl)
    pos = (row + t_idx * tc).astype(F32) + float(pos0)
    pooled = sel / jnp.minimum(winl, pos + 1.0)
    od = _dot((pooled - ud).astype(BF16), wp_ref[...]) * vec(V_PSCALE)
    od_ref[...] = od.astype(od_ref.dtype)
    xpl_scr[0:PAD, :] = xpl_scr[tc:tc + PAD, :]

    @pl.when(t_idx == pl.num_programs(1) - 1)
    def _():
        sh_ref[...] = hst_scr[...]
        sr_ref[...] = rst_scr[...]
        sl_ref[...] = hl_scr[0:1, :]
        scv_ref[...] = xcv_scr[PAD + tc - (CONV_WIDTH - 1):PAD + tc, :]
        spl_ref[...] = xpl_scr[PAD + tc - POOL_BUF:PAD + tc, :]


def _mixer_prompt(u, lb_logits, vecs, wr_bd, wi_bd, wp_bd, cos_t, sin_t, layer, *, batch, seq, tc, pos0):
    nt = seq // tc
    ind = jnp.asarray(_head_block_ones(), BF16)
    cid = np.arange(tc) // CHUNK
    tri = jnp.asarray((cid[:, None] == cid[None, :]) & (np.arange(tc)[:, None] >= np.arange(tc)[None, :]), BF16)
    qdec, kdec, sdec, dmat = _retention_tables()
    depth = lb_logits.shape[0]

    def const(shape):
        return pl.BlockSpec(shape, lambda b, t: tuple(0 for _ in shape))

    row_spec = pl.BlockSpec((tc, WIDTH), lambda b, t: (b * nt + t, 0))
    tab_spec = pl.BlockSpec((tc, WIDTH), lambda b, t: (t, 0))

    def state_spec(rows):
        return pl.BlockSpec((None, rows, WIDTH), lambda b, t: (b, 0, 0))

    n = batch * seq
    outs = pl.pallas_call(
        functools.partial(_mixer_prompt_kernel, layer=layer, tc=tc, pos0=pos0),
        grid=(batch, nt),
        in_specs=[
            pl.BlockSpec((tc, MIX_COLS), lambda b, t: (b * nt + t, 0)),
            const((depth, WIDTH)),
            pl.BlockSpec((None, N_VEC_ROWS, WIDTH), lambda b, t: (layer, 0, 0)),
            pl.BlockSpec((None, WIDTH, WIDTH), lambda b, t: (layer, 0, 0)),
            pl.BlockSpec((None, WIDTH, WIDTH), lambda b, t: (layer, 0, 0)),
            pl.BlockSpec((None, WIDTH, WIDTH), lambda b, t: (layer, 0, 0)),
            tab_spec, tab_spec,
            const((WIDTH, WIDTH)), const((tc, tc)),
            const((CHUNK, WIDTH)), const((CHUNK, WIDTH)), const((1, WIDTH)),
            const((N_HEADS * CHUNK, CHUNK)),
        ],
        out_specs=[row_spec, row_spec, row_spec, row_spec,
                   state_spec(WIDTH), state_spec(1), state_spec(CONV_WIDTH - 1), state_spec(WIDTH),
                   state_spec(POOL_BUF)],
        out_shape=[jax.ShapeDtypeStruct((n, WIDTH), BF16)] * 4 + [
            jax.ShapeDtypeStruct((batch, WIDTH, WIDTH), F32),
            jax.ShapeDtypeStruct((batch, 1, WIDTH), F32),
            jax.ShapeDtypeStruct((batch, CONV_WIDTH - 1, WIDTH), F32),
            jax.ShapeDtypeStruct((batch, WIDTH, WIDTH), F32),
            jax.ShapeDtypeStruct((batch, POOL_BUF, WIDTH), F32),
        ],
        scratch_shapes=[
            pltpu.VMEM((WIDTH, WIDTH), F32), pltpu.VMEM((WIDTH, WIDTH), F32), pltpu.VMEM((8, WIDTH), F32),
            pltpu.VMEM((PAD + tc, WIDTH), F32), pltpu.VMEM((PAD + tc, WIDTH), F32),
            pltpu.VMEM((PAD + tc, WIDTH), F32), pltpu.VMEM((PAD + tc, WIDTH), F32),
            pltpu.VMEM((PAD + tc, WIDTH), F32),
            pltpu.VMEM((tc, WIDTH), F32), pltpu.VMEM((tc, WIDTH), F32),
        ],
        compiler_params=_cparams("parallel", "arbitrary"),
        name="mixer_prompt",
    )(u, lb_logits, vecs, wr_bd, wi_bd, wp_bd, cos_t, sin_t, ind, tri, qdec, kdec, sdec, dmat)
    return outs


def _diag_blocks(s):
    return jnp.stack([s[:, h * HEAD_DIM:(h + 1) * HEAD_DIM, h * HEAD_DIM:(h + 1) * HEAD_DIM]
                      for h in range(N_HEADS)], axis=1)


def _mixer_sample_kernel(u_ref, u3_ref, ut_ref, lbl_ref, lblt_ref, vec_ref, vec3_ref, wr_ref, wi_ref,
                         wp_ref, cost_ref, sint_ref, sh_ref, sl_ref, scv_ref, sr_ref, spl_ref,
                         oa_ref, ob_ref, oc_ref, od_ref, nh_ref, nl_ref, ncv_ref, nr_ref, npl_ref,
                         *, layer, bb, pos0):
    def slot(s):
        return u_ref[:, s * WIDTH:(s + 1) * WIDTH]

    def vec(r):
        return vec_ref[r:r + 1, :]

    lblt = lblt_ref[...]
    m = jnp.max(lblt, axis=1, keepdims=True)
    e = jnp.exp(lblt - m)
    soft = e / jnp.sum(e, axis=1, keepdims=True)
    acc = soft[:, 0:1]
    for l in range(1, layer + 1):
        acc = acc + soft[:, l:l + 1]
    lb_col = acc - soft[:, 0:1]
    aq_t = _silu(ut_ref[0 * WIDTH:1 * WIDTH, :])
    z_t = ut_ref[1 * WIDTH:2 * WIDTH, :]
    f_t = lb_col + (1.0 - lb_col) * _sigmoid(z_t)
    f_t = jnp.maximum(f_t, F_FLOOR)
    k_t = (1.0 - lb_col) * _sigmoid(-z_t)

    def swap_rows(x):
        half = HEAD_DIM // 2
        parts = []
        for h in range(N_HEADS):
            parts.append(x[h * HEAD_DIM + half:(h + 1) * HEAD_DIM])
            parts.append(x[h * HEAD_DIM:h * HEAD_DIM + half])
        return jnp.concatenate(parts, axis=0)

    cost, sint = cost_ref[...], sint_ref[...]
    cq_t, ck_t = ut_ref[2 * WIDTH:3 * WIDTH, :], ut_ref[3 * WIDTH:4 * WIDTH, :]
    rq_t = cq_t * cost + swap_rows(cq_t) * sint
    rk_t = (ck_t * cost + swap_rows(ck_t) * sint) * (HEAD_DIM ** -0.5)
    gam = [float(np.exp(g)) for g in _log_gamma()]

    for b in range(bb):
        for h in range(N_HEADS):
            rows = slice(h * HEAD_DIM, (h + 1) * HEAD_DIM)
            s_new = f_t[rows, b:b + 1] * sh_ref[b, h] + k_t[rows, b:b + 1] * u3_ref[b, 2 * N_HEADS + h:2 * N_HEADS + h + 1, :]
            nh_ref[b, h] = s_new
            oa_ref[b, h:h + 1, :] = jnp.sum(aq_t[rows, b:b + 1] * s_new, axis=0, keepdims=True)
            r_new = gam[h] * sr_ref[b, h] + rk_t[rows, b:b + 1] * u3_ref[b, 8 * N_HEADS + h:8 * N_HEADS + h + 1, :]
            nr_ref[b, h] = r_new
            oc_ref[b, h:h + 1, :] = jnp.sum(rq_t[rows, b:b + 1] * r_new, axis=0, keepdims=True)

    o_a = oa_ref[...]
    ms = jnp.mean(o_a * o_a, axis=-1, keepdims=True)
    ug3 = u3_ref[:, 3 * N_HEADS:4 * N_HEADS, :]
    oa_ref[...] = o_a * lax.rsqrt(ms + EPS) * vec3_ref[0] * _silu(ug3)
    o_c = oc_ref[...]
    mu = jnp.mean(o_c, axis=-1, keepdims=True)
    dev = o_c - mu
    var = jnp.mean(dev * dev, axis=-1, keepdims=True)
    cg3 = u3_ref[:, 9 * N_HEADS:10 * N_HEADS, :]
    oc_ref[...] = dev * lax.rsqrt(var + EPS) * vec3_ref[1] * _silu(cg3)

    ux = slot(4)
    xc = vec(V_CB) + vec(V_CW0 + CONV_WIDTH - 1) * ux
    for j in range(CONV_WIDTH - 1):
        xc = xc + vec(V_CW0 + j) * scv_ref[:, j * WIDTH:(j + 1) * WIDTH]
    xc_bf = xc.astype(BF16)
    rg = _sigmoid(_dot(xc_bf, wr_ref[...]) + vec(V_BR))
    ig = _sigmoid(_dot(xc_bf, wi_ref[...]) + vec(V_BI))
    log_a = -LRU_C * rg * _softplus(-vec(V_LAM))
    a = jnp.exp(log_a)
    b_in = jnp.sqrt(jnp.maximum(_neg_expm1(2.0 * log_a), 0.0)) * (ig * xc)
    hnew = a * sl_ref[...] + b_in
    nl_ref[...] = hnew
    ob_ref[...] = hnew * jax.nn.gelu(slot(5))
    ncv_ref[:, 0:(CONV_WIDTH - 2) * WIDTH] = scv_ref[:, WIDTH:(CONV_WIDTH - 1) * WIDTH]
    ncv_ref[:, (CONV_WIDTH - 2) * WIDTH:] = ux

    ud = slot(10)
    wsum = ud
    sums = []
    nxt = 1
    for win in POOL_WINDOWS:
        while nxt < win:
            wsum = wsum + spl_ref[:, (POOL_BUF - nxt) * WIDTH:(POOL_BUF - nxt + 1) * WIDTH]
            nxt += 1
        sums.append(wsum)
    lane = lax.broadcasted_iota(jnp.int32, (1, WIDTH), 1)
    grp = lane // (WIDTH // len(POOL_WINDOWS))
    sel = sums[-1]
    winl = jnp.full((1, WIDTH), float(POOL_WINDOWS[-1]), F32)
    for gi in range(len(POOL_WINDOWS) - 2, -1, -1):
        sel = jnp.where(grp == gi, sums[gi], sel)
        winl = jnp.where(grp == gi, float(POOL_WINDOWS[gi]), winl)
    pooled = sel / jnp.minimum(winl, float(pos0) + 1.0)
    od_ref[...] = _dot((pooled - ud).astype(BF16), wp_ref[...]) * vec(V_PSCALE)
    npl_ref[:, 0:(POOL_BUF - 1) * WIDTH] = spl_ref[:, WIDTH:POOL_BUF * WIDTH]
    npl_ref[:, (POOL_BUF - 1) * WIDTH:] = ud


def _mixer_sample(u, lb_logits, vecs, vecs3, wr_bd, wi_bd, wp_bd, cos_t, sin_t, s_hgrn, s_lru, s_conv,
                  s_ret, s_pool, layer, *, bb, pos0):
    nb = u.shape[0]
    ng = nb // bb
    depth = lb_logits.shape[0]
    u3 = u.reshape(nb, N_MIX_SLOTS * N_HEADS, HEAD_DIM)
    cols = jnp.concatenate([u[:, 0:2 * WIDTH], u[:, 6 * WIDTH:8 * WIDTH]], axis=1)
    ut = cols.reshape(ng, bb, 4 * WIDTH).transpose(0, 2, 1)
    cost = jnp.broadcast_to(cos_t.reshape(WIDTH, 1), (WIDTH, bb))
    sint = jnp.broadcast_to(sin_t.reshape(WIDTH, 1), (WIDTH, bb))
    conv2 = s_conv.reshape(nb, (CONV_WIDTH - 1) * WIDTH)
    pool2 = s_pool.reshape(nb, POOL_BUF * WIDTH)

    def const(shape):
        return pl.BlockSpec(shape, lambda i: tuple(0 for _ in shape))

    def rows(width):
        return pl.BlockSpec((bb, width), lambda i: (i, 0))

    st4 = pl.BlockSpec((bb, N_HEADS, HEAD_DIM, HEAD_DIM), lambda i: (i, 0, 0, 0))
    o3 = pl.BlockSpec((bb, N_HEADS, HEAD_DIM), lambda i: (i, 0, 0))
    outs = pl.pallas_call(
        functools.partial(_mixer_sample_kernel, layer=layer, bb=bb, pos0=pos0),
        grid=(ng,),
        in_specs=[
            rows(MIX_COLS),
            pl.BlockSpec((bb, N_MIX_SLOTS * N_HEADS, HEAD_DIM), lambda i: (i, 0, 0)),
            pl.BlockSpec((None, 4 * WIDTH, bb), lambda i: (i, 0, 0)),
            const((depth, WIDTH)), const((WIDTH, depth)),
            pl.BlockSpec((None, N_VEC_ROWS, WIDTH), lambda i: (layer, 0, 0)),
            pl.BlockSpec((None, 2, N_HEADS, HEAD_DIM), lambda i: (layer, 0, 0, 0)),
            pl.BlockSpec((None, WIDTH, WIDTH), lambda i: (layer, 0, 0)),
            pl.BlockSpec((None, WIDTH, WIDTH), lambda i: (layer, 0, 0)),
            pl.BlockSpec((None, WIDTH, WIDTH), lambda i: (layer, 0, 0)),
            const((WIDTH, bb)), const((WIDTH, bb)),
            st4, rows(WIDTH), rows((CONV_WIDTH - 1) * WIDTH), st4, rows(POOL_BUF * WIDTH),
        ],
        out_specs=[o3, rows(WIDTH), o3, rows(WIDTH),
                   st4, rows(WIDTH), rows((CONV_WIDTH - 1) * WIDTH), st4, rows(POOL_BUF * WIDTH)],
        out_shape=[
            jax.ShapeDtypeStruct((nb, N_HEADS, HEAD_DIM), F32), jax.ShapeDtypeStruct((nb, WIDTH), F32),
            jax.ShapeDtypeStruct((nb, N_HEADS, HEAD_DIM), F32), jax.ShapeDtypeStruct((nb, WIDTH), F32),
            jax.ShapeDtypeStruct(s_hgrn.shape, F32), jax.ShapeDtypeStruct((nb, WIDTH), F32),
            jax.ShapeDtypeStruct(conv2.shape, F32), jax.ShapeDtypeStruct(s_ret.shape, F32),
            jax.ShapeDtypeStruct(pool2.shape, F32),
        ],
        compiler_params=_cparams("parallel"),
        name="mixer_sample",
    )(u, u3, ut, lb_logits, lb_logits.T, vecs, vecs3, wr_bd, wi_bd, wp_bd, cost, sint,
      s_hgrn, s_lru, conv2, s_ret, pool2)
    o_a, o_b, o_c, o_d, n_h, n_l, n_cv, n_r, n_pl = outs
    return (o_a.reshape(nb, WIDTH), o_b, o_c.reshape(nb, WIDTH), o_d, n_h, n_l,
            n_cv.reshape(s_conv.shape), n_r, n_pl.reshape(s_pool.shape))


def _pick_tile(n, pref):
    t = min(n, pref)
    while n % t:
        t //= 2
    return t


def kernel(x_prompt, x_sample, state_hgrn, state_rglru, state_conv, state_retention, state_pool, lb_logits, ffn1_norm, ffn1_up, ffn1_down, mix_norm, w_in, hgrn_norm, conv_w, conv_b, w_rgate, b_rgate, w_igate, b_igate, lru_lambda, ret_norm, w_pool, pool_scale, w_branch, w_o, ffn2_norm, ffn2_up, ffn2_down, final_norm):
    batch, seq, d = x_prompt.shape
    nb, dec_seq, _ = x_sample.shape
    assert dec_seq == 1
    depth = w_in.shape[0]
    past_len = PAST_LEN

    up1, dn1 = ffn1_up.astype(BF16), ffn1_down.astype(BF16)
    up2, dn2 = ffn2_up.astype(BF16), ffn2_down.astype(BF16)
    w_mix = w_in[:, :, :MIX_COLS].astype(BF16)
    w_gate = w_in[:, :, MIX_COLS:].astype(BF16)
    w_br, w_out = w_branch.astype(BF16), w_o.astype(BF16)
    wr_bd = jax.vmap(_block_diag)(w_rgate).astype(BF16)
    wi_bd = jax.vmap(_block_diag)(w_igate).astype(BF16)
    wp_bd = jax.vmap(_block_diag)(w_pool).astype(BF16)
    vecs = jnp.concatenate(
        [hgrn_norm[:, None], conv_w, conv_b[:, None], b_rgate[:, None], b_igate[:, None],
         lru_lambda[:, None], ret_norm[:, None], pool_scale[:, None],
         jnp.zeros((depth, N_VEC_ROWS - 11, WIDTH), F32)], axis=1)
    vecs3 = jnp.stack([hgrn_norm, ret_norm], axis=1).reshape(depth, 2, N_HEADS, HEAD_DIM)
    n1 = ffn1_norm[:, None, :]
    n2 = ffn2_norm[:, None, :]
    nm = mix_norm[:, None, :]
    fg = final_norm[None, :]

    cos_p, sin_p = _rope_tables(jnp.arange(seq, dtype=F32))
    cos_s, sin_s = _rope_tables(past_len + jnp.arange(dec_seq, dtype=F32))

    xp = x_prompt.reshape(batch * seq, d)
    xs = x_sample.reshape(nb * dec_seq, d)
    tm_p = _pick_tile(batch * seq, FFN_ROWS)
    tm_s = _pick_tile(nb, 128)
    tc = _pick_tile(seq, 256)

    st_p = [[] for _ in range(5)]
    st_s = [[] for _ in range(5)]
    for l in range(depth):
        last = l == depth - 1
        xp = _ffn(xp, n1, up1, dn1, l, fg, final_norm=False, tm=tm_p)
        u_p = _inproj(xp, nm, w_mix, l, tm=_pick_tile(batch * seq, 512))
        oa, ob, oc, od, sh, sl, scv, sr, spl = _mixer_prompt(
            u_p, lb_logits, vecs, wr_bd, wi_bd, wp_bd, cos_p, sin_p, l, batch=batch, seq=seq, tc=tc, pos0=0)
        xp = _merge(xp, nm, (oa, ob, oc, od), w_gate, w_br, w_out, l, tm=_pick_tile(batch * seq, 512))
        xp = _ffn(xp, n2, up2, dn2, l, fg, final_norm=last, tm=tm_p)
        for lst, s in zip(st_p, (jnp.swapaxes(_diag_blocks(sh), 2, 3), sl[:, 0], scv, _diag_blocks(sr), spl)):
            lst.append(s)
        xs = _ffn(xs, n1, up1, dn1, l, fg, final_norm=False, tm=tm_s)
        u_s = _inproj(xs, nm, w_mix, l, tm=tm_s)
        oa, ob, oc, od, nh, nl, ncv, nr, npl = _mixer_sample(
            u_s, lb_logits, vecs, vecs3, wr_bd, wi_bd, wp_bd, cos_s, sin_s, state_hgrn[l], state_rglru[l],
            state_conv[l], state_retention[l], state_pool[l], l, bb=8, pos0=past_len)
        xs = _merge(xs, nm, (oa, ob, oc, od), w_gate, w_br, w_out, l, tm=tm_s)
        xs = _ffn(xs, n2, up2, dn2, l, fg, final_norm=last, tm=tm_s)
        for lst, s in zip(st_s, (nh, nl, ncv, nr, npl)):
            lst.append(s)

    y_p = xp.reshape(batch, seq, d)
    y_s = xs.reshape(nb, dec_seq, d)
    return (y_p, y_s) + tuple(jnp.stack(s) for s in st_p) + tuple(jnp.stack(s) for s in st_s)
```

```python
import functools

import numpy as np
import jax
import jax.numpy as jnp
from jax import lax
from jax.experimental import pallas as pl
from jax.experimental.pallas import tpu as pltpu

F32 = jnp.float32
BF16 = jnp.bfloat16

N_HEADS = 4
HEAD_DIM = 64
WIDTH = N_HEADS * HEAD_DIM
N_MIX_SLOTS = 11
MIX_COLS = N_MIX_SLOTS * WIDTH
CHUNK = 64
N_LEVELS = 6
PAD = 16
CONV_WIDTH = 4
LRU_C = 8.0
POOL_WINDOWS = (2, 4, 8, 16)
POOL_BUF = 15
ROPE_BASE = 10000.0
EPS = 1e-6
F_FLOOR = 1e-30
PAST_LEN = 16384
VMEM_LIMIT = 56 * 1024 * 1024
FFN_ROWS = 512
MIXER_ROWS = 512


def _cparams(*sem):
    return pltpu.CompilerParams(dimension_semantics=sem, vmem_limit_bytes=VMEM_LIMIT)


def _dot(a, b):
    return jnp.dot(a, b, preferred_element_type=F32)


def _dot_nt(a, b):
    return lax.dot_general(a, b, (((1,), (1,)), ((), ())), preferred_element_type=F32)


def _dot_tn(a, b):
    return lax.dot_general(a, b, (((0,), (0,)), ((), ())), preferred_element_type=F32)


def _rms(x, g):
    return x * lax.rsqrt(jnp.mean(x * x, axis=-1, keepdims=True) + EPS) * g


def _sigmoid(x):
    return jax.nn.sigmoid(x)


def _silu(x):
    return x * jax.nn.sigmoid(x)


def _softplus(x):
    return jnp.maximum(x, 0.0) + jnp.log1p(jnp.exp(-jnp.abs(x)))


def _neg_expm1(y):
    t = jnp.tanh(0.5 * y)
    return -2.0 * t / (1.0 - t)


def _split2(x):
    hi = x.astype(BF16)
    lo = (x - hi.astype(F32)).astype(BF16)
    return hi, lo


def _split3(x):
    hi = x.astype(BF16)
    r = x - hi.astype(F32)
    mid = r.astype(BF16)
    lo = (r - mid.astype(F32)).astype(BF16)
    return hi, mid, lo


def _lower_bound(lbl, layer):
    m = jnp.max(lbl, axis=0, keepdims=True)
    e = jnp.exp(lbl - m)
    soft = e / jnp.sum(e, axis=0, keepdims=True)
    acc = soft[0:1]
    for l in range(1, layer + 1):
        acc = acc + soft[l:l + 1]
    return acc - soft[0:1]


def _ffn_kernel(x_ref, g_ref, wu_ref, wd_ref, fg_ref, o_ref, *, final_norm):
    x = x_ref[...]
    dff = wd_ref.shape[0]
    h = _rms(x, g_ref[...]).astype(BF16)
    a = _dot(h, wu_ref[:, :dff])
    b = _dot(h, wu_ref[:, dff:])
    act = (_silu(a) * b).astype(BF16)
    y = x + 0.5 * _dot(act, wd_ref[...])
    if final_norm:
        y = _rms(y, fg_ref[...])
    o_ref[...] = y


def _resident(shape, index_map):
    return pl.BlockSpec(shape, index_map, pipeline_mode=pl.Buffered(1))


def _ffn(x, g, w_up, w_down, layer, final_g, *, final_norm, tm):
    n, d = x.shape
    dff = w_down.shape[1]
    return pl.pallas_call(
        functools.partial(_ffn_kernel, final_norm=final_norm),
        grid=(n // tm,),
        in_specs=[
            pl.BlockSpec((tm, d), lambda i: (i, 0)),
            _resident((None, 1, d), lambda i: (layer, 0, 0)),
            _resident((None, d, 2 * dff), lambda i: (layer, 0, 0)),
            _resident((None, dff, d), lambda i: (layer, 0, 0)),
            _resident((1, d), lambda i: (0, 0)),
        ],
        out_specs=pl.BlockSpec((tm, d), lambda i: (i, 0)),
        out_shape=jax.ShapeDtypeStruct((n, d), F32),
        compiler_params=_cparams("parallel"),
        name="ffn",
    )(x, g, w_up, w_down, final_g)


def _inproj_kernel(x_ref, g_ref, w_ref, o_ref):
    h = _rms(x_ref[...], g_ref[...]).astype(BF16)
    o_ref[...] = _dot(h, w_ref[...])


def _inproj(x, g, w_mix, layer, *, tm):
    n, d = x.shape
    cols = w_mix.shape[2]
    return pl.pallas_call(
        _inproj_kernel,
        grid=(n // tm,),
        in_specs=[
            pl.BlockSpec((tm, d), lambda i: (i, 0)),
            _resident((None, 1, d), lambda i: (layer, 0, 0)),
            _resident((None, d, cols), lambda i: (layer, 0, 0)),
        ],
        out_specs=pl.BlockSpec((tm, cols), lambda i: (i, 0)),
        out_shape=jax.ShapeDtypeStruct((n, cols), F32),
        compiler_params=_cparams("parallel"),
        name="inproj",
    )(x, g, w_mix)


def _merge_kernel(x_ref, g_ref, oa_ref, ob_ref, oc_ref, od_ref, wg_ref, wb_ref, wo_ref, o_ref):
    x = x_ref[...]
    d = x.shape[1]
    h = _rms(x, g_ref[...]).astype(BF16)
    merged = None
    for b, br_ref in enumerate((oa_ref, ob_ref, oc_ref, od_ref)):
        gate = _sigmoid(_dot(h, wg_ref[:, b * d:(b + 1) * d]))
        y = _dot(br_ref[...].astype(BF16), wb_ref[b])
        merged = gate * y if merged is None else merged + gate * y
    o_ref[...] = x + _dot(merged.astype(BF16), wo_ref[...])


def _merge(x, g, branches, w_gate, w_branch, w_o, layer, *, tm):
    n, d = x.shape
    w = branches[0].shape[1]
    nb = len(branches)
    return pl.pallas_call(
        _merge_kernel,
        grid=(n // tm,),
        in_specs=[
            pl.BlockSpec((tm, d), lambda i: (i, 0)),
            _resident((None, 1, d), lambda i: (layer, 0, 0)),
        ] + [pl.BlockSpec((tm, w), lambda i: (i, 0)) for _ in range(nb)] + [
            _resident((None, d, nb * d), lambda i: (layer, 0, 0)),
            _resident((None, nb, w, d), lambda i: (layer, 0, 0, 0)),
            _resident((None, d, d), lambda i: (layer, 0, 0)),
        ],
        out_specs=pl.BlockSpec((tm, d), lambda i: (i, 0)),
        out_shape=jax.ShapeDtypeStruct((n, d), F32),
        compiler_params=_cparams("parallel"),
        name="merge",
    )(x, g, *branches, w_gate, w_branch, w_o)


def _head_block_ones():
    h = np.arange(WIDTH) // HEAD_DIM
    return (h[:, None] == h[None, :]).astype(np.float32)


def _log_gamma():
    return np.log1p(-(2.0 ** (-5.0 - np.arange(N_HEADS, dtype=np.float64))))


def _retention_tables():
    lg = np.repeat(_log_gamma(), HEAD_DIM)[None, :]
    t = np.arange(CHUNK, dtype=np.float64)[:, None]
    q_dec = np.exp((t + 1.0) * lg)
    k_dec = np.exp((CHUNK - 1.0 - t) * lg)
    s_dec = np.exp(CHUNK * lg)
    dt = t.T - t
    dmat = np.concatenate(
        [np.where(dt >= 0, np.exp(dt * g), 0.0) for g in _log_gamma()], axis=1)
    return (jnp.asarray(q_dec, F32), jnp.asarray(k_dec, F32), jnp.asarray(s_dec, F32),
            jnp.asarray(dmat, F32))


def _level_tables():
    t = np.arange(CHUNK)[:, None]
    r = np.arange(CHUNK)[None, :]
    blocks = [r <= t, r > t]
    pairs = []
    for i in range(N_LEVELS):
        m = 1 << i
        mid = t - (t % (2 * m)) + m
        up = (t % (2 * m)) >= m
        blocks.append(np.where(up, (r >= mid) & (r <= t), (r > t) & (r <= mid - 1)))
        same = (t // (2 * m)) == (r // (2 * m))
        pr = same & ((t % (2 * m)) < m) & ((r % (2 * m)) >= m)
        pairs.append(np.tile(pr, (1, N_HEADS)))
    dsum = np.tile(np.concatenate(blocks, axis=0), (1, 3))
    return jnp.asarray(dsum, BF16), jnp.asarray(np.stack(pairs), F32)


def _rope_tables(pos):
    half = HEAD_DIM // 2
    freq = ROPE_BASE ** (-jnp.arange(half, dtype=F32) / half)
    ang = pos[:, None] * freq[None, :]
    cos, sin = jnp.cos(ang), jnp.sin(ang)
    cos_h = jnp.concatenate([cos, cos], axis=-1)
    sin_h = jnp.concatenate([-sin, sin], axis=-1)
    return jnp.tile(cos_h, (1, N_HEADS)), jnp.tile(sin_h, (1, N_HEADS))


def _block_diag(w):
    h, dh, _ = w.shape
    eye = jnp.eye(h, dtype=w.dtype)
    return (eye[:, None, :, None] * w[:, :, None, :]).reshape(h * dh, h * dh)


V_HNORM, V_CW0, V_CB, V_BR, V_BI, V_LAM, V_RNORM, V_PSCALE = 0, 1, 5, 6, 7, 8, 9, 10
N_VEC_ROWS = 16


def _swap_halves(x):
    half = HEAD_DIM // 2
    lane = lax.broadcasted_iota(jnp.int32, (1, 128), 1)
    first = (lane % HEAD_DIM) < half
    parts = []
    for c in range(x.shape[1] // 128):
        xc = x[:, c * 128:(c + 1) * 128]
        parts.append(jnp.where(first, pltpu.roll(xc, 128 - half, 1), pltpu.roll(xc, half, 1)))
    return jnp.concatenate(parts, axis=1)


def _mixer_prompt_kernel(u_ref, lbl_ref, vec_ref, wr_ref, wi_ref, wp_ref, cos_ref, sin_ref,
                         ind_ref, dsum_ref, pair_ref, qdec_ref, kdec_ref, sdec_ref, dmat_ref,
                         oa_ref, ob_ref, oc_ref, od_ref, sh_ref, sl_ref, scv_ref, sr_ref, spl_ref,
                         hst_scr, rst_scr, hl_scr, xcv_scr, xpl_scr, oh_scr, orr_scr, perm_scr,
                         *, layer, tc, pos0):
    t_idx = pl.program_id(1)
    nchunk = tc // CHUNK

    @pl.when(t_idx == 0)
    def _():
        hst_scr[...] = jnp.zeros_like(hst_scr)
        rst_scr[...] = jnp.zeros_like(rst_scr)
        hl_scr[...] = jnp.zeros_like(hl_scr)
        xcv_scr[...] = jnp.zeros_like(xcv_scr)
        xpl_scr[0:PAD, :] = jnp.zeros((PAD, WIDTH), F32)

    def slot(s):
        return u_ref[:, s * WIDTH:(s + 1) * WIDTH]

    def vec(r):
        return vec_ref[r:r + 1, :]

    ind = ind_ref[...]
    maskbd = ind.astype(F32)
    lane = lax.broadcasted_iota(jnp.int32, (1, WIDTH), 1)
    hmask = [(lane // HEAD_DIM == h).astype(F32) for h in range(N_HEADS)]
    hm_bf = [m.astype(BF16) for m in hmask]
    row = lax.broadcasted_iota(jnp.int32, (tc, WIDTH), 0)

    def head_sum(x):
        hi, lo = _split2(x)
        return _dot(hi, ind) + _dot(lo, ind)

    lb = _lower_bound(lbl_ref[...], layer)
    uq, z, v = slot(0), slot(1), slot(2)
    q = _silu(uq)
    sg = _sigmoid(z)
    f = lb + (1.0 - lb) * sg
    logf = jnp.log(jnp.maximum(f, F_FLOOR))
    k = (1.0 - lb) * (1.0 - sg)
    l_hi, l_mid, l_lo = _split3(logf)
    v_bf = v.astype(BF16)
    dsum = dsum_ref[...]
    rowc = lax.broadcasted_iota(jnp.int32, (CHUNK, WIDTH), 0)
    upper = [((rowc >> i) & 1) == 1 for i in range(N_LEVELS)]
    chunks = [slice(n * CHUNK, (n + 1) * CHUNK) for n in range(nchunk)]
    dec = [jnp.exp(jnp.minimum(_dot(dsum, jnp.concatenate([l_hi[rs], l_mid[rs], l_lo[rs]], axis=0)), 0.0))
           for rs in chunks]
    sct = [None] * nchunk
    for i in range(N_LEVELS):
        for n, rs in enumerate(chunks):
            fac = dec[n][(2 + i) * CHUNK:(3 + i) * CHUNK]
            zz = (jnp.where(upper[i], q[rs], k[rs]) * fac).astype(BF16)
            qbd = jnp.concatenate([zz * hm_bf[h] for h in range(N_HEADS)], axis=0)
            term = _dot_nt(zz, qbd) * pair_ref[i]
            sct[n] = term if sct[n] is None else sct[n] + term
    o_diag = _dot((q * k).astype(BF16), ind) * v
    upd = [_dot_tn(v_bf[rs], (k[rs] * dec[n][CHUNK:2 * CHUNK]).astype(BF16)) for n, rs in enumerate(chunks)]
    of = [_dot_tn(sct[n].astype(BF16), v_bf[rs]) for n, rs in enumerate(chunks)]
    for n, rs in enumerate(chunks):
        st = hst_scr[...]
        o_c = _dot_nt((q[rs] * dec[n][0:CHUNK]).astype(BF16), st.astype(BF16)) + o_diag[rs]
        hst_scr[...] = st * dec[n][CHUNK - 1:CHUNK] + maskbd * upd[n]
        for h in range(N_HEADS):
            o_c = o_c + of[n][h * CHUNK:(h + 1) * CHUNK] * hmask[h]
        oh_scr[rs, :] = o_c
    o_h = oh_scr[...]
    ms = head_sum(o_h * o_h) * (1.0 / HEAD_DIM)
    oa_ref[...] = (o_h * lax.rsqrt(ms + EPS) * vec(V_HNORM) * _silu(slot(3))).astype(oa_ref.dtype)

    seg_len = tc // 8
    def strided_rows(ref, start, stride):
        return jnp.concatenate([ref[c, pl.ds(start, 8, stride=stride), :] for c in range(2)], axis=1)

    for c in range(2):
        perm_scr[c] = u_ref[:, 4 * WIDTH + c * 128:4 * WIDTH + (c + 1) * 128]
    slabs = [strided_rows(perm_scr, i, seg_len) for i in range(seg_len)]
    sub0 = lax.broadcasted_iota(jnp.int32, (8, WIDTH), 0) == 0
    front = [jnp.where(sub0, xcv_scr[8 + k:9 + k, :], pltpu.roll(slabs[seg_len + k], 1, 0))
             for k in range(-(CONV_WIDTH - 1), 0)]
    ext = front + slabs
    cw = [vec(V_CW0 + j) for j in range(CONV_WIDTH)]
    xc = jnp.concatenate(
        [vec(V_CB) + sum(cw[j] * ext[i + j] for j in range(CONV_WIDTH)) for i in range(seg_len)], axis=0)
    xc_bf = xc.astype(BF16)
    rg = _sigmoid(_dot(xc_bf, wr_ref[...]) + vec(V_BR))
    ig = _sigmoid(_dot(xc_bf, wi_ref[...]) + vec(V_BI))
    log_a = -LRU_C * rg * _softplus(-vec(V_LAM))
    a_all = jnp.exp(log_a)
    b_all = jnp.sqrt(jnp.maximum(_neg_expm1(2.0 * log_a), 0.0)) * (ig * xc)
    h_loc, a_loc = [b_all[0:8]], [a_all[0:8]]
    for i in range(1, seg_len):
        a_i = a_all[8 * i:8 * i + 8]
        h_loc.append(a_i * h_loc[-1] + b_all[8 * i:8 * i + 8])
        a_loc.append(a_i * a_loc[-1])
    carry = [hl_scr[0:1, :]]
    for s in range(8):
        carry.append(a_loc[-1][s:s + 1] * carry[-1] + h_loc[-1][s:s + 1])
    hl_scr[0:1, :] = carry[8]
    carry8 = jnp.concatenate(carry[0:8], axis=0)
    hperm = jnp.concatenate([h_loc[i] + a_loc[i] * carry8 for i in range(seg_len)], axis=0)
    for c in range(2):
        perm_scr[c] = hperm[:, c * 128:(c + 1) * 128]
    hseq = jnp.concatenate(
        [strided_rows(perm_scr, ((8 * r) % seg_len) * 8 + (8 * r) // seg_len, 8) for r in range(seg_len)],
        axis=0)
    ob_ref[...] = (hseq * jax.nn.gelu(slot(5))).astype(ob_ref.dtype)
    xcv_scr[...] = u_ref[tc - 8:tc, 4 * WIDTH:5 * WIDTH]

    cosv, sinv = cos_ref[...], sin_ref[...]
    cq, ck = slot(6), slot(7)
    qr = cq * cosv + _swap_halves(cq) * sinv
    kr = (ck * cosv + _swap_halves(ck) * sinv) * (HEAD_DIM ** -0.5)
    rv_bf = slot(8).astype(BF16)
    qdec, kdec, sdec, dmat = qdec_ref[...], kdec_ref[...], sdec_ref[...], dmat_ref[...]
    qr_bf, kr_bf = qr.astype(BF16), kr.astype(BF16)
    rsc = []
    for rs in chunks:
        qbd = jnp.concatenate([qr_bf[rs] * hm_bf[h] for h in range(N_HEADS)], axis=0)
        rsc.append((_dot_nt(kr_bf[rs], qbd) * dmat).astype(BF16))
    rof = [_dot_tn(rsc[n], rv_bf[rs]) for n, rs in enumerate(chunks)]
    rupd = [_dot_tn((kr[rs] * kdec).astype(BF16), rv_bf[rs]) for rs in chunks]
    for n, rs in enumerate(chunks):
        s_ret = rst_scr[...]
        o_c = _dot((qr[rs] * qdec).astype(BF16), s_ret.astype(BF16))
        rst_scr[...] = s_ret * sdec + maskbd * rupd[n]
        for h in range(N_HEADS):
            o_c = o_c + rof[n][h * CHUNK:(h + 1) * CHUNK] * hmask[h]
        orr_scr[rs, :] = o_c
    o_r = orr_scr[...]
    mu = head_sum(o_r) * (1.0 / HEAD_DIM)
    dev = o_r - mu
    var = head_sum(dev * dev) * (1.0 / HEAD_DIM)
    oc_ref[...] = (dev * lax.rsqrt(var + EPS) * vec(V_RNORM) * _silu(slot(9))).astype(oc_ref.dtype)

    ud = slot(10)
    xpl_scr[PAD:PAD + tc, :] = ud
    assert POOL_WINDOWS == (2, 4, 8, 16) and WIDTH // len(POOL_WINDOWS) == 64 and PAD >= POOL_BUF
    lane_t = lax.broadcasted_iota(jnp.int32, (1, 128), 1)
    halves = []
    for c in range(2):
        acc = xpl_scr[:, c * 128:(c + 1) * 128]
        built = {}
        for sh in (1, 2, 4, 8):
            acc = acc + pltpu.roll(acc, sh, 0)
            built[2 * sh] = acc
        lo, hi = POOL_WINDOWS[2 * c], POOL_WINDOWS[2 * c + 1]
        halves.append(jnp.where(lane_t < 64, built[lo], built[hi])[PAD:PAD + tc])
    sel = jnp.concatenate(halves, axis=1)
    grp = lane // (WIDTH // len(POOL_WINDOWS))
    winl = jnp.full((1, WIDTH), float(POOL_WINDOWS[-1]), F32)
    for gi in range(len(POOL_WINDOWS) - 2, -1, -1):
        winl = jnp.where(grp == gi, float(POOL_WINDOWS[gi]), winl)
    pos = (row + t_idx * tc).astype(F32) + float(pos0)
    pooled = sel / jnp.minimum(winl, pos + 1.0)
    od = _dot((pooled - ud).astype(BF16), wp_ref[...]) * vec(V_PSCALE)
    od_ref[...] = od.astype(od_ref.dtype)
    xpl_scr[0:PAD, :] = xpl_scr[tc:tc + PAD, :]

    @pl.when(t_idx == pl.num_programs(1) - 1)
    def _():
        sh_ref[...] = hst_scr[...]
        sr_ref[...] = rst_scr[...]
        sl_ref[...] = hl_scr[0:1, :]
        scv_ref[...] = xcv_scr[8 - (CONV_WIDTH - 1):8, :]
        spl_ref[...] = xpl_scr[PAD + tc - POOL_BUF:PAD + tc, :]


def _mixer_prompt(u, lb_logits, vecs, wr_bd, wi_bd, wp_bd, cos_t, sin_t, layer, *, batch, seq, tc, pos0):
    nt = seq // tc
    ind = jnp.asarray(_head_block_ones(), BF16)
    dsum, pair = _level_tables()
    qdec, kdec, sdec, dmat = _retention_tables()
    depth = lb_logits.shape[0]

    def const(shape):
        return pl.BlockSpec(shape, lambda b, t: tuple(0 for _ in shape))

    row_spec = pl.BlockSpec((tc, WIDTH), lambda b, t: (b * nt + t, 0))
    tab_spec = pl.BlockSpec((tc, WIDTH), lambda b, t: (t, 0))

    def state_spec(rows):
        return pl.BlockSpec((None, rows, WIDTH), lambda b, t: (b, 0, 0))

    n = batch * seq
    outs = pl.pallas_call(
        functools.partial(_mixer_prompt_kernel, layer=layer, tc=tc, pos0=pos0),
        grid=(batch, nt),
        in_specs=[
            pl.BlockSpec((tc, MIX_COLS), lambda b, t: (b * nt + t, 0)),
            const((depth, WIDTH)),
            pl.BlockSpec((None, N_VEC_ROWS, WIDTH), lambda b, t: (layer, 0, 0)),
            pl.BlockSpec((None, WIDTH, WIDTH), lambda b, t: (layer, 0, 0)),
            pl.BlockSpec((None, WIDTH, WIDTH), lambda b, t: (layer, 0, 0)),
            pl.BlockSpec((None, WIDTH, WIDTH), lambda b, t: (layer, 0, 0)),
            tab_spec, tab_spec,
            const((WIDTH, WIDTH)), const(dsum.shape), const(pair.shape),
            const((CHUNK, WIDTH)), const((CHUNK, WIDTH)), const((1, WIDTH)),
            const((CHUNK, N_HEADS * CHUNK)),
        ],
        out_specs=[row_spec, row_spec, row_spec, row_spec,
                   state_spec(WIDTH), state_spec(1), state_spec(CONV_WIDTH - 1), state_spec(WIDTH),
                   state_spec(POOL_BUF)],
        out_shape=[jax.ShapeDtypeStruct((n, WIDTH), BF16)] * 4 + [
            jax.ShapeDtypeStruct((batch, WIDTH, WIDTH), F32),
            jax.ShapeDtypeStruct((batch, 1, WIDTH), F32),
            jax.ShapeDtypeStruct((batch, CONV_WIDTH - 1, WIDTH), F32),
            jax.ShapeDtypeStruct((batch, WIDTH, WIDTH), F32),
            jax.ShapeDtypeStruct((batch, POOL_BUF, WIDTH), F32),
        ],
        scratch_shapes=[
            pltpu.VMEM((WIDTH, WIDTH), F32), pltpu.VMEM((WIDTH, WIDTH), F32), pltpu.VMEM((8, WIDTH), F32),
            pltpu.VMEM((8, WIDTH), F32), pltpu.VMEM((PAD + tc, WIDTH), F32),
            pltpu.VMEM((tc, WIDTH), F32), pltpu.VMEM((tc, WIDTH), F32), pltpu.VMEM((2, tc, 128), F32),
        ],
        compiler_params=_cparams("parallel", "arbitrary"),
        name="mixer_prompt",
    )(u, lb_logits, vecs, wr_bd, wi_bd, wp_bd, cos_t, sin_t, ind, dsum, pair, qdec, kdec, sdec, dmat)
    return outs


def _diag_blocks(s):
    return jnp.stack([s[:, h * HEAD_DIM:(h + 1) * HEAD_DIM, h * HEAD_DIM:(h + 1) * HEAD_DIM]
                      for h in range(N_HEADS)], axis=1)


def _mixer_sample_kernel(u_ref, u3_ref, ut_ref, lbl_ref, lblt_ref, vec_ref, vec3_ref, wr_ref, wi_ref,
                         wp_ref, cost_ref, sint_ref, sh_ref, sl_ref, scv_ref, sr_ref, spl_ref,
                         oa_ref, ob_ref, oc_ref, od_ref, nh_ref, nl_ref, ncv_ref, nr_ref, npl_ref,
                         *, layer, bb, pos0):
    def slot(s):
        return u_ref[:, s * WIDTH:(s + 1) * WIDTH]

    def vec(r):
        return vec_ref[r:r + 1, :]

    lblt = lblt_ref[...]
    m = jnp.max(lblt, axis=1, keepdims=True)
    e = jnp.exp(lblt - m)
    soft = e / jnp.sum(e, axis=1, keepdims=True)
    acc = soft[:, 0:1]
    for l in range(1, layer + 1):
        acc = acc + soft[:, l:l + 1]
    lb_col = acc - soft[:, 0:1]
    aq_t = _silu(ut_ref[0 * WIDTH:1 * WIDTH, :])
    z_t = ut_ref[1 * WIDTH:2 * WIDTH, :]
    f_t = lb_col + (1.0 - lb_col) * _sigmoid(z_t)
    f_t = jnp.maximum(f_t, F_FLOOR)
    k_t = (1.0 - lb_col) * _sigmoid(-z_t)

    def swap_rows(x):
        half = HEAD_DIM // 2
        parts = []
        for h in range(N_HEADS):
            parts.append(x[h * HEAD_DIM + half:(h + 1) * HEAD_DIM])
            parts.append(x[h * HEAD_DIM:h * HEAD_DIM + half])
        return jnp.concatenate(parts, axis=0)

    cost, sint = cost_ref[...], sint_ref[...]
    cq_t, ck_t = ut_ref[2 * WIDTH:3 * WIDTH, :], ut_ref[3 * WIDTH:4 * WIDTH, :]
    rq_t = cq_t * cost + swap_rows(cq_t) * sint
    rk_t = (ck_t * cost + swap_rows(ck_t) * sint) * (HEAD_DIM ** -0.5)
    gam = [float(np.exp(g)) for g in _log_gamma()]

    for b in range(bb):
        for h in range(N_HEADS):
            rows = slice(h * HEAD_DIM, (h + 1) * HEAD_DIM)
            s_new = f_t[rows, b:b + 1] * sh_ref[b, h] + k_t[rows, b:b + 1] * u3_ref[b, 2 * N_HEADS + h:2 * N_HEADS + h + 1, :]
            nh_ref[b, h] = s_new
            oa_ref[b, h:h + 1, :] = jnp.sum(aq_t[rows, b:b + 1] * s_new, axis=0, keepdims=True)
            r_new = gam[h] * sr_ref[b, h] + rk_t[rows, b:b + 1] * u3_ref[b, 8 * N_HEADS + h:8 * N_HEADS + h + 1, :]
            nr_ref[b, h] = r_new
            oc_ref[b, h:h + 1, :] = jnp.sum(rq_t[rows, b:b + 1] * r_new, axis=0, keepdims=True)

    o_a = oa_ref[...]
    ms = jnp.mean(o_a * o_a, axis=-1, keepdims=True)
    ug3 = u3_ref[:, 3 * N_HEADS:4 * N_HEADS, :]
    oa_ref[...] = o_a * lax.rsqrt(ms + EPS) * vec3_ref[0] * _silu(ug3)
    o_c = oc_ref[...]
    mu = jnp.mean(o_c, axis=-1, keepdims=True)
    dev = o_c - mu
    var = jnp.mean(dev * dev, axis=-1, keepdims=True)
    cg3 = u3_ref[:, 9 * N_HEADS:10 * N_HEADS, :]
    oc_ref[...] = dev * lax.rsqrt(var + EPS) * vec3_ref[1] * _silu(cg3)

    ux = slot(4)
    xc = vec(V_CB) + vec(V_CW0 + CONV_WIDTH - 1) * ux
    for j in range(CONV_WIDTH - 1):
        xc = xc + vec(V_CW0 + j) * scv_ref[:, j * WIDTH:(j + 1) * WIDTH]
    xc_bf = xc.astype(BF16)
    rg = _sigmoid(_dot(xc_bf, wr_ref[...]) + vec(V_BR))
    ig = _sigmoid(_dot(xc_bf, wi_ref[...]) + vec(V_BI))
    log_a = -LRU_C * rg * _softplus(-vec(V_LAM))
    a = jnp.exp(log_a)
    b_in = jnp.sqrt(jnp.maximum(_neg_expm1(2.0 * log_a), 0.0)) * (ig * xc)
    hnew = a * sl_ref[...] + b_in
    nl_ref[...] = hnew
    ob_ref[...] = hnew * jax.nn.gelu(slot(5))
    ncv_ref[:, 0:(CONV_WIDTH - 2) * WIDTH] = scv_ref[:, WIDTH:(CONV_WIDTH - 1) * WIDTH]
    ncv_ref[:, (CONV_WIDTH - 2) * WIDTH:] = ux

    ud = slot(10)
    wsum = ud
    sums = []
    nxt = 1
    for win in POOL_WINDOWS:
        while nxt < win:
            wsum = wsum + spl_ref[:, (POOL_BUF - nxt) * WIDTH:(POOL_BUF - nxt + 1) * WIDTH]
            nxt += 1
        sums.append(wsum)
    lane = lax.broadcasted_iota(jnp.int32, (1, WIDTH), 1)
    grp = lane // (WIDTH // len(POOL_WINDOWS))
    sel = sums[-1]
    winl = jnp.full((1, WIDTH), float(POOL_WINDOWS[-1]), F32)
    for gi in range(len(POOL_WINDOWS) - 2, -1, -1):
        sel = jnp.where(grp == gi, sums[gi], sel)
        winl = jnp.where(grp == gi, float(POOL_WINDOWS[gi]), winl)
    pooled = sel / jnp.minimum(winl, float(pos0) + 1.0)
    od_ref[...] = _dot((pooled - ud).astype(BF16), wp_ref[...]) * vec(V_PSCALE)
    npl_ref[:, 0:(POOL_BUF - 1) * WIDTH] = spl_ref[:, WIDTH:POOL_BUF * WIDTH]
    npl_ref[:, (POOL_BUF - 1) * WIDTH:] = ud


def _mixer_sample(u, lb_logits, vecs, vecs3, wr_bd, wi_bd, wp_bd, cos_t, sin_t, s_hgrn, s_lru, s_conv,
                  s_ret, s_pool, layer, *, bb, pos0):
    nb = u.shape[0]
    ng = nb // bb
    depth = lb_logits.shape[0]
    u3 = u.reshape(nb, N_MIX_SLOTS * N_HEADS, HEAD_DIM)
    cols = jnp.concatenate([u[:, 0:2 * WIDTH], u[:, 6 * WIDTH:8 * WIDTH]], axis=1)
    ut = cols.reshape(ng, bb, 4 * WIDTH).transpose(0, 2, 1)
    cost = jnp.broadcast_to(cos_t.reshape(WIDTH, 1), (WIDTH, bb))
    sint = jnp.broadcast_to(sin_t.reshape(WIDTH, 1), (WIDTH, bb))
    conv2 = s_conv.reshape(nb, (CONV_WIDTH - 1) * WIDTH)
    pool2 = s_pool.reshape(nb, POOL_BUF * WIDTH)

    def const(shape):
        return pl.BlockSpec(shape, lambda i: tuple(0 for _ in shape))

    def rows(width):
        return pl.BlockSpec((bb, width), lambda i: (i, 0))

    st4 = pl.BlockSpec((bb, N_HEADS, HEAD_DIM, HEAD_DIM), lambda i: (i, 0, 0, 0))
    o3 = pl.BlockSpec((bb, N_HEADS, HEAD_DIM), lambda i: (i, 0, 0))
    outs = pl.pallas_call(
        functools.partial(_mixer_sample_kernel, layer=layer, bb=bb, pos0=pos0),
        grid=(ng,),
        in_specs=[
            rows(MIX_COLS),
            pl.BlockSpec((bb, N_MIX_SLOTS * N_HEADS, HEAD_DIM), lambda i: (i, 0, 0)),
            pl.BlockSpec((None, 4 * WIDTH, bb), lambda i: (i, 0, 0)),
            const((depth, WIDTH)), const((WIDTH, depth)),
            pl.BlockSpec((None, N_VEC_ROWS, WIDTH), lambda i: (layer, 0, 0)),
            pl.BlockSpec((None, 2, N_HEADS, HEAD_DIM), lambda i: (layer, 0, 0, 0)),
            pl.BlockSpec((None, WIDTH, WIDTH), lambda i: (layer, 0, 0)),
            pl.BlockSpec((None, WIDTH, WIDTH), lambda i: (layer, 0, 0)),
            pl.BlockSpec((None, WIDTH, WIDTH), lambda i: (layer, 0, 0)),
            const((WIDTH, bb)), const((WIDTH, bb)),
            st4, rows(WIDTH), rows((CONV_WIDTH - 1) * WIDTH), st4, rows(POOL_BUF * WIDTH),
        ],
        out_specs=[o3, rows(WIDTH), o3, rows(WIDTH),
                   st4, rows(WIDTH), rows((CONV_WIDTH - 1) * WIDTH), st4, rows(POOL_BUF * WIDTH)],
        out_shape=[
            jax.ShapeDtypeStruct((nb, N_HEADS, HEAD_DIM), F32), jax.ShapeDtypeStruct((nb, WIDTH), F32),
            jax.ShapeDtypeStruct((nb, N_HEADS, HEAD_DIM), F32), jax.ShapeDtypeStruct((nb, WIDTH), F32),
            jax.ShapeDtypeStruct(s_hgrn.shape, F32), jax.ShapeDtypeStruct((nb, WIDTH), F32),
            jax.ShapeDtypeStruct(conv2.shape, F32), jax.ShapeDtypeStruct(s_ret.shape, F32),
            jax.ShapeDtypeStruct(pool2.shape, F32),
        ],
        compiler_params=_cparams("parallel"),
        name="mixer_sample",
    )(u, u3, ut, lb_logits, lb_logits.T, vecs, vecs3, wr_bd, wi_bd, wp_bd, cost, sint,
      s_hgrn, s_lru, conv2, s_ret, pool2)
    o_a, o_b, o_c, o_d, n_h, n_l, n_cv, n_r, n_pl = outs
    return (o_a.reshape(nb, WIDTH), o_b, o_c.reshape(nb, WIDTH), o_d, n_h, n_l,
            n_cv.reshape(s_conv.shape), n_r, n_pl.reshape(s_pool.shape))


def _pick_tile(n, pref):
    t = min(n, pref)
    while n % t:
        t //= 2
    return t


def kernel(x_prompt, x_sample, state_hgrn, state_rglru, state_conv, state_retention, state_pool, lb_logits, ffn1_norm, ffn1_up, ffn1_down, mix_norm, w_in, hgrn_norm, conv_w, conv_b, w_rgate, b_rgate, w_igate, b_igate, lru_lambda, ret_norm, w_pool, pool_scale, w_branch, w_o, ffn2_norm, ffn2_up, ffn2_down, final_norm):
    batch, seq, d = x_prompt.shape
    nb, dec_seq, _ = x_sample.shape
    assert dec_seq == 1
    depth = w_in.shape[0]
    past_len = PAST_LEN

    up1, dn1 = ffn1_up.astype(BF16), ffn1_down.astype(BF16)
    up2, dn2 = ffn2_up.astype(BF16), ffn2_down.astype(BF16)
    w_mix = w_in[:, :, :MIX_COLS].astype(BF16)
    w_gate = w_in[:, :, MIX_COLS:].astype(BF16)
    w_br, w_out = w_branch.astype(BF16), w_o.astype(BF16)
    wr_bd = jax.vmap(_block_diag)(w_rgate).astype(BF16)
    wi_bd = jax.vmap(_block_diag)(w_igate).astype(BF16)
    wp_bd = jax.vmap(_block_diag)(w_pool).astype(BF16)
    vecs = jnp.concatenate(
        [hgrn_norm[:, None], conv_w, conv_b[:, None], b_rgate[:, None], b_igate[:, None],
         lru_lambda[:, None], ret_norm[:, None], pool_scale[:, None],
         jnp.zeros((depth, N_VEC_ROWS - 11, WIDTH), F32)], axis=1)
    vecs3 = jnp.stack([hgrn_norm, ret_norm], axis=1).reshape(depth, 2, N_HEADS, HEAD_DIM)
    n1 = ffn1_norm[:, None, :]
    n2 = ffn2_norm[:, None, :]
    nm = mix_norm[:, None, :]
    fg = final_norm[None, :]

    cos_p, sin_p = _rope_tables(jnp.arange(seq, dtype=F32))
    cos_s, sin_s = _rope_tables(past_len + jnp.arange(dec_seq, dtype=F32))

    xp = x_prompt.reshape(batch * seq, d)
    xs = x_sample.reshape(nb * dec_seq, d)
    tm_p = _pick_tile(batch * seq, FFN_ROWS)
    tm_s = _pick_tile(nb, 128)
    tc = _pick_tile(seq, MIXER_ROWS)

    st_p = [[] for _ in range(5)]
    st_s = [[] for _ in range(5)]
    for l in range(depth):
        last = l == depth - 1
        xp = _ffn(xp, n1, up1, dn1, l, fg, final_norm=False, tm=tm_p)
        u_p = _inproj(xp, nm, w_mix, l, tm=_pick_tile(batch * seq, 512))
        oa, ob, oc, od, sh, sl, scv, sr, spl = _mixer_prompt(
            u_p, lb_logits, vecs, wr_bd, wi_bd, wp_bd, cos_p, sin_p, l, batch=batch, seq=seq, tc=tc, pos0=0)
        xp = _merge(xp, nm, (oa, ob, oc, od), w_gate, w_br, w_out, l, tm=_pick_tile(batch * seq, 512))
        xp = _ffn(xp, n2, up2, dn2, l, fg, final_norm=last, tm=tm_p)
        for lst, s in zip(st_p, (jnp.swapaxes(_diag_blocks(sh), 2, 3), sl[:, 0], scv, _diag_blocks(sr), spl)):
            lst.append(s)
        xs = _ffn(xs, n1, up1, dn1, l, fg, final_norm=False, tm=tm_s)
        u_s = _inproj(xs, nm, w_mix, l, tm=tm_s)
        oa, ob, oc, od, nh, nl, ncv, nr, npl = _mixer_sample(
            u_s, lb_logits, vecs, vecs3, wr_bd, wi_bd, wp_bd, cos_s, sin_s, state_hgrn[l], state_rglru[l],
            state_conv[l], state_retention[l], state_pool[l], l, bb=8, pos0=past_len)
        xs = _merge(xs, nm, (oa, ob, oc, od), w_gate, w_br, w_out, l, tm=tm_s)
        xs = _ffn(xs, n2, up2, dn2, l, fg, final_norm=last, tm=tm_s)
        for lst, s in zip(st_s, (nh, nl, ncv, nr, npl)):
            lst.append(s)

    y_p = xp.reshape(batch, seq, d)
    y_s = xs.reshape(nb, dec_seq, d)
    return (y_p, y_s) + tuple(jnp.stack(s) for s in st_p) + tuple(jnp.stack(s) for s in st_s)
```

```python
import functools

import numpy as np
import jax
import jax.numpy as jnp
from jax import lax
from jax.experimental import pallas as pl
from jax.experimental.pallas import tpu as pltpu

F32 = jnp.float32
BF16 = jnp.bfloat16

N_HEADS = 4
HEAD_DIM = 64
WIDTH = N_HEADS * HEAD_DIM
N_MIX_SLOTS = 11
MIX_COLS = N_MIX_SLOTS * WIDTH
CHUNK = 64
N_LEVELS = 6
PAD = 16
CONV_WIDTH = 4
LRU_C = 8.0
POOL_WINDOWS = (2, 4, 8, 16)
POOL_BUF = 15
ROPE_BASE = 10000.0
EPS = 1e-6
F_FLOOR = 1e-30
PAST_LEN = 16384
VMEM_LIMIT = 56 * 1024 * 1024
DENSE_ROWS = 512
MIXER_ROWS = 512


def _cparams(*sem):
    return pltpu.CompilerParams(dimension_semantics=sem, vmem_limit_bytes=VMEM_LIMIT)


def _dot(a, b):
    return jnp.dot(a, b, preferred_element_type=F32)


def _dot_nt(a, b):
    return lax.dot_general(a, b, (((1,), (1,)), ((), ())), preferred_element_type=F32)


def _dot_tn(a, b):
    return lax.dot_general(a, b, (((0,), (0,)), ((), ())), preferred_element_type=F32)


def _rms(x, g):
    return x * lax.rsqrt(jnp.mean(x * x, axis=-1, keepdims=True) + EPS) * g


def _sigmoid(x):
    return 0.5 * jnp.tanh(0.5 * x) + 0.5


def _silu(x):
    hx = 0.5 * x
    return hx * jnp.tanh(hx) + hx


def _softplus(x):
    return jnp.maximum(x, 0.0) + jnp.log1p(jnp.exp(-jnp.abs(x)))


def _neg_expm1(y):
    t = jnp.tanh(0.5 * y)
    return -2.0 * t / (1.0 - t)


def _split2(x):
    hi = x.astype(BF16)
    lo = (x - hi.astype(F32)).astype(BF16)
    return hi, lo


def _split3(x):
    hi = x.astype(BF16)
    r = x - hi.astype(F32)
    mid = r.astype(BF16)
    lo = (r - mid.astype(F32)).astype(BF16)
    return hi, mid, lo


def _lower_bound(lbl, layer):
    m = jnp.max(lbl, axis=0, keepdims=True)
    e = jnp.exp(lbl - m)
    soft = e / jnp.sum(e, axis=0, keepdims=True)
    acc = soft[0:1]
    for l in range(1, layer + 1):
        acc = acc + soft[l:l + 1]
    return acc - soft[0:1]


def _resident(shape, index_map):
    return pl.BlockSpec(shape, index_map, pipeline_mode=pl.Buffered(1))


def _two_group_kernel(*refs, body, n_rows, n_tiles):
    p_refs, s_refs = refs[:n_rows], refs[n_rows:2 * n_rows]
    params, (op_ref, os_ref) = refs[2 * n_rows:-2], refs[-2:]
    i = pl.program_id(0)

    @pl.when(i < n_tiles)
    def _():
        op_ref[...] = body([r[...] for r in p_refs], params)

    @pl.when(i == n_tiles)
    def _():
        os_ref[...] = body([r[...] for r in s_refs], params)


def _two_group_call(body, rows_p, rows_s, params, param_specs, out_cols, tm, name):
    n_p, n_s = rows_p[0].shape[0], rows_s[0].shape[0]
    n_tiles = n_p // tm

    def tile(i):
        return (jnp.minimum(i, n_tiles - 1), 0)

    return pl.pallas_call(
        functools.partial(_two_group_kernel, body=body, n_rows=len(rows_p), n_tiles=n_tiles),
        grid=(n_tiles + 1,),
        in_specs=[pl.BlockSpec((tm, a.shape[1]), tile) for a in rows_p]
        + [pl.BlockSpec(a.shape, lambda i: (0, 0)) for a in rows_s] + list(param_specs),
        out_specs=[pl.BlockSpec((tm, out_cols), tile), pl.BlockSpec((n_s, out_cols), lambda i: (0, 0))],
        out_shape=[jax.ShapeDtypeStruct((n_p, out_cols), F32), jax.ShapeDtypeStruct((n_s, out_cols), F32)],
        compiler_params=_cparams("arbitrary"),
        name=name,
    )(*rows_p, *rows_s, *params)


def _ffn_body(rows, params, *, final_norm):
    (x,), (g_ref, wu_ref, wd_ref, fg_ref) = rows, params
    dff = wd_ref.shape[0]
    h = _rms(x, g_ref[...]).astype(BF16)
    a = _dot(h, wu_ref[:, :dff])
    b = _dot(h, wu_ref[:, dff:])
    act = (_silu(a) * b).astype(BF16)
    y = x + 0.5 * _dot(act, wd_ref[...])
    return _rms(y, fg_ref[...]) if final_norm else y


def _ffn(xp, xs, g, w_up, w_down, layer, final_g, *, final_norm, tm):
    d = xp.shape[1]
    dff = w_down.shape[1]
    specs = [
        _resident((None, 1, d), lambda i: (layer, 0, 0)),
        _resident((None, d, 2 * dff), lambda i: (layer, 0, 0)),
        _resident((None, dff, d), lambda i: (layer, 0, 0)),
        _resident((1, d), lambda i: (0, 0)),
    ]
    return _two_group_call(functools.partial(_ffn_body, final_norm=final_norm), [xp], [xs],
                           (g, w_up, w_down, final_g), specs, d, tm, "ffn")


def _inproj_body(rows, params):
    (x,), (g_ref, w_ref) = rows, params
    return _dot(_rms(x, g_ref[...]).astype(BF16), w_ref[...])


def _inproj(xp, xs, g, w_mix, layer, *, tm):
    d = xp.shape[1]
    cols = w_mix.shape[2]
    specs = [
        _resident((None, 1, d), lambda i: (layer, 0, 0)),
        _resident((None, d, cols), lambda i: (layer, 0, 0)),
    ]
    return _two_group_call(_inproj_body, [xp], [xs], (g, w_mix), specs, cols, tm, "inproj")


def _merge_body(rows, params):
    x, branches = rows[0], rows[1:]
    g_ref, wg_ref, wb_ref, wo_ref = params
    d = x.shape[1]
    h = _rms(x, g_ref[...]).astype(BF16)
    merged = None
    for b, o_b in enumerate(branches):
        gate = _sigmoid(_dot(h, wg_ref[:, b * d:(b + 1) * d]))
        y = _dot(o_b.astype(BF16), wb_ref[b])
        merged = gate * y if merged is None else merged + gate * y
    return x + _dot(merged.astype(BF16), wo_ref[...])


def _merge(xp, xs, g, branches_p, branches_s, w_gate, w_branch, w_o, layer, *, tm):
    d = xp.shape[1]
    w = branches_p[0].shape[1]
    nb = len(branches_p)
    specs = [
        _resident((None, 1, d), lambda i: (layer, 0, 0)),
        _resident((None, d, nb * d), lambda i: (layer, 0, 0)),
        _resident((None, nb, w, d), lambda i: (layer, 0, 0, 0)),
        _resident((None, d, d), lambda i: (layer, 0, 0)),
    ]
    return _two_group_call(_merge_body, [xp, *branches_p], [xs, *branches_s],
                           (g, w_gate, w_branch, w_o), specs, d, tm, "merge")


def _head_block_ones():
    h = np.arange(WIDTH) // HEAD_DIM
    return (h[:, None] == h[None, :]).astype(np.float32)


def _log_gamma():
    return np.log1p(-(2.0 ** (-5.0 - np.arange(N_HEADS, dtype=np.float64))))


def _retention_tables():
    lg = np.repeat(_log_gamma(), HEAD_DIM)[None, :]
    t = np.arange(CHUNK, dtype=np.float64)[:, None]
    q_dec = np.exp((t + 1.0) * lg)
    k_dec = np.exp((CHUNK - 1.0 - t) * lg)
    s_dec = np.exp(CHUNK * lg)
    dt = t.T - t
    dmat = np.concatenate(
        [np.where(dt >= 0, np.exp(dt * g), 0.0) for g in _log_gamma()], axis=1)
    return (jnp.asarray(q_dec, F32), jnp.asarray(k_dec, F32), jnp.asarray(s_dec, F32),
            jnp.asarray(dmat, F32))


def _level_tables():
    t = np.arange(CHUNK)[:, None]
    r = np.arange(CHUNK)[None, :]
    blocks = [r <= t, r > t]
    pairs = []
    for i in range(N_LEVELS):
        m = 1 << i
        mid = t - (t % (2 * m)) + m
        up = (t % (2 * m)) >= m
        blocks.append(np.where(up, (r >= mid) & (r <= t), (r > t) & (r <= mid - 1)))
        same = (t // (2 * m)) == (r // (2 * m))
        pr = same & ((t % (2 * m)) < m) & ((r % (2 * m)) >= m)
        pairs.append(np.tile(pr, (1, N_HEADS)))
    dsum = np.tile(np.concatenate(blocks, axis=0), (1, 3))
    return jnp.asarray(dsum, BF16), jnp.asarray(np.stack(pairs), F32)


def _rope_tables(pos):
    half = HEAD_DIM // 2
    freq = ROPE_BASE ** (-jnp.arange(half, dtype=F32) / half)
    ang = pos[:, None] * freq[None, :]
    cos, sin = jnp.cos(ang), jnp.sin(ang)
    cos_h = jnp.concatenate([cos, cos], axis=-1)
    sin_h = jnp.concatenate([-sin, sin], axis=-1)
    return jnp.tile(cos_h, (1, N_HEADS)), jnp.tile(sin_h, (1, N_HEADS))


def _block_diag(w):
    h, dh, _ = w.shape
    eye = jnp.eye(h, dtype=w.dtype)
    return (eye[:, None, :, None] * w[:, :, None, :]).reshape(h * dh, h * dh)


V_HNORM, V_CW0, V_CB, V_BR, V_BI, V_LAM, V_RNORM, V_PSCALE = 0, 1, 5, 6, 7, 8, 9, 10
N_VEC_ROWS = 16


def _swap_halves(x):
    half = HEAD_DIM // 2
    lane = lax.broadcasted_iota(jnp.int32, (1, 128), 1)
    first = (lane % HEAD_DIM) < half
    parts = []
    for c in range(x.shape[1] // 128):
        xc = x[:, c * 128:(c + 1) * 128]
        parts.append(jnp.where(first, pltpu.roll(xc, 128 - half, 1), pltpu.roll(xc, half, 1)))
    return jnp.concatenate(parts, axis=1)


def _mixer_prompt_kernel(u_ref, lbl_ref, vec_ref, wr_ref, wi_ref, wp_ref, cos_ref, sin_ref,
                         ind_ref, dsum_ref, pair_ref, qdec_ref, kdec_ref, sdec_ref, dmat_ref,
                         oa_ref, ob_ref, oc_ref, od_ref, sh_ref, sl_ref, scv_ref, sr_ref, spl_ref,
                         hst_scr, rst_scr, hl_scr, xcv_scr, xpl_scr, oh_scr, orr_scr, perm_scr,
                         *, layer, tc, pos0):
    t_idx = pl.program_id(1)
    nchunk = tc // CHUNK

    @pl.when(t_idx == 0)
    def _():
        hst_scr[...] = jnp.zeros_like(hst_scr)
        rst_scr[...] = jnp.zeros_like(rst_scr)
        hl_scr[...] = jnp.zeros_like(hl_scr)
        xcv_scr[...] = jnp.zeros_like(xcv_scr)
        xpl_scr[0:PAD, :] = jnp.zeros((PAD, WIDTH), F32)

    def slot(s):
        return u_ref[:, s * WIDTH:(s + 1) * WIDTH]

    def vec(r):
        return vec_ref[r:r + 1, :]

    ind = ind_ref[...]
    lane = lax.broadcasted_iota(jnp.int32, (1, WIDTH), 1)
    hm_bf = [(lane // HEAD_DIM == h).astype(F32).astype(BF16) for h in range(N_HEADS)]
    lane_t = lax.broadcasted_iota(jnp.int32, (1, 128), 1)
    row = lax.broadcasted_iota(jnp.int32, (tc, WIDTH), 0)
    chunks = [slice(n * CHUNK, (n + 1) * CHUNK) for n in range(nchunk)]

    def head_rows(full):
        assert HEAD_DIM == 64 and WIDTH % 128 == 0
        tiles = []
        for c in range(WIDTH // 128):
            even = full[2 * c * CHUNK:(2 * c + 1) * CHUNK, c * 128:(c + 1) * 128]
            odd = full[(2 * c + 1) * CHUNK:(2 * c + 2) * CHUNK, c * 128:(c + 1) * 128]
            tiles.append(jnp.where(lane_t < HEAD_DIM, even, odd))
        return jnp.concatenate(tiles, axis=1)

    def head_sum(x):
        hi, lo = _split2(x)
        return _dot(hi, ind) + _dot(lo, ind)

    lb = _lower_bound(lbl_ref[...], layer)
    uq, z, v = slot(0), slot(1), slot(2)
    q = _silu(uq)
    sg = _sigmoid(z)
    f = lb + (1.0 - lb) * sg
    logf = jnp.log(jnp.maximum(f, F_FLOOR))
    k = (1.0 - lb) * (1.0 - sg)
    l_hi, l_mid, l_lo = _split3(logf)
    v_bf = v.astype(BF16)
    dsum = dsum_ref[...]
    rowc = lax.broadcasted_iota(jnp.int32, (CHUNK, WIDTH), 0)
    upper = [((rowc >> i) & 1) == 1 for i in range(N_LEVELS)]
    dec = [jnp.exp(jnp.minimum(_dot(dsum, jnp.concatenate([l_hi[rs], l_mid[rs], l_lo[rs]], axis=0)), 0.0))
           for rs in chunks]
    sct = [None] * nchunk
    for i in range(N_LEVELS):
        for n, rs in enumerate(chunks):
            fac = dec[n][(2 + i) * CHUNK:(3 + i) * CHUNK]
            zz = (jnp.where(upper[i], q[rs], k[rs]) * fac).astype(BF16)
            qbd = jnp.concatenate([zz * hm_bf[h] for h in range(N_HEADS)], axis=0)
            term = _dot_nt(zz, qbd) * pair_ref[i]
            sct[n] = term if sct[n] is None else sct[n] + term
    o_diag = _dot((q * k).astype(BF16), ind) * v
    upd = [_dot_tn(v_bf[rs], (k[rs] * dec[n][CHUNK:2 * CHUNK]).astype(BF16)) for n, rs in enumerate(chunks)]
    of = [_dot_tn(sct[n].astype(BF16), v_bf[rs]) for n, rs in enumerate(chunks)]

    seg_len = tc // 8
    def strided_rows(ref, start, stride):
        return jnp.concatenate([ref[c, pl.ds(start, 8, stride=stride), :] for c in range(2)], axis=1)

    for c in range(2):
        perm_scr[c] = u_ref[:, 4 * WIDTH + c * 128:4 * WIDTH + (c + 1) * 128]
    slabs = [strided_rows(perm_scr, i, seg_len) for i in range(seg_len)]
    sub0 = lax.broadcasted_iota(jnp.int32, (8, WIDTH), 0) == 0
    front = [jnp.where(sub0, xcv_scr[8 + k:9 + k, :], pltpu.roll(slabs[seg_len + k], 1, 0))
             for k in range(-(CONV_WIDTH - 1), 0)]
    ext = front + slabs
    cw = [vec(V_CW0 + j) for j in range(CONV_WIDTH)]
    xc = jnp.concatenate(
        [vec(V_CB) + sum(cw[j] * ext[i + j] for j in range(CONV_WIDTH)) for i in range(seg_len)], axis=0)
    xc_bf = xc.astype(BF16)
    rg = _sigmoid(_dot(xc_bf, wr_ref[...]) + vec(V_BR))
    ig = _sigmoid(_dot(xc_bf, wi_ref[...]) + vec(V_BI))
    log_a = -LRU_C * rg * _softplus(-vec(V_LAM))
    a_all = jnp.exp(log_a)
    b_all = jnp.sqrt(jnp.maximum(_neg_expm1(2.0 * log_a), 0.0)) * (ig * xc)
    h_loc, a_loc = [b_all[0:8]], [a_all[0:8]]
    for i in range(1, seg_len):
        a_i = a_all[8 * i:8 * i + 8]
        h_loc.append(a_i * h_loc[-1] + b_all[8 * i:8 * i + 8])
        a_loc.append(a_i * a_loc[-1])
    carry = [hl_scr[0:1, :]]
    for s in range(8):
        carry.append(a_loc[-1][s:s + 1] * carry[-1] + h_loc[-1][s:s + 1])
    hl_scr[0:1, :] = carry[8]
    carry8 = jnp.concatenate(carry[0:8], axis=0)
    hperm = jnp.concatenate([h_loc[i] + a_loc[i] * carry8 for i in range(seg_len)], axis=0)
    for c in range(2):
        perm_scr[c] = hperm[:, c * 128:(c + 1) * 128]
    hseq = jnp.concatenate(
        [strided_rows(perm_scr, ((8 * r) % seg_len) * 8 + (8 * r) // seg_len, 8) for r in range(seg_len)],
        axis=0)
    ob_ref[...] = (hseq * jax.nn.gelu(slot(5))).astype(ob_ref.dtype)
    xcv_scr[...] = u_ref[tc - 8:tc, 4 * WIDTH:5 * WIDTH]

    cosv, sinv = cos_ref[...], sin_ref[...]
    cq, ck = slot(6), slot(7)
    qr = cq * cosv + _swap_halves(cq) * sinv
    kr = (ck * cosv + _swap_halves(ck) * sinv) * (HEAD_DIM ** -0.5)
    rv_bf = slot(8).astype(BF16)
    qdec, kdec, sdec, dmat = qdec_ref[...], kdec_ref[...], sdec_ref[...], dmat_ref[...]
    qr_bf, kr_bf = qr.astype(BF16), kr.astype(BF16)
    rsc = []
    for rs in chunks:
        qbd = jnp.concatenate([qr_bf[rs] * hm_bf[h] for h in range(N_HEADS)], axis=0)
        rsc.append((_dot_nt(kr_bf[rs], qbd) * dmat).astype(BF16))
    rof = [_dot_tn(rsc[n], rv_bf[rs]) for n, rs in enumerate(chunks)]
    rupd = [_dot_tn((kr[rs] * kdec).astype(BF16), rv_bf[rs]) for rs in chunks]

    ud = slot(10)
    xpl_scr[PAD:PAD + tc, :] = ud
    assert POOL_WINDOWS == (2, 4, 8, 16) and WIDTH // len(POOL_WINDOWS) == 64 and PAD >= POOL_BUF
    halves = []
    for c in range(2):
        acc = xpl_scr[:, c * 128:(c + 1) * 128]
        built = {}
        for sh in (1, 2, 4, 8):
            acc = acc + pltpu.roll(acc, sh, 0)
            built[2 * sh] = acc
        lo, hi = POOL_WINDOWS[2 * c], POOL_WINDOWS[2 * c + 1]
        halves.append(jnp.where(lane_t < 64, built[lo], built[hi])[PAD:PAD + tc])
    sel = jnp.concatenate(halves, axis=1)
    grp = lane // (WIDTH // len(POOL_WINDOWS))
    winl = jnp.full((1, WIDTH), float(POOL_WINDOWS[-1]), F32)
    for gi in range(len(POOL_WINDOWS) - 2, -1, -1):
        winl = jnp.where(grp == gi, float(POOL_WINDOWS[gi]), winl)
    pos = (row + t_idx * tc).astype(F32) + float(pos0)
    pooled = sel / jnp.minimum(winl, pos + 1.0)
    od = _dot((pooled - ud).astype(BF16), wp_ref[...]) * vec(V_PSCALE)
    od_ref[...] = od.astype(od_ref.dtype)
    xpl_scr[0:PAD, :] = xpl_scr[tc:tc + PAD, :]

    st, s_ret = hst_scr[...], rst_scr[...]
    for n, rs in enumerate(chunks):
        o_c = _dot_nt((q[rs] * dec[n][0:CHUNK]).astype(BF16), st.astype(BF16) * ind) + o_diag[rs]
        st = st * dec[n][CHUNK - 1:CHUNK] + upd[n]
        oh_scr[rs, :] = o_c + head_rows(of[n])
        o_c = _dot((qr[rs] * qdec).astype(BF16), s_ret.astype(BF16) * ind)
        s_ret = s_ret * sdec + rupd[n]
        orr_scr[rs, :] = o_c + head_rows(rof[n])
    hst_scr[...] = st
    rst_scr[...] = s_ret
    o_h = oh_scr[...]
    ms = head_sum(o_h * o_h) * (1.0 / HEAD_DIM)
    oa_ref[...] = (o_h * lax.rsqrt(ms + EPS) * vec(V_HNORM) * _silu(slot(3))).astype(oa_ref.dtype)
    o_r = orr_scr[...]
    mu = head_sum(o_r) * (1.0 / HEAD_DIM)
    dev = o_r - mu
    var = head_sum(dev * dev) * (1.0 / HEAD_DIM)
    oc_ref[...] = (dev * lax.rsqrt(var + EPS) * vec(V_RNORM) * _silu(slot(9))).astype(oc_ref.dtype)

    @pl.when(t_idx == pl.num_programs(1) - 1)
    def _():
        sh_ref[...] = hst_scr[...]
        sr_ref[...] = rst_scr[...]
        sl_ref[...] = hl_scr[0:1, :]
        scv_ref[...] = xcv_scr[8 - (CONV_WIDTH - 1):8, :]
        spl_ref[...] = xpl_scr[PAD + tc - POOL_BUF:PAD + tc, :]


def _mixer_prompt(u, lb_logits, vecs, wr_bd, wi_bd, wp_bd, cos_t, sin_t, layer, *, batch, seq, tc, pos0):
    nt = seq // tc
    ind = jnp.asarray(_head_block_ones(), BF16)
    dsum, pair = _level_tables()
    qdec, kdec, sdec, dmat = _retention_tables()
    depth = lb_logits.shape[0]

    def const(shape):
        return pl.BlockSpec(shape, lambda b, t: tuple(0 for _ in shape))

    row_spec = pl.BlockSpec((tc, WIDTH), lambda b, t: (b * nt + t, 0))
    tab_spec = pl.BlockSpec((tc, WIDTH), lambda b, t: (t, 0))

    def state_spec(rows):
        return pl.BlockSpec((None, rows, WIDTH), lambda b, t: (b, 0, 0))

    n = batch * seq
    outs = pl.pallas_call(
        functools.partial(_mixer_prompt_kernel, layer=layer, tc=tc, pos0=pos0),
        grid=(batch, nt),
        in_specs=[
            pl.BlockSpec((tc, MIX_COLS), lambda b, t: (b * nt + t, 0)),
            const((depth, WIDTH)),
            pl.BlockSpec((None, N_VEC_ROWS, WIDTH), lambda b, t: (layer, 0, 0)),
            pl.BlockSpec((None, WIDTH, WIDTH), lambda b, t: (layer, 0, 0)),
            pl.BlockSpec((None, WIDTH, WIDTH), lambda b, t: (layer, 0, 0)),
            pl.BlockSpec((None, WIDTH, WIDTH), lambda b, t: (layer, 0, 0)),
            tab_spec, tab_spec,
            const((WIDTH, WIDTH)), const(dsum.shape), const(pair.shape),
            const((CHUNK, WIDTH)), const((CHUNK, WIDTH)), const((1, WIDTH)),
            const((CHUNK, N_HEADS * CHUNK)),
        ],
        out_specs=[row_spec, row_spec, row_spec, row_spec,
                   state_spec(WIDTH), state_spec(1), state_spec(CONV_WIDTH - 1), state_spec(WIDTH),
                   state_spec(POOL_BUF)],
        out_shape=[jax.ShapeDtypeStruct((n, WIDTH), BF16)] * 4 + [
            jax.ShapeDtypeStruct((batch, WIDTH, WIDTH), F32),
            jax.ShapeDtypeStruct((batch, 1, WIDTH), F32),
            jax.ShapeDtypeStruct((batch, CONV_WIDTH - 1, WIDTH), F32),
            jax.ShapeDtypeStruct((batch, WIDTH, WIDTH), F32),
            jax.ShapeDtypeStruct((batch, POOL_BUF, WIDTH), F32),
        ],
        scratch_shapes=[
            pltpu.VMEM((WIDTH, WIDTH), F32), pltpu.VMEM((WIDTH, WIDTH), F32), pltpu.VMEM((8, WIDTH), F32),
            pltpu.VMEM((8, WIDTH), F32), pltpu.VMEM((PAD + tc, WIDTH), F32),
            pltpu.VMEM((tc, WIDTH), F32), pltpu.VMEM((tc, WIDTH), F32), pltpu.VMEM((2, tc, 128), F32),
        ],
        compiler_params=_cparams("parallel", "arbitrary"),
        name="mixer_prompt",
    )(u, lb_logits, vecs, wr_bd, wi_bd, wp_bd, cos_t, sin_t, ind, dsum, pair, qdec, kdec, sdec, dmat)
    return outs


def _diag_blocks(s):
    return jnp.stack([s[:, h * HEAD_DIM:(h + 1) * HEAD_DIM, h * HEAD_DIM:(h + 1) * HEAD_DIM]
                      for h in range(N_HEADS)], axis=1)


def _mixer_sample_kernel(u_ref, u3_ref, ut_ref, lbl_ref, lblt_ref, vec_ref, vec3_ref, wr_ref, wi_ref,
                         wp_ref, cost_ref, sint_ref, sh_ref, sl_ref, scv_ref, sr_ref, spl_ref,
                         oa_ref, ob_ref, oc_ref, od_ref, nh_ref, nl_ref, ncv_ref, nr_ref, npl_ref,
                         *, layer, bb, pos0):
    def slot(s):
        return u_ref[:, s * WIDTH:(s + 1) * WIDTH]

    def vec(r):
        return vec_ref[r:r + 1, :]

    lblt = lblt_ref[...]
    m = jnp.max(lblt, axis=1, keepdims=True)
    e = jnp.exp(lblt - m)
    soft = e / jnp.sum(e, axis=1, keepdims=True)
    acc = soft[:, 0:1]
    for l in range(1, layer + 1):
        acc = acc + soft[:, l:l + 1]
    lb_col = acc - soft[:, 0:1]
    aq_t = _silu(ut_ref[0 * WIDTH:1 * WIDTH, :])
    z_t = ut_ref[1 * WIDTH:2 * WIDTH, :]
    f_t = lb_col + (1.0 - lb_col) * _sigmoid(z_t)
    f_t = jnp.maximum(f_t, F_FLOOR)
    k_t = (1.0 - lb_col) * _sigmoid(-z_t)

    def swap_rows(x):
        half = HEAD_DIM // 2
        parts = []
        for h in range(N_HEADS):
            parts.append(x[h * HEAD_DIM + half:(h + 1) * HEAD_DIM])
            parts.append(x[h * HEAD_DIM:h * HEAD_DIM + half])
        return jnp.concatenate(parts, axis=0)

    cost, sint = cost_ref[...], sint_ref[...]
    cq_t, ck_t = ut_ref[2 * WIDTH:3 * WIDTH, :], ut_ref[3 * WIDTH:4 * WIDTH, :]
    rq_t = cq_t * cost + swap_rows(cq_t) * sint
    rk_t = (ck_t * cost + swap_rows(ck_t) * sint) * (HEAD_DIM ** -0.5)
    gam = [float(np.exp(g)) for g in _log_gamma()]

    for b in range(bb):
        for h in range(N_HEADS):
            rows = slice(h * HEAD_DIM, (h + 1) * HEAD_DIM)
            s_new = f_t[rows, b:b + 1] * sh_ref[b, h] + k_t[rows, b:b + 1] * u3_ref[b, 2 * N_HEADS + h:2 * N_HEADS + h + 1, :]
            nh_ref[b, h] = s_new
            oa_ref[b, h:h + 1, :] = jnp.sum(aq_t[rows, b:b + 1] * s_new, axis=0, keepdims=True)
            r_new = gam[h] * sr_ref[b, h] + rk_t[rows, b:b + 1] * u3_ref[b, 8 * N_HEADS + h:8 * N_HEADS + h + 1, :]
            nr_ref[b, h] = r_new
            oc_ref[b, h:h + 1, :] = jnp.sum(rq_t[rows, b:b + 1] * r_new, axis=0, keepdims=True)

    o_a = oa_ref[...]
    ms = jnp.mean(o_a * o_a, axis=-1, keepdims=True)
    ug3 = u3_ref[:, 3 * N_HEADS:4 * N_HEADS, :]
    oa_ref[...] = o_a * lax.rsqrt(ms + EPS) * vec3_ref[0] * _silu(ug3)
    o_c = oc_ref[...]
    mu = jnp.mean(o_c, axis=-1, keepdims=True)
    dev = o_c - mu
    var = jnp.mean(dev * dev, axis=-1, keepdims=True)
    cg3 = u3_ref[:, 9 * N_HEADS:10 * N_HEADS, :]
    oc_ref[...] = dev * lax.rsqrt(var + EPS) * vec3_ref[1] * _silu(cg3)

    ux = slot(4)
    xc = vec(V_CB) + vec(V_CW0 + CONV_WIDTH - 1) * ux
    for j in range(CONV_WIDTH - 1):
        xc = xc + vec(V_CW0 + j) * scv_ref[:, j * WIDTH:(j + 1) * WIDTH]
    xc_bf = xc.astype(BF16)
    rg = _sigmoid(_dot(xc_bf, wr_ref[...]) + vec(V_BR))
    ig = _sigmoid(_dot(xc_bf, wi_ref[...]) + vec(V_BI))
    log_a = -LRU_C * rg * _softplus(-vec(V_LAM))
    a = jnp.exp(log_a)
    b_in = jnp.sqrt(jnp.maximum(_neg_expm1(2.0 * log_a), 0.0)) * (ig * xc)
    hnew = a * sl_ref[...] + b_in
    nl_ref[...] = hnew
    ob_ref[...] = hnew * jax.nn.gelu(slot(5))
    ncv_ref[:, 0:(CONV_WIDTH - 2) * WIDTH] = scv_ref[:, WIDTH:(CONV_WIDTH - 1) * WIDTH]
    ncv_ref[:, (CONV_WIDTH - 2) * WIDTH:] = ux

    ud = slot(10)
    wsum = ud
    sums = []
    nxt = 1
    for win in POOL_WINDOWS:
        while nxt < win:
            wsum = wsum + spl_ref[:, (POOL_BUF - nxt) * WIDTH:(POOL_BUF - nxt + 1) * WIDTH]
            nxt += 1
        sums.append(wsum)
    lane = lax.broadcasted_iota(jnp.int32, (1, WIDTH), 1)
    grp = lane // (WIDTH // len(POOL_WINDOWS))
    sel = sums[-1]
    winl = jnp.full((1, WIDTH), float(POOL_WINDOWS[-1]), F32)
    for gi in range(len(POOL_WINDOWS) - 2, -1, -1):
        sel = jnp.where(grp == gi, sums[gi], sel)
        winl = jnp.where(grp == gi, float(POOL_WINDOWS[gi]), winl)
    pooled = sel / jnp.minimum(winl, float(pos0) + 1.0)
    od_ref[...] = _dot((pooled - ud).astype(BF16), wp_ref[...]) * vec(V_PSCALE)
    npl_ref[:, 0:(POOL_BUF - 1) * WIDTH] = spl_ref[:, WIDTH:POOL_BUF * WIDTH]
    npl_ref[:, (POOL_BUF - 1) * WIDTH:] = ud


def _mixer_sample(u, lb_logits, vecs, vecs3, wr_bd, wi_bd, wp_bd, cos_t, sin_t, s_hgrn, s_lru, s_conv,
                  s_ret, s_pool, layer, *, bb, pos0):
    nb = u.shape[0]
    ng = nb // bb
    depth = lb_logits.shape[0]
    u3 = u.reshape(nb, N_MIX_SLOTS * N_HEADS, HEAD_DIM)
    cols = jnp.concatenate([u[:, 0:2 * WIDTH], u[:, 6 * WIDTH:8 * WIDTH]], axis=1)
    ut = cols.reshape(ng, bb, 4 * WIDTH).transpose(0, 2, 1)
    cost = jnp.broadcast_to(cos_t.reshape(WIDTH, 1), (WIDTH, bb))
    sint = jnp.broadcast_to(sin_t.reshape(WIDTH, 1), (WIDTH, bb))
    conv2 = s_conv.reshape(depth, nb, (CONV_WIDTH - 1) * WIDTH)
    pool2 = s_pool.reshape(depth, nb, POOL_BUF * WIDTH)

    def const(shape):
        return pl.BlockSpec(shape, lambda i: tuple(0 for _ in shape))

    def rows(width):
        return pl.BlockSpec((bb, width), lambda i: (i, 0))

    def lrows(width):
        return pl.BlockSpec((None, bb, width), lambda i: (layer, i, 0))

    st4 = pl.BlockSpec((bb, N_HEADS, HEAD_DIM, HEAD_DIM), lambda i: (i, 0, 0, 0))
    lst4 = pl.BlockSpec((None, bb, N_HEADS, HEAD_DIM, HEAD_DIM), lambda i: (layer, i, 0, 0, 0))
    o3 = pl.BlockSpec((bb, N_HEADS, HEAD_DIM), lambda i: (i, 0, 0))
    outs = pl.pallas_call(
        functools.partial(_mixer_sample_kernel, layer=layer, bb=bb, pos0=pos0),
        grid=(ng,),
        in_specs=[
            rows(MIX_COLS),
            pl.BlockSpec((bb, N_MIX_SLOTS * N_HEADS, HEAD_DIM), lambda i: (i, 0, 0)),
            pl.BlockSpec((None, 4 * WIDTH, bb), lambda i: (i, 0, 0)),
            const((depth, WIDTH)), const((WIDTH, depth)),
            pl.BlockSpec((None, N_VEC_ROWS, WIDTH), lambda i: (layer, 0, 0)),
            pl.BlockSpec((None, 2, N_HEADS, HEAD_DIM), lambda i: (layer, 0, 0, 0)),
            pl.BlockSpec((None, WIDTH, WIDTH), lambda i: (layer, 0, 0)),
            pl.BlockSpec((None, WIDTH, WIDTH), lambda i: (layer, 0, 0)),
            pl.BlockSpec((None, WIDTH, WIDTH), lambda i: (layer, 0, 0)),
            const((WIDTH, bb)), const((WIDTH, bb)),
            lst4, lrows(WIDTH), lrows((CONV_WIDTH - 1) * WIDTH), lst4, lrows(POOL_BUF * WIDTH),
        ],
        out_specs=[o3, rows(WIDTH), o3, rows(WIDTH),
                   st4, rows(WIDTH), rows((CONV_WIDTH - 1) * WIDTH), st4, rows(POOL_BUF * WIDTH)],
        out_shape=[
            jax.ShapeDtypeStruct((nb, N_HEADS, HEAD_DIM), F32), jax.ShapeDtypeStruct((nb, WIDTH), F32),
            jax.ShapeDtypeStruct((nb, N_HEADS, HEAD_DIM), F32), jax.ShapeDtypeStruct((nb, WIDTH), F32),
            jax.ShapeDtypeStruct(s_hgrn.shape[1:], F32), jax.ShapeDtypeStruct((nb, WIDTH), F32),
            jax.ShapeDtypeStruct(conv2.shape[1:], F32), jax.ShapeDtypeStruct(s_ret.shape[1:], F32),
            jax.ShapeDtypeStruct(pool2.shape[1:], F32),
        ],
        compiler_params=_cparams("parallel"),
        name="mixer_sample",
    )(u, u3, ut, lb_logits, lb_logits.T, vecs, vecs3, wr_bd, wi_bd, wp_bd, cost, sint,
      s_hgrn, s_lru, conv2, s_ret, pool2)
    o_a, o_b, o_c, o_d, n_h, n_l, n_cv, n_r, n_pl = outs
    return (o_a.reshape(nb, WIDTH), o_b, o_c.reshape(nb, WIDTH), o_d, n_h, n_l,
            n_cv.reshape(s_conv.shape[1:]), n_r, n_pl.reshape(s_pool.shape[1:]))


def _pick_tile(n, pref):
    t = min(n, pref)
    while n % t:
        t //= 2
    return t


def kernel(x_prompt, x_sample, state_hgrn, state_rglru, state_conv, state_retention, state_pool, lb_logits, ffn1_norm, ffn1_up, ffn1_down, mix_norm, w_in, hgrn_norm, conv_w, conv_b, w_rgate, b_rgate, w_igate, b_igate, lru_lambda, ret_norm, w_pool, pool_scale, w_branch, w_o, ffn2_norm, ffn2_up, ffn2_down, final_norm):
    batch, seq, d = x_prompt.shape
    nb, dec_seq, _ = x_sample.shape
    assert dec_seq == 1
    depth = w_in.shape[0]
    past_len = PAST_LEN

    up1, dn1 = ffn1_up.astype(BF16), ffn1_down.astype(BF16)
    up2, dn2 = ffn2_up.astype(BF16), ffn2_down.astype(BF16)
    w_mix = w_in[:, :, :MIX_COLS].astype(BF16)
    w_gate = w_in[:, :, MIX_COLS:].astype(BF16)
    w_br, w_out = w_branch.astype(BF16), w_o.astype(BF16)
    wr_bd = jax.vmap(_block_diag)(w_rgate).astype(BF16)
    wi_bd = jax.vmap(_block_diag)(w_igate).astype(BF16)
    wp_bd = jax.vmap(_block_diag)(w_pool).astype(BF16)
    vecs = jnp.concatenate(
        [hgrn_norm[:, None], conv_w, conv_b[:, None], b_rgate[:, None], b_igate[:, None],
         lru_lambda[:, None], ret_norm[:, None], pool_scale[:, None],
         jnp.zeros((depth, N_VEC_ROWS - 11, WIDTH), F32)], axis=1)
    vecs3 = jnp.stack([hgrn_norm, ret_norm], axis=1).reshape(depth, 2, N_HEADS, HEAD_DIM)
    n1 = ffn1_norm[:, None, :]
    n2 = ffn2_norm[:, None, :]
    nm = mix_norm[:, None, :]
    fg = final_norm[None, :]

    cos_p, sin_p = _rope_tables(jnp.arange(seq, dtype=F32))
    cos_s, sin_s = _rope_tables(past_len + jnp.arange(dec_seq, dtype=F32))

    xp = x_prompt.reshape(batch * seq, d)
    xs = x_sample.reshape(nb * dec_seq, d)
    tm_p = _pick_tile(batch * seq, DENSE_ROWS)
    tc = _pick_tile(seq, MIXER_ROWS)

    st_p = [[] for _ in range(5)]
    st_s = [[] for _ in range(5)]
    for l in range(depth):
        last = l == depth - 1
        xp, xs = _ffn(xp, xs, n1, up1, dn1, l, fg, final_norm=False, tm=tm_p)
        u_p, u_s = _inproj(xp, xs, nm, w_mix, l, tm=tm_p)
        *br_p, sh, sl, scv, sr, spl = _mixer_prompt(
            u_p, lb_logits, vecs, wr_bd, wi_bd, wp_bd, cos_p, sin_p, l, batch=batch, seq=seq, tc=tc, pos0=0)
        *br_s, nh, nl, ncv, nr, npl = _mixer_sample(
            u_s, lb_logits, vecs, vecs3, wr_bd, wi_bd, wp_bd, cos_s, sin_s, state_hgrn, state_rglru,
            state_conv, state_retention, state_pool, l, bb=8, pos0=past_len)
        xp, xs = _merge(xp, xs, nm, br_p, br_s, w_gate, w_br, w_out, l, tm=tm_p)
        xp, xs = _ffn(xp, xs, n2, up2, dn2, l, fg, final_norm=last, tm=tm_p)
        for lst, s in zip(st_p, (jnp.swapaxes(_diag_blocks(sh), 2, 3), sl[:, 0], scv, _diag_blocks(sr), spl)):
            lst.append(s)
        for lst, s in zip(st_s, (nh, nl, ncv, nr, npl)):
            lst.append(s)

    y_p = xp.reshape(batch, seq, d)
    y_s = xs.reshape(nb, dec_seq, d)
    return (y_p, y_s) + tuple(jnp.stack(s) for s in st_p) + tuple(jnp.stack(s) for s in st_s)
```

```python
import functools

import numpy as np
import jax
import jax.numpy as jnp
from jax import lax
from jax.experimental import pallas as pl
from jax.experimental.pallas import tpu as pltpu

F32 = jnp.float32
BF16 = jnp.bfloat16

N_HEADS = 4
HEAD_DIM = 64
WIDTH = N_HEADS * HEAD_DIM
N_MIX_SLOTS = 11
MIX_COLS = N_MIX_SLOTS * WIDTH
CHUNK = 64
N_LEVELS = 6
PAD = 16
CONV_WIDTH = 4
LRU_C = 8.0
POOL_WINDOWS = (2, 4, 8, 16)
POOL_BUF = 15
ROPE_BASE = 10000.0
EPS = 1e-6
F_FLOOR = 1e-30
PAST_LEN = 16384
VMEM_LIMIT = 56 * 1024 * 1024
DENSE_ROWS = 512
MIXER_ROWS = 512


def _cparams(*sem):
    return pltpu.CompilerParams(dimension_semantics=sem, vmem_limit_bytes=VMEM_LIMIT)


def _dot(a, b):
    return jnp.dot(a, b, preferred_element_type=F32)


def _dot_nt(a, b):
    return lax.dot_general(a, b, (((1,), (1,)), ((), ())), preferred_element_type=F32)


def _dot_tn(a, b):
    return lax.dot_general(a, b, (((0,), (0,)), ((), ())), preferred_element_type=F32)


def _rms(x, g):
    return x * lax.rsqrt(jnp.mean(x * x, axis=-1, keepdims=True) + EPS) * g


def _sigmoid(x):
    return 0.5 * jnp.tanh(0.5 * x) + 0.5


def _silu(x):
    hx = 0.5 * x
    return hx * jnp.tanh(hx) + hx


def _softplus(x):
    return jnp.maximum(x, 0.0) + jnp.log1p(jnp.exp(-jnp.abs(x)))


def _neg_expm1(y):
    t = jnp.tanh(0.5 * y)
    return -2.0 * t / (1.0 - t)


def _split2(x):
    hi = x.astype(BF16)
    lo = (x - hi.astype(F32)).astype(BF16)
    return hi, lo


def _split3(x):
    hi = x.astype(BF16)
    r = x - hi.astype(F32)
    mid = r.astype(BF16)
    lo = (r - mid.astype(F32)).astype(BF16)
    return hi, mid, lo


def _lower_bound(lbl, layer):
    m = jnp.max(lbl, axis=0, keepdims=True)
    e = jnp.exp(lbl - m)
    soft = e / jnp.sum(e, axis=0, keepdims=True)
    acc = soft[0:1]
    for l in range(1, layer + 1):
        acc = acc + soft[l:l + 1]
    return acc - soft[0:1]


def _resident(shape, index_map):
    return pl.BlockSpec(shape, index_map, pipeline_mode=pl.Buffered(1))


def _two_group_kernel(*refs, body, n_rows, n_tiles):
    p_refs, s_refs = refs[:n_rows], refs[n_rows:2 * n_rows]
    params, (op_ref, os_ref) = refs[2 * n_rows:-2], refs[-2:]
    i = pl.program_id(0)

    @pl.when(i < n_tiles)
    def _():
        op_ref[...] = body([r[...] for r in p_refs], params)

    @pl.when(i == n_tiles)
    def _():
        os_ref[...] = body([r[...] for r in s_refs], params)


def _two_group_call(body, rows_p, rows_s, params, param_specs, out_cols, tm, name):
    n_p, n_s = rows_p[0].shape[0], rows_s[0].shape[0]
    n_tiles = n_p // tm

    def tile(i):
        return (jnp.minimum(i, n_tiles - 1), 0)

    return pl.pallas_call(
        functools.partial(_two_group_kernel, body=body, n_rows=len(rows_p), n_tiles=n_tiles),
        grid=(n_tiles + 1,),
        in_specs=[pl.BlockSpec((tm, a.shape[1]), tile) for a in rows_p]
        + [pl.BlockSpec(a.shape, lambda i: (0, 0)) for a in rows_s] + list(param_specs),
        out_specs=[pl.BlockSpec((tm, out_cols), tile), pl.BlockSpec((n_s, out_cols), lambda i: (0, 0))],
        out_shape=[jax.ShapeDtypeStruct((n_p, out_cols), F32), jax.ShapeDtypeStruct((n_s, out_cols), F32)],
        compiler_params=_cparams("arbitrary"),
        name=name,
    )(*rows_p, *rows_s, *params)


def _ffn_body(rows, params, *, final_norm):
    (x,), (g_ref, wu_ref, wd_ref, fg_ref) = rows, params
    dff = wd_ref.shape[0]
    h = _rms(x, g_ref[...]).astype(BF16)
    a = _dot(h, wu_ref[:, :dff])
    b = _dot(h, wu_ref[:, dff:])
    act = (_silu(a) * b).astype(BF16)
    y = x + 0.5 * _dot(act, wd_ref[...])
    return _rms(y, fg_ref[...]) if final_norm else y


def _ffn(xp, xs, g, w_up, w_down, layer, final_g, *, final_norm, tm):
    d = xp.shape[1]
    dff = w_down.shape[1]
    specs = [
        _resident((None, 1, d), lambda i: (layer, 0, 0)),
        _resident((None, d, 2 * dff), lambda i: (layer, 0, 0)),
        _resident((None, dff, d), lambda i: (layer, 0, 0)),
        _resident((1, d), lambda i: (0, 0)),
    ]
    return _two_group_call(functools.partial(_ffn_body, final_norm=final_norm), [xp], [xs],
                           (g, w_up, w_down, final_g), specs, d, tm, "ffn")


def _inproj_body(rows, params):
    (x,), (g_ref, w_ref) = rows, params
    return _dot(_rms(x, g_ref[...]).astype(BF16), w_ref[...])


def _inproj(xp, xs, g, w_in, layer, *, tm):
    d = xp.shape[1]
    specs = [
        _resident((None, 1, d), lambda i: (layer, 0, 0)),
        _resident((None, d, MIX_COLS), lambda i: (layer, 0, 0)),
    ]
    return _two_group_call(_inproj_body, [xp], [xs], (g, w_in), specs, MIX_COLS, tm, "inproj")


def _merge_body(rows, params):
    x, branches = rows[0], rows[1:]
    g_ref, win_ref, wb_ref, wo_ref = params
    d = x.shape[1]
    h = _rms(x, g_ref[...]).astype(BF16)
    merged = None
    for b, o_b in enumerate(branches):
        gate = _sigmoid(_dot(h, win_ref[:, MIX_COLS + b * d:MIX_COLS + (b + 1) * d]))
        y = _dot(o_b.astype(BF16), wb_ref[b])
        merged = gate * y if merged is None else merged + gate * y
    return x + _dot(merged.astype(BF16), wo_ref[...])


def _merge(xp, xs, g, branches_p, branches_s, w_in, w_branch, w_o, layer, *, tm):
    d = xp.shape[1]
    w = branches_p[0].shape[1]
    nb = len(branches_p)
    assert w_in.shape[2] == MIX_COLS + nb * d and MIX_COLS % 128 == 0
    specs = [
        _resident((None, 1, d), lambda i: (layer, 0, 0)),
        _resident((None, d, w_in.shape[2]), lambda i: (layer, 0, 0)),
        _resident((None, nb, w, d), lambda i: (layer, 0, 0, 0)),
        _resident((None, d, d), lambda i: (layer, 0, 0)),
    ]
    return _two_group_call(_merge_body, [xp, *branches_p], [xs, *branches_s],
                           (g, w_in, w_branch, w_o), specs, d, tm, "merge")


def _head_block_ones():
    h = np.arange(WIDTH) // HEAD_DIM
    return (h[:, None] == h[None, :]).astype(np.float32)


def _log_gamma():
    return np.log1p(-(2.0 ** (-5.0 - np.arange(N_HEADS, dtype=np.float64))))


def _retention_tables():
    lg = np.repeat(_log_gamma(), HEAD_DIM)[None, :]
    t = np.arange(CHUNK, dtype=np.float64)[:, None]
    q_dec = np.exp((t + 1.0) * lg)
    k_dec = np.exp((CHUNK - 1.0 - t) * lg)
    s_dec = np.exp(CHUNK * lg)
    dt = t.T - t
    dmat = np.concatenate(
        [np.where(dt >= 0, np.exp(dt * g), 0.0) for g in _log_gamma()], axis=1)
    return (jnp.asarray(q_dec, F32), jnp.asarray(k_dec, F32), jnp.asarray(s_dec, F32),
            jnp.asarray(dmat, F32))


def _level_tables():
    t = np.arange(CHUNK)[:, None]
    r = np.arange(CHUNK)[None, :]
    blocks = [r <= t, r > t]
    pairs = []
    for i in range(N_LEVELS):
        m = 1 << i
        mid = t - (t % (2 * m)) + m
        up = (t % (2 * m)) >= m
        blocks.append(np.where(up, (r >= mid) & (r <= t), (r > t) & (r <= mid - 1)))
        same = (t // (2 * m)) == (r // (2 * m))
        pr = same & ((t % (2 * m)) < m) & ((r % (2 * m)) >= m)
        pairs.append(np.tile(pr, (1, N_HEADS)))
    dsum = np.tile(np.concatenate(blocks, axis=0), (1, 3))
    return jnp.asarray(dsum, BF16), jnp.asarray(np.stack(pairs), F32)


def _rope_tables(pos):
    half = HEAD_DIM // 2
    freq = ROPE_BASE ** (-jnp.arange(half, dtype=F32) / half)
    ang = pos[:, None] * freq[None, :]
    cos, sin = jnp.cos(ang), jnp.sin(ang)
    cos_h = jnp.concatenate([cos, cos], axis=-1)
    sin_h = jnp.concatenate([-sin, sin], axis=-1)
    return jnp.tile(cos_h, (1, N_HEADS)), jnp.tile(sin_h, (1, N_HEADS))


def _block_diag(w):
    h, dh, _ = w.shape
    eye = jnp.eye(h, dtype=w.dtype)
    return (eye[:, None, :, None] * w[:, :, None, :]).reshape(h * dh, h * dh)


V_HNORM, V_CW0, V_CB, V_BR, V_BI, V_LAM, V_RNORM, V_PSCALE = 0, 1, 5, 6, 7, 8, 9, 10
N_VEC_ROWS = 16


def _swap_halves(x):
    half = HEAD_DIM // 2
    lane = lax.broadcasted_iota(jnp.int32, (1, 128), 1)
    first = (lane % HEAD_DIM) < half
    parts = []
    for c in range(x.shape[1] // 128):
        xc = x[:, c * 128:(c + 1) * 128]
        parts.append(jnp.where(first, pltpu.roll(xc, 128 - half, 1), pltpu.roll(xc, half, 1)))
    return jnp.concatenate(parts, axis=1)


def _mixer_prompt_kernel(u_ref, lbl_ref, vec_ref, wr_ref, wi_ref, wp_ref, cos_ref, sin_ref,
                         ind_ref, dsum_ref, pair_ref, qdec_ref, kdec_ref, sdec_ref, dmat_ref,
                         oa_ref, ob_ref, oc_ref, od_ref, sh_ref, sl_ref, scv_ref, sr_ref, spl_ref,
                         hst_scr, rst_scr, hl_scr, xcv_scr, xpl_scr, oh_scr, orr_scr, perm_scr,
                         *, layer, tc, pos0):
    t_idx = pl.program_id(1)
    nchunk = tc // CHUNK

    @pl.when(t_idx == 0)
    def _():
        hst_scr[...] = jnp.zeros_like(hst_scr)
        rst_scr[...] = jnp.zeros_like(rst_scr)
        hl_scr[...] = jnp.zeros_like(hl_scr)
        xcv_scr[...] = jnp.zeros_like(xcv_scr)
        xpl_scr[0:PAD, :] = jnp.zeros((PAD, WIDTH), F32)

    def slot(s):
        return u_ref[:, s * WIDTH:(s + 1) * WIDTH]

    def vec(r):
        return vec_ref[r:r + 1, :]

    ind = ind_ref[...]
    lane = lax.broadcasted_iota(jnp.int32, (1, WIDTH), 1)
    hm_bf = [(lane // HEAD_DIM == h).astype(F32).astype(BF16) for h in range(N_HEADS)]
    lane_t = lax.broadcasted_iota(jnp.int32, (1, 128), 1)
    row = lax.broadcasted_iota(jnp.int32, (tc, WIDTH), 0)
    chunks = [slice(n * CHUNK, (n + 1) * CHUNK) for n in range(nchunk)]

    def head_rows(full):
        assert HEAD_DIM == 64 and WIDTH % 128 == 0
        tiles = []
        for c in range(WIDTH // 128):
            even = full[2 * c * CHUNK:(2 * c + 1) * CHUNK, c * 128:(c + 1) * 128]
            odd = full[(2 * c + 1) * CHUNK:(2 * c + 2) * CHUNK, c * 128:(c + 1) * 128]
            tiles.append(jnp.where(lane_t < HEAD_DIM, even, odd))
        return jnp.concatenate(tiles, axis=1)

    def head_sum(x):
        hi, lo = _split2(x)
        return _dot(hi, ind) + _dot(lo, ind)

    lb = _lower_bound(lbl_ref[...], layer)
    uq, z, v = slot(0), slot(1), slot(2)
    q = _silu(uq)
    sg = _sigmoid(z)
    f = lb + (1.0 - lb) * sg
    logf = jnp.log(jnp.maximum(f, F_FLOOR))
    k = (1.0 - lb) * (1.0 - sg)
    l_hi, l_mid, l_lo = _split3(logf)
    v_bf = v.astype(BF16)
    dsum = dsum_ref[...]
    rowc = lax.broadcasted_iota(jnp.int32, (CHUNK, WIDTH), 0)
    upper = [((rowc >> i) & 1) == 1 for i in range(N_LEVELS)]
    dec = [jnp.exp(jnp.minimum(_dot(dsum, jnp.concatenate([l_hi[rs], l_mid[rs], l_lo[rs]], axis=0)), 0.0))
           for rs in chunks]
    sct = [None] * nchunk
    for i in range(N_LEVELS):
        for n, rs in enumerate(chunks):
            fac = dec[n][(2 + i) * CHUNK:(3 + i) * CHUNK]
            zz = (jnp.where(upper[i], q[rs], k[rs]) * fac).astype(BF16)
            qbd = jnp.concatenate([zz * hm_bf[h] for h in range(N_HEADS)], axis=0)
            term = _dot_nt(zz, qbd) * pair_ref[i]
            sct[n] = term if sct[n] is None else sct[n] + term
    o_diag = _dot((q * k).astype(BF16), ind) * v
    upd = [_dot_tn(v_bf[rs], (k[rs] * dec[n][CHUNK:2 * CHUNK]).astype(BF16)) for n, rs in enumerate(chunks)]
    of = [_dot_tn(sct[n].astype(BF16), v_bf[rs]) for n, rs in enumerate(chunks)]

    seg_len = tc // 8
    def strided_rows(ref, start, stride):
        return jnp.concatenate([ref[c, pl.ds(start, 8, stride=stride), :] for c in range(2)], axis=1)

    for c in range(2):
        perm_scr[c] = u_ref[:, 4 * WIDTH + c * 128:4 * WIDTH + (c + 1) * 128]
    slabs = [strided_rows(perm_scr, i, seg_len) for i in range(seg_len)]
    sub0 = lax.broadcasted_iota(jnp.int32, (8, WIDTH), 0) == 0
    front = [jnp.where(sub0, xcv_scr[8 + k:9 + k, :], pltpu.roll(slabs[seg_len + k], 1, 0))
             for k in range(-(CONV_WIDTH - 1), 0)]
    ext = front + slabs
    cw = [vec(V_CW0 + j) for j in range(CONV_WIDTH)]
    xc = jnp.concatenate(
        [vec(V_CB) + sum(cw[j] * ext[i + j] for j in range(CONV_WIDTH)) for i in range(seg_len)], axis=0)
    xc_bf = xc.astype(BF16)
    rg = _sigmoid(_dot(xc_bf, wr_ref[...]) + vec(V_BR))
    ig = _sigmoid(_dot(xc_bf, wi_ref[...]) + vec(V_BI))
    log_a = -LRU_C * rg * _softplus(-vec(V_LAM))
    a_all = jnp.exp(log_a)
    b_all = jnp.sqrt(jnp.maximum(_neg_expm1(2.0 * log_a), 0.0)) * (ig * xc)
    h_loc, a_loc = [b_all[0:8]], [a_all[0:8]]
    for i in range(1, seg_len):
        a_i = a_all[8 * i:8 * i + 8]
        h_loc.append(a_i * h_loc[-1] + b_all[8 * i:8 * i + 8])
        a_loc.append(a_i * a_loc[-1])
    carry = [hl_scr[0:1, :]]
    for s in range(8):
        carry.append(a_loc[-1][s:s + 1] * carry[-1] + h_loc[-1][s:s + 1])
    hl_scr[0:1, :] = carry[8]
    carry8 = jnp.concatenate(carry[0:8], axis=0)
    hperm = jnp.concatenate([h_loc[i] + a_loc[i] * carry8 for i in range(seg_len)], axis=0)
    for c in range(2):
        perm_scr[c] = hperm[:, c * 128:(c + 1) * 128]
    hseq = jnp.concatenate(
        [strided_rows(perm_scr, ((8 * r) % seg_len) * 8 + (8 * r) // seg_len, 8) for r in range(seg_len)],
        axis=0)
    ob_ref[...] = (hseq * jax.nn.gelu(slot(5))).astype(ob_ref.dtype)
    xcv_scr[...] = u_ref[tc - 8:tc, 4 * WIDTH:5 * WIDTH]

    cosv, sinv = cos_ref[...], sin_ref[...]
    cq, ck = slot(6), slot(7)
    qr = cq * cosv + _swap_halves(cq) * sinv
    kr = (ck * cosv + _swap_halves(ck) * sinv) * (HEAD_DIM ** -0.5)
    rv_bf = slot(8).astype(BF16)
    qdec, kdec, sdec, dmat = qdec_ref[...], kdec_ref[...], sdec_ref[...], dmat_ref[...]
    qr_bf, kr_bf = qr.astype(BF16), kr.astype(BF16)
    rsc = []
    for rs in chunks:
        qbd = jnp.concatenate([qr_bf[rs] * hm_bf[h] for h in range(N_HEADS)], axis=0)
        rsc.append((_dot_nt(kr_bf[rs], qbd) * dmat).astype(BF16))
    rof = [_dot_tn(rsc[n], rv_bf[rs]) for n, rs in enumerate(chunks)]
    rupd = [_dot_tn((kr[rs] * kdec).astype(BF16), rv_bf[rs]) for rs in chunks]

    ud = slot(10)
    xpl_scr[PAD:PAD + tc, :] = ud
    assert POOL_WINDOWS == (2, 4, 8, 16) and WIDTH // len(POOL_WINDOWS) == 64 and PAD >= POOL_BUF
    halves = []
    for c in range(2):
        acc = xpl_scr[:, c * 128:(c + 1) * 128]
        built = {}
        for sh in (1, 2, 4, 8):
            acc = acc + pltpu.roll(acc, sh, 0)
            built[2 * sh] = acc
        lo, hi = POOL_WINDOWS[2 * c], POOL_WINDOWS[2 * c + 1]
        halves.append(jnp.where(lane_t < 64, built[lo], built[hi])[PAD:PAD + tc])
    sel = jnp.concatenate(halves, axis=1)
    grp = lane // (WIDTH // len(POOL_WINDOWS))
    winl = jnp.full((1, WIDTH), float(POOL_WINDOWS[-1]), F32)
    for gi in range(len(POOL_WINDOWS) - 2, -1, -1):
        winl = jnp.where(grp == gi, float(POOL_WINDOWS[gi]), winl)
    pos = (row + t_idx * tc).astype(F32) + float(pos0)
    pooled = sel / jnp.minimum(winl, pos + 1.0)
    od = _dot((pooled - ud).astype(BF16), wp_ref[...]) * vec(V_PSCALE)
    od_ref[...] = od.astype(od_ref.dtype)
    xpl_scr[0:PAD, :] = xpl_scr[tc:tc + PAD, :]

    st, s_ret = hst_scr[...], rst_scr[...]
    for n, rs in enumerate(chunks):
        o_c = _dot_nt((q[rs] * dec[n][0:CHUNK]).astype(BF16), st.astype(BF16) * ind) + o_diag[rs]
        st = st * dec[n][CHUNK - 1:CHUNK] + upd[n]
        oh_scr[rs, :] = o_c + head_rows(of[n])
        o_c = _dot((qr[rs] * qdec).astype(BF16), s_ret.astype(BF16) * ind)
        s_ret = s_ret * sdec + rupd[n]
        orr_scr[rs, :] = o_c + head_rows(rof[n])
    hst_scr[...] = st
    rst_scr[...] = s_ret
    o_h = oh_scr[...]
    ms = head_sum(o_h * o_h) * (1.0 / HEAD_DIM)
    oa_ref[...] = (o_h * lax.rsqrt(ms + EPS) * vec(V_HNORM) * _silu(slot(3))).astype(oa_ref.dtype)
    o_r = orr_scr[...]
    mu = head_sum(o_r) * (1.0 / HEAD_DIM)
    dev = o_r - mu
    var = head_sum(dev * dev) * (1.0 / HEAD_DIM)
    oc_ref[...] = (dev * lax.rsqrt(var + EPS) * vec(V_RNORM) * _silu(slot(9))).astype(oc_ref.dtype)

    @pl.when(t_idx == pl.num_programs(1) - 1)
    def _():
        sh_ref[...] = hst_scr[...]
        sr_ref[...] = rst_scr[...]
        sl_ref[...] = hl_scr[0:1, :]
        scv_ref[...] = xcv_scr[8 - (CONV_WIDTH - 1):8, :]
        spl_ref[...] = xpl_scr[PAD + tc - POOL_BUF:PAD + tc, :]


def _mixer_prompt(u, lb_logits, vecs, wr_bd, wi_bd, wp_bd, cos_t, sin_t, layer, *, batch, seq, tc, pos0):
    nt = seq // tc
    ind = jnp.asarray(_head_block_ones(), BF16)
    dsum, pair = _level_tables()
    qdec, kdec, sdec, dmat = _retention_tables()
    depth = lb_logits.shape[0]

    def const(shape):
        return pl.BlockSpec(shape, lambda b, t: tuple(0 for _ in shape))

    row_spec = pl.BlockSpec((tc, WIDTH), lambda b, t: (b * nt + t, 0))
    tab_spec = pl.BlockSpec((tc, WIDTH), lambda b, t: (t, 0))

    def state_spec(rows):
        return pl.BlockSpec((None, rows, WIDTH), lambda b, t: (b, 0, 0))

    n = batch * seq
    outs = pl.pallas_call(
        functools.partial(_mixer_prompt_kernel, layer=layer, tc=tc, pos0=pos0),
        grid=(batch, nt),
        in_specs=[
            pl.BlockSpec((tc, MIX_COLS), lambda b, t: (b * nt + t, 0)),
            const((depth, WIDTH)),
            pl.BlockSpec((None, N_VEC_ROWS, WIDTH), lambda b, t: (layer, 0, 0)),
            pl.BlockSpec((None, WIDTH, WIDTH), lambda b, t: (layer, 0, 0)),
            pl.BlockSpec((None, WIDTH, WIDTH), lambda b, t: (layer, 0, 0)),
            pl.BlockSpec((None, WIDTH, WIDTH), lambda b, t: (layer, 0, 0)),
            tab_spec, tab_spec,
            const((WIDTH, WIDTH)), const(dsum.shape), const(pair.shape),
            const((CHUNK, WIDTH)), const((CHUNK, WIDTH)), const((1, WIDTH)),
            const((CHUNK, N_HEADS * CHUNK)),
        ],
        out_specs=[row_spec, row_spec, row_spec, row_spec,
                   state_spec(WIDTH), state_spec(1), state_spec(CONV_WIDTH - 1), state_spec(WIDTH),
                   state_spec(POOL_BUF)],
        out_shape=[jax.ShapeDtypeStruct((n, WIDTH), BF16)] * 4 + [
            jax.ShapeDtypeStruct((batch, WIDTH, WIDTH), F32),
            jax.ShapeDtypeStruct((batch, 1, WIDTH), F32),
            jax.ShapeDtypeStruct((batch, CONV_WIDTH - 1, WIDTH), F32),
            jax.ShapeDtypeStruct((batch, WIDTH, WIDTH), F32),
            jax.ShapeDtypeStruct((batch, POOL_BUF, WIDTH), F32),
        ],
        scratch_shapes=[
            pltpu.VMEM((WIDTH, WIDTH), F32), pltpu.VMEM((WIDTH, WIDTH), F32), pltpu.VMEM((8, WIDTH), F32),
            pltpu.VMEM((8, WIDTH), F32), pltpu.VMEM((PAD + tc, WIDTH), F32),
            pltpu.VMEM((tc, WIDTH), F32), pltpu.VMEM((tc, WIDTH), F32), pltpu.VMEM((2, tc, 128), F32),
        ],
        compiler_params=_cparams("parallel", "arbitrary"),
        name="mixer_prompt",
    )(u, lb_logits, vecs, wr_bd, wi_bd, wp_bd, cos_t, sin_t, ind, dsum, pair, qdec, kdec, sdec, dmat)
    return outs


def _diag_blocks(s):
    return jnp.stack([s[:, h * HEAD_DIM:(h + 1) * HEAD_DIM, h * HEAD_DIM:(h + 1) * HEAD_DIM]
                      for h in range(N_HEADS)], axis=1)


def _sample_step_kernel(u_ref, lblt_ref, vec_ref, vect_ref, wr_ref, wi_ref, wp_ref, cost_ref, sint_ref,
                        sh_ref, sl_ref, scv_ref, sr_ref, spl_ref,
                        oa_ref, ob_ref, oc_ref, od_ref, nh_ref, nl_ref, ncv_ref, nr_ref, npl_ref,
                        fac_scr, ot_scr, *, layer, pos0):
    step = pl.program_id(0)
    F_, K_, Q_, V_, RQ_, RK_, RV_ = range(7)

    def slot_t(s):
        return u_ref[:, s * WIDTH:(s + 1) * WIDTH].T

    def vcol(r):
        return vect_ref[:, r:r + 1]

    @pl.when(step == 0)
    def _():
        lblt = lblt_ref[...]
        e = jnp.exp(lblt - jnp.max(lblt, axis=1, keepdims=True))
        soft = e / jnp.sum(e, axis=1, keepdims=True)
        acc = soft[:, 0:1]
        for l in range(1, layer + 1):
            acc = acc + soft[:, l:l + 1]
        lb = acc - soft[:, 0:1]
        sg = _sigmoid(slot_t(1))
        fac_scr[F_] = jnp.maximum(lb + (1.0 - lb) * sg, F_FLOOR)
        fac_scr[K_] = (1.0 - lb) * (1.0 - sg)
        fac_scr[Q_] = _silu(slot_t(0))
        fac_scr[V_] = slot_t(2)

        def swap_rows(x):
            half = HEAD_DIM // 2
            parts = []
            for h in range(N_HEADS):
                parts.append(x[h * HEAD_DIM + half:(h + 1) * HEAD_DIM])
                parts.append(x[h * HEAD_DIM:h * HEAD_DIM + half])
            return jnp.concatenate(parts, axis=0)

        cost, sint = cost_ref[...], sint_ref[...]
        cq, ck = slot_t(6), slot_t(7)
        fac_scr[RQ_] = cq * cost + swap_rows(cq) * sint
        fac_scr[RK_] = (ck * cost + swap_rows(ck) * sint) * (HEAD_DIM ** -0.5)
        fac_scr[RV_] = slot_t(8)

    @pl.when(step < N_HEADS)
    def _():
        base = pl.multiple_of(step * HEAD_DIM, HEAD_DIM)
        gam = [float(np.exp(g)) for g in _log_gamma()]
        gamma = jnp.float32(gam[-1])
        for h in range(N_HEADS - 2, -1, -1):
            gamma = jnp.where(step == h, jnp.float32(gam[h]), gamma)
        v_t = fac_scr[V_, pl.ds(base, HEAD_DIM), :]
        rv_t = fac_scr[RV_, pl.ds(base, HEAD_DIM), :]
        acc_a = jnp.zeros((HEAD_DIM, v_t.shape[1]), F32)
        acc_c = jnp.zeros((HEAD_DIM, v_t.shape[1]), F32)
        for k in range(HEAD_DIM):
            def row(which):
                return fac_scr[which, pl.ds(base + k, 1), :]
            s_new = row(F_) * sh_ref[k] + row(K_) * v_t
            nh_ref[k] = s_new
            acc_a = acc_a + row(Q_) * s_new
            r_new = gamma * sr_ref[k] + row(RK_) * rv_t
            nr_ref[k] = r_new
            acc_c = acc_c + row(RQ_) * r_new
        ot_scr[0, pl.ds(base, HEAD_DIM), :] = acc_a
        ot_scr[1, pl.ds(base, HEAD_DIM), :] = acc_c

    @pl.when(step == N_HEADS)
    def _():
        def vec(r):
            return vec_ref[r:r + 1, :]

        def slot(s):
            return u_ref[:, s * WIDTH:(s + 1) * WIDTH]

        def per_head(x, fn):
            return jnp.concatenate([fn(x[h * HEAD_DIM:(h + 1) * HEAD_DIM]) for h in range(N_HEADS)], axis=0)

        def rms_head(o):
            return o * lax.rsqrt(jnp.mean(o * o, axis=0, keepdims=True) + EPS)

        def group_head(o):
            dev = o - jnp.mean(o, axis=0, keepdims=True)
            return dev * lax.rsqrt(jnp.mean(dev * dev, axis=0, keepdims=True) + EPS)

        o_a = per_head(ot_scr[0], rms_head) * vcol(V_HNORM) * _silu(slot_t(3))
        oa_ref[...] = o_a.T
        o_c = per_head(ot_scr[1], group_head) * vcol(V_RNORM) * _silu(slot_t(9))
        oc_ref[...] = o_c.T

        ux = slot(4)
        xc = vec(V_CB) + vec(V_CW0 + CONV_WIDTH - 1) * ux
        for j in range(CONV_WIDTH - 1):
            xc = xc + vec(V_CW0 + j) * scv_ref[j]
        xc_bf = xc.astype(BF16)
        rg = _sigmoid(_dot(xc_bf, wr_ref[...]) + vec(V_BR))
        ig = _sigmoid(_dot(xc_bf, wi_ref[...]) + vec(V_BI))
        log_a = -LRU_C * rg * _softplus(-vec(V_LAM))
        b_in = jnp.sqrt(jnp.maximum(_neg_expm1(2.0 * log_a), 0.0)) * (ig * xc)
        hnew = jnp.exp(log_a) * sl_ref[...] + b_in
        nl_ref[...] = hnew
        ob_ref[...] = hnew * jax.nn.gelu(slot(5))
        for j in range(CONV_WIDTH - 2):
            ncv_ref[j] = scv_ref[j + 1]
        ncv_ref[CONV_WIDTH - 2] = ux

        ud = slot(10)
        wsum = ud
        sums = []
        nxt = 1
        for win in POOL_WINDOWS:
            while nxt < win:
                wsum = wsum + spl_ref[POOL_BUF - nxt]
                nxt += 1
            sums.append(wsum)
        lane = lax.broadcasted_iota(jnp.int32, (1, WIDTH), 1)
        grp = lane // (WIDTH // len(POOL_WINDOWS))
        sel = sums[-1]
        winl = jnp.full((1, WIDTH), float(POOL_WINDOWS[-1]), F32)
        for gi in range(len(POOL_WINDOWS) - 2, -1, -1):
            sel = jnp.where(grp == gi, sums[gi], sel)
            winl = jnp.where(grp == gi, float(POOL_WINDOWS[gi]), winl)
        pooled = sel / jnp.minimum(winl, float(pos0) + 1.0)
        od_ref[...] = _dot((pooled - ud).astype(BF16), wp_ref[...]) * vec(V_PSCALE)
        for j in range(POOL_BUF - 1):
            npl_ref[j] = spl_ref[j + 1]
        npl_ref[POOL_BUF - 1] = ud


def _sample_step(u, lb_logits, vecs, vecs_t, wr_bd, wi_bd, wp_bd, cos_t, sin_t, s_hgrn, s_lru, s_conv,
                 s_ret, s_pool, layer, *, pos0):
    nb = u.shape[0]
    depth = lb_logits.shape[0]
    cost = jnp.broadcast_to(cos_t.reshape(WIDTH, 1), (WIDTH, nb))
    sint = jnp.broadcast_to(sin_t.reshape(WIDTH, 1), (WIDTH, nb))

    def const(shape):
        return pl.BlockSpec(shape, lambda i: tuple(0 for _ in shape))

    def layer_block(shape):
        return pl.BlockSpec((None,) + shape, lambda i: (layer,) + tuple(0 for _ in shape))

    def head(i):
        return jnp.minimum(i, N_HEADS - 1)

    mat = (HEAD_DIM, HEAD_DIM, nb)
    mat_in = pl.BlockSpec((None, None) + mat, lambda i: (layer, head(i), 0, 0, 0))
    mat_out = pl.BlockSpec((None,) + mat, lambda i: (head(i), 0, 0, 0))
    rows = const((nb, WIDTH))
    return pl.pallas_call(
        functools.partial(_sample_step_kernel, layer=layer, pos0=pos0),
        grid=(N_HEADS + 1,),
        in_specs=[
            const((nb, MIX_COLS)), const((WIDTH, depth)),
            layer_block((N_VEC_ROWS, WIDTH)), layer_block((WIDTH, N_VEC_ROWS)),
            layer_block((WIDTH, WIDTH)), layer_block((WIDTH, WIDTH)), layer_block((WIDTH, WIDTH)),
            const((WIDTH, nb)), const((WIDTH, nb)),
            mat_in, layer_block((nb, WIDTH)), layer_block((CONV_WIDTH - 1, nb, WIDTH)), mat_in,
            layer_block((POOL_BUF, nb, WIDTH)),
        ],
        out_specs=[rows, rows, rows, rows, mat_out, rows, const((CONV_WIDTH - 1, nb, WIDTH)), mat_out,
                   const((POOL_BUF, nb, WIDTH))],
        out_shape=[jax.ShapeDtypeStruct((nb, WIDTH), F32)] * 4 + [
            jax.ShapeDtypeStruct((N_HEADS,) + mat, F32), jax.ShapeDtypeStruct((nb, WIDTH), F32),
            jax.ShapeDtypeStruct((CONV_WIDTH - 1, nb, WIDTH), F32), jax.ShapeDtypeStruct((N_HEADS,) + mat, F32),
            jax.ShapeDtypeStruct((POOL_BUF, nb, WIDTH), F32),
        ],
        scratch_shapes=[pltpu.VMEM((7, WIDTH, nb), F32), pltpu.VMEM((2, WIDTH, nb), F32)],
        compiler_params=_cparams("arbitrary"),
        name="sample_step",
    )(u, lb_logits.T, vecs, vecs_t, wr_bd, wi_bd, wp_bd, cost, sint, s_hgrn, s_lru, s_conv, s_ret, s_pool)


def _pick_tile(n, pref):
    t = min(n, pref)
    while n % t:
        t //= 2
    return t


def kernel(x_prompt, x_sample, state_hgrn, state_rglru, state_conv, state_retention, state_pool, lb_logits, ffn1_norm, ffn1_up, ffn1_down, mix_norm, w_in, hgrn_norm, conv_w, conv_b, w_rgate, b_rgate, w_igate, b_igate, lru_lambda, ret_norm, w_pool, pool_scale, w_branch, w_o, ffn2_norm, ffn2_up, ffn2_down, final_norm):
    batch, seq, d = x_prompt.shape
    nb, dec_seq, _ = x_sample.shape
    assert dec_seq == 1
    depth = w_in.shape[0]
    past_len = PAST_LEN

    up1, dn1 = ffn1_up.astype(BF16), ffn1_down.astype(BF16)
    up2, dn2 = ffn2_up.astype(BF16), ffn2_down.astype(BF16)
    w_in_bf = w_in.astype(BF16)
    w_br, w_out = w_branch.astype(BF16), w_o.astype(BF16)
    wr_bd = jax.vmap(_block_diag)(w_rgate).astype(BF16)
    wi_bd = jax.vmap(_block_diag)(w_igate).astype(BF16)
    wp_bd = jax.vmap(_block_diag)(w_pool).astype(BF16)
    vecs = jnp.concatenate(
        [hgrn_norm[:, None], conv_w, conv_b[:, None], b_rgate[:, None], b_igate[:, None],
         lru_lambda[:, None], ret_norm[:, None], pool_scale[:, None],
         jnp.zeros((depth, N_VEC_ROWS - 11, WIDTH), F32)], axis=1)
    vecs_t = jnp.swapaxes(vecs, 1, 2)
    hgrn_t = jnp.transpose(state_hgrn, (0, 2, 3, 4, 1))
    ret_t = jnp.transpose(state_retention, (0, 2, 3, 4, 1))
    conv_t = jnp.transpose(state_conv, (0, 2, 1, 3))
    pool_t = jnp.transpose(state_pool, (0, 2, 1, 3))
    n1 = ffn1_norm[:, None, :]
    n2 = ffn2_norm[:, None, :]
    nm = mix_norm[:, None, :]
    fg = final_norm[None, :]

    cos_p, sin_p = _rope_tables(jnp.arange(seq, dtype=F32))
    cos_s, sin_s = _rope_tables(past_len + jnp.arange(dec_seq, dtype=F32))

    xp = x_prompt.reshape(batch * seq, d)
    xs = x_sample.reshape(nb * dec_seq, d)
    tm_p = _pick_tile(batch * seq, DENSE_ROWS)
    tc = _pick_tile(seq, MIXER_ROWS)

    st_p = [[] for _ in range(5)]
    st_s = [[] for _ in range(5)]
    for l in range(depth):
        last = l == depth - 1
        xp, xs = _ffn(xp, xs, n1, up1, dn1, l, fg, final_norm=False, tm=tm_p)
        u_p, u_s = _inproj(xp, xs, nm, w_in_bf, l, tm=tm_p)
        *br_p, sh, sl, scv, sr, spl = _mixer_prompt(
            u_p, lb_logits, vecs, wr_bd, wi_bd, wp_bd, cos_p, sin_p, l, batch=batch, seq=seq, tc=tc, pos0=0)
        *br_s, nh, nl, ncv, nr, npl = _sample_step(
            u_s, lb_logits, vecs, vecs_t, wr_bd, wi_bd, wp_bd, cos_s, sin_s, hgrn_t, state_rglru,
            conv_t, ret_t, pool_t, l, pos0=past_len)
        xp, xs = _merge(xp, xs, nm, br_p, br_s, w_in_bf, w_br, w_out, l, tm=tm_p)
        xp, xs = _ffn(xp, xs, n2, up2, dn2, l, fg, final_norm=last, tm=tm_p)
        for lst, s in zip(st_p, (jnp.swapaxes(_diag_blocks(sh), 2, 3), sl[:, 0], scv, _diag_blocks(sr), spl)):
            lst.append(s)
        for lst, s in zip(st_s, (nh, nl, ncv, nr, npl)):
            lst.append(s)

    y_p = xp.reshape(batch, seq, d)
    y_s = xs.reshape(nb, dec_seq, d)
    nh, nl, ncv, nr, npl = (jnp.stack(s) for s in st_s)
    sample_states = (jnp.transpose(nh, (0, 4, 1, 2, 3)), nl, jnp.transpose(ncv, (0, 2, 1, 3)),
                     jnp.transpose(nr, (0, 4, 1, 2, 3)), jnp.transpose(npl, (0, 2, 1, 3)))
    return (y_p, y_s) + tuple(jnp.stack(s) for s in st_p) + sample_states
```

```python
import functools

import numpy as np
import jax
import jax.numpy as jnp
from jax import lax
from jax.experimental import pallas as pl
from jax.experimental.pallas import tpu as pltpu

F32 = jnp.float32
BF16 = jnp.bfloat16

N_HEADS = 4
HEAD_DIM = 64
WIDTH = N_HEADS * HEAD_DIM
N_MIX_SLOTS = 11
MIX_COLS = N_MIX_SLOTS * WIDTH
CHUNK = 64
N_LEVELS = 6
PAD = 16
CONV_WIDTH = 4
LRU_C = 8.0
POOL_WINDOWS = (2, 4, 8, 16)
POOL_BUF = 15
ROPE_BASE = 10000.0
EPS = 1e-6
F_FLOOR = 1e-30
PAST_LEN = 16384
VMEM_LIMIT = 56 * 1024 * 1024
DENSE_ROWS = 512
MIXER_ROWS = 512


def _cparams(*sem):
    return pltpu.CompilerParams(dimension_semantics=sem, vmem_limit_bytes=VMEM_LIMIT)


def _dot(a, b):
    return jnp.dot(a, b, preferred_element_type=F32)


def _dot_nt(a, b):
    return lax.dot_general(a, b, (((1,), (1,)), ((), ())), preferred_element_type=F32)


def _dot_tn(a, b):
    return lax.dot_general(a, b, (((0,), (0,)), ((), ())), preferred_element_type=F32)


def _rms(x, g):
    return x * lax.rsqrt(jnp.mean(x * x, axis=-1, keepdims=True) + EPS) * g


def _sigmoid(x):
    return 0.5 * jnp.tanh(0.5 * x) + 0.5


def _silu(x):
    hx = 0.5 * x
    return hx * jnp.tanh(hx) + hx


def _softplus(x):
    return jnp.maximum(x, 0.0) + jnp.log1p(jnp.exp(-jnp.abs(x)))


def _neg_expm1(y):
    t = jnp.tanh(0.5 * y)
    return -2.0 * t / (1.0 - t)


def _split2(x):
    hi = x.astype(BF16)
    lo = (x - hi.astype(F32)).astype(BF16)
    return hi, lo


def _split3(x):
    hi = x.astype(BF16)
    r = x - hi.astype(F32)
    mid = r.astype(BF16)
    lo = (r - mid.astype(F32)).astype(BF16)
    return hi, mid, lo


def _lower_bound(lbl, layer):
    m = jnp.max(lbl, axis=0, keepdims=True)
    e = jnp.exp(lbl - m)
    soft = e / jnp.sum(e, axis=0, keepdims=True)
    acc = soft[0:1]
    for l in range(1, layer + 1):
        acc = acc + soft[l:l + 1]
    return acc - soft[0:1]


def _resident(shape, index_map):
    return pl.BlockSpec(shape, index_map, pipeline_mode=pl.Buffered(1))


def _two_group_kernel(*refs, body, n_rows, n_tiles):
    p_refs, s_refs = refs[:n_rows], refs[n_rows:2 * n_rows]
    params, (op_ref, os_ref) = refs[2 * n_rows:-2], refs[-2:]
    i = pl.program_id(0)

    @pl.when(i < n_tiles)
    def _():
        op_ref[...] = body([r[...] for r in p_refs], params)

    @pl.when(i == n_tiles)
    def _():
        os_ref[...] = body([r[...] for r in s_refs], params)


def _two_group_call(body, rows_p, rows_s, params, param_specs, out_cols, tm, name):
    n_p, n_s = rows_p[0].shape[0], rows_s[0].shape[0]
    n_tiles = n_p // tm

    def tile(i):
        return (jnp.minimum(i, n_tiles - 1), 0)

    return pl.pallas_call(
        functools.partial(_two_group_kernel, body=body, n_rows=len(rows_p), n_tiles=n_tiles),
        grid=(n_tiles + 1,),
        in_specs=[pl.BlockSpec((tm, a.shape[1]), tile) for a in rows_p]
        + [pl.BlockSpec(a.shape, lambda i: (0, 0)) for a in rows_s] + list(param_specs),
        out_specs=[pl.BlockSpec((tm, out_cols), tile), pl.BlockSpec((n_s, out_cols), lambda i: (0, 0))],
        out_shape=[jax.ShapeDtypeStruct((n_p, out_cols), F32), jax.ShapeDtypeStruct((n_s, out_cols), F32)],
        compiler_params=_cparams("arbitrary"),
        name=name,
    )(*rows_p, *rows_s, *params)


def _ffn_body(rows, params, *, final_norm):
    (x,), (g_ref, wu_ref, wd_ref, fg_ref) = rows, params
    dff = wd_ref.shape[0]
    h = _rms(x, g_ref[...]).astype(BF16)
    a = _dot(h, wu_ref[:, :dff])
    b = _dot(h, wu_ref[:, dff:])
    act = (_silu(a) * b).astype(BF16)
    y = x + 0.5 * _dot(act, wd_ref[...])
    return _rms(y, fg_ref[...]) if final_norm else y


def _ffn(xp, xs, g, w_up, w_down, layer, final_g, *, final_norm, tm):
    d = xp.shape[1]
    dff = w_down.shape[1]
    specs = [
        _resident((None, 1, d), lambda i: (layer, 0, 0)),
        _resident((None, d, 2 * dff), lambda i: (layer, 0, 0)),
        _resident((None, dff, d), lambda i: (layer, 0, 0)),
        _resident((1, d), lambda i: (0, 0)),
    ]
    return _two_group_call(functools.partial(_ffn_body, final_norm=final_norm), [xp], [xs],
                           (g, w_up, w_down, final_g), specs, d, tm, "ffn")


def _inproj_body(rows, params):
    (x,), (g_ref, w_ref) = rows, params
    return _dot(_rms(x, g_ref[...]).astype(BF16), w_ref[...])


def _inproj(xp, xs, g, w_in, layer, *, tm):
    d = xp.shape[1]
    specs = [
        _resident((None, 1, d), lambda i: (layer, 0, 0)),
        _resident((None, d, MIX_COLS), lambda i: (layer, 0, 0)),
    ]
    return _two_group_call(_inproj_body, [xp], [xs], (g, w_in), specs, MIX_COLS, tm, "inproj")


def _merge_body(rows, params):
    x, branches = rows[0], rows[1:]
    g_ref, win_ref, wb_ref, wo_ref = params
    d = x.shape[1]
    h = _rms(x, g_ref[...]).astype(BF16)
    merged = None
    for b, o_b in enumerate(branches):
        gate = _sigmoid(_dot(h, win_ref[:, MIX_COLS + b * d:MIX_COLS + (b + 1) * d]))
        y = _dot(o_b.astype(BF16), wb_ref[b])
        merged = gate * y if merged is None else merged + gate * y
    return x + _dot(merged.astype(BF16), wo_ref[...])


def _merge(xp, xs, g, branches_p, branches_s, w_in, w_branch, w_o, layer, *, tm):
    d = xp.shape[1]
    w = branches_p[0].shape[1]
    nb = len(branches_p)
    assert w_in.shape[2] == MIX_COLS + nb * d and MIX_COLS % 128 == 0
    specs = [
        _resident((None, 1, d), lambda i: (layer, 0, 0)),
        _resident((None, d, w_in.shape[2]), lambda i: (layer, 0, 0)),
        _resident((None, nb, w, d), lambda i: (layer, 0, 0, 0)),
        _resident((None, d, d), lambda i: (layer, 0, 0)),
    ]
    return _two_group_call(_merge_body, [xp, *branches_p], [xs, *branches_s],
                           (g, w_in, w_branch, w_o), specs, d, tm, "merge")


def _head_block_ones():
    h = np.arange(WIDTH) // HEAD_DIM
    return (h[:, None] == h[None, :]).astype(np.float32)


def _log_gamma():
    return np.log1p(-(2.0 ** (-5.0 - np.arange(N_HEADS, dtype=np.float64))))


def _retention_tables():
    lg = np.repeat(_log_gamma(), HEAD_DIM)[None, :]
    t = np.arange(CHUNK, dtype=np.float64)[:, None]
    q_dec = np.exp((t + 1.0) * lg)
    k_dec = np.exp((CHUNK - 1.0 - t) * lg)
    s_dec = np.exp(CHUNK * lg)
    dt = t.T - t
    dmat = np.concatenate(
        [np.where(dt >= 0, np.exp(dt * g), 0.0) for g in _log_gamma()], axis=1)
    return (jnp.asarray(q_dec, F32), jnp.asarray(k_dec, F32), jnp.asarray(s_dec, F32),
            jnp.asarray(dmat, F32))


def _level_tables():
    t = np.arange(CHUNK)[:, None]
    r = np.arange(CHUNK)[None, :]
    blocks = [r <= t, r > t]
    pairs = []
    for i in range(N_LEVELS):
        m = 1 << i
        mid = t - (t % (2 * m)) + m
        up = (t % (2 * m)) >= m
        blocks.append(np.where(up, (r >= mid) & (r <= t), (r > t) & (r <= mid - 1)))
        same = (t // (2 * m)) == (r // (2 * m))
        pr = same & ((t % (2 * m)) < m) & ((r % (2 * m)) >= m)
        pairs.append(np.tile(pr, (1, N_HEADS)))
    dsum = np.tile(np.concatenate(blocks, axis=0), (1, 3))
    return jnp.asarray(dsum, BF16), jnp.asarray(np.stack(pairs), F32)


def _rope_tables(pos):
    half = HEAD_DIM // 2
    freq = ROPE_BASE ** (-jnp.arange(half, dtype=F32) / half)
    ang = pos[:, None] * freq[None, :]
    cos, sin = jnp.cos(ang), jnp.sin(ang)
    cos_h = jnp.concatenate([cos, cos], axis=-1)
    sin_h = jnp.concatenate([-sin, sin], axis=-1)
    return jnp.tile(cos_h, (1, N_HEADS)), jnp.tile(sin_h, (1, N_HEADS))


def _block_diag(w):
    h, dh, _ = w.shape
    eye = jnp.eye(h, dtype=w.dtype)
    return (eye[:, None, :, None] * w[:, :, None, :]).reshape(h * dh, h * dh)


V_HNORM, V_CW0, V_CB, V_BR, V_BI, V_LAM, V_RNORM, V_PSCALE = 0, 1, 5, 6, 7, 8, 9, 10
N_VEC_ROWS = 16


def _swap_halves(x):
    half = HEAD_DIM // 2
    lane = lax.broadcasted_iota(jnp.int32, (1, 128), 1)
    first = (lane % HEAD_DIM) < half
    parts = []
    for c in range(x.shape[1] // 128):
        xc = x[:, c * 128:(c + 1) * 128]
        parts.append(jnp.where(first, pltpu.roll(xc, 128 - half, 1), pltpu.roll(xc, half, 1)))
    return jnp.concatenate(parts, axis=1)


def _mixer_prompt_kernel(u_ref, lbl_ref, vec_ref, wr_ref, wi_ref, wp_ref, cos_ref, sin_ref,
                         ind_ref, dsum_ref, pair_ref, qdec_ref, kdec_ref, sdec_ref, dmat_ref,
                         oa_ref, ob_ref, oc_ref, od_ref, sh_ref, sl_ref, scv_ref, sr_ref, spl_ref,
                         hst_scr, rst_scr, hl_scr, xcv_scr, xpl_scr, oh_scr, orr_scr, perm_scr,
                         *, layer, tc, pos0):
    t_idx = pl.program_id(1)
    nchunk = tc // CHUNK

    @pl.when(t_idx == 0)
    def _():
        hst_scr[...] = jnp.zeros_like(hst_scr)
        rst_scr[...] = jnp.zeros_like(rst_scr)
        hl_scr[...] = jnp.zeros_like(hl_scr)
        xcv_scr[...] = jnp.zeros_like(xcv_scr)
        xpl_scr[0:PAD, :] = jnp.zeros((PAD, WIDTH), F32)

    def slot(s):
        return u_ref[:, s * WIDTH:(s + 1) * WIDTH]

    def vec(r):
        return vec_ref[r:r + 1, :]

    ind = ind_ref[...]
    lane = lax.broadcasted_iota(jnp.int32, (1, WIDTH), 1)
    hm_bf = [(lane // HEAD_DIM == h).astype(F32).astype(BF16) for h in range(N_HEADS)]
    lane_t = lax.broadcasted_iota(jnp.int32, (1, 128), 1)
    row = lax.broadcasted_iota(jnp.int32, (tc, WIDTH), 0)
    chunks = [slice(n * CHUNK, (n + 1) * CHUNK) for n in range(nchunk)]

    def head_scores(keys, queries):
        stacked = jnp.concatenate([queries * hm_bf[h] for h in range(N_HEADS)], axis=0)
        return _dot_nt(keys, stacked)

    def scores_times_values(sct, val):
        assert HEAD_DIM == 64 and WIDTH % 128 == 0
        tiles = []
        for c in range(WIDTH // 128):
            pair = _dot_tn(sct[:, 2 * c * CHUNK:(2 * c + 2) * CHUNK], val[:, c * 128:(c + 1) * 128])
            tiles.append(jnp.where(lane_t < HEAD_DIM, pair[0:CHUNK], pair[CHUNK:2 * CHUNK]))
        return jnp.concatenate(tiles, axis=1)

    n_lt = WIDTH // 128
    ind_t = ind[0:128, 0:128]

    def lane_tile(x, c):
        return x[:, c * 128:(c + 1) * 128]

    def tile_outer(a, b):
        return [_dot_tn(lane_tile(a, c), lane_tile(b, c)) for c in range(n_lt)]

    ind2 = jnp.concatenate([ind, ind], axis=0)

    def head_sum(x):
        return _dot(jnp.concatenate(_split2(x), axis=1), ind2)

    lb = _lower_bound(lbl_ref[...], layer)
    uq, z, v = slot(0), slot(1), slot(2)
    q = _silu(uq)
    sg = _sigmoid(z)
    f = lb + (1.0 - lb) * sg
    logf = jnp.log(jnp.maximum(f, F_FLOOR))
    k = (1.0 - lb) * (1.0 - sg)
    l_hi, l_mid, l_lo = _split3(logf)
    v_bf = v.astype(BF16)
    dsum = dsum_ref[...]
    rowc = lax.broadcasted_iota(jnp.int32, (CHUNK, WIDTH), 0)
    upper = [((rowc >> i) & 1) == 1 for i in range(N_LEVELS)]
    dec = [jnp.exp(jnp.minimum(_dot(dsum, jnp.concatenate([l_hi[rs], l_mid[rs], l_lo[rs]], axis=0)), 0.0))
           for rs in chunks]
    sct = [None] * nchunk
    for i in range(N_LEVELS):
        for n, rs in enumerate(chunks):
            fac = dec[n][(2 + i) * CHUNK:(3 + i) * CHUNK]
            zz = (jnp.where(upper[i], q[rs], k[rs]) * fac).astype(BF16)
            term = head_scores(zz, zz) * pair_ref[i]
            sct[n] = term if sct[n] is None else sct[n] + term
    o_diag = _dot((q * k).astype(BF16), ind) * v
    upd = [tile_outer(v_bf[rs], (k[rs] * dec[n][CHUNK:2 * CHUNK]).astype(BF16)) for n, rs in enumerate(chunks)]
    o_intra = [scores_times_values(sct[n].astype(BF16), v_bf[rs]) for n, rs in enumerate(chunks)]

    seg_len = tc // 8
    def strided_rows(ref, start, stride):
        return jnp.concatenate([ref[c, pl.ds(start, 8, stride=stride), :] for c in range(2)], axis=1)

    for c in range(2):
        perm_scr[c] = u_ref[:, 4 * WIDTH + c * 128:4 * WIDTH + (c + 1) * 128]
    slabs = [strided_rows(perm_scr, i, seg_len) for i in range(seg_len)]
    sub0 = lax.broadcasted_iota(jnp.int32, (8, WIDTH), 0) == 0
    front = [jnp.where(sub0, xcv_scr[8 + k:9 + k, :], pltpu.roll(slabs[seg_len + k], 1, 0))
             for k in range(-(CONV_WIDTH - 1), 0)]
    ext = front + slabs
    cw = [vec(V_CW0 + j) for j in range(CONV_WIDTH)]
    xc = jnp.concatenate(
        [vec(V_CB) + sum(cw[j] * ext[i + j] for j in range(CONV_WIDTH)) for i in range(seg_len)], axis=0)
    xc_bf = xc.astype(BF16)
    rg = _sigmoid(_dot(xc_bf, wr_ref[...]) + vec(V_BR))
    ig = _sigmoid(_dot(xc_bf, wi_ref[...]) + vec(V_BI))
    log_a = -LRU_C * rg * _softplus(-vec(V_LAM))
    a_all = jnp.exp(log_a)
    b_all = jnp.sqrt(jnp.maximum(_neg_expm1(2.0 * log_a), 0.0)) * (ig * xc)
    h_loc, a_loc = [b_all[0:8]], [a_all[0:8]]
    for i in range(1, seg_len):
        a_i = a_all[8 * i:8 * i + 8]
        h_loc.append(a_i * h_loc[-1] + b_all[8 * i:8 * i + 8])
        a_loc.append(a_i * a_loc[-1])
    carry = [hl_scr[0:1, :]]
    for s in range(8):
        carry.append(a_loc[-1][s:s + 1] * carry[-1] + h_loc[-1][s:s + 1])
    hl_scr[0:1, :] = carry[8]
    carry8 = jnp.concatenate(carry[0:8], axis=0)
    hperm = jnp.concatenate([h_loc[i] + a_loc[i] * carry8 for i in range(seg_len)], axis=0)
    for c in range(2):
        perm_scr[c] = hperm[:, c * 128:(c + 1) * 128]
    hseq = jnp.concatenate(
        [strided_rows(perm_scr, ((8 * r) % seg_len) * 8 + (8 * r) // seg_len, 8) for r in range(seg_len)],
        axis=0)
    ob_ref[...] = (hseq * jax.nn.gelu(slot(5))).astype(ob_ref.dtype)
    xcv_scr[...] = u_ref[tc - 8:tc, 4 * WIDTH:5 * WIDTH]

    cosv, sinv = cos_ref[...], sin_ref[...]
    cq, ck = slot(6), slot(7)
    qr = cq * cosv + _swap_halves(cq) * sinv
    kr = (ck * cosv + _swap_halves(ck) * sinv) * (HEAD_DIM ** -0.5)
    rv_bf = slot(8).astype(BF16)
    qdec, kdec, sdec, dmat = qdec_ref[...], kdec_ref[...], sdec_ref[...], dmat_ref[...]
    qr_bf, kr_bf = qr.astype(BF16), kr.astype(BF16)
    rsc = []
    for rs in chunks:
        rsc.append((head_scores(kr_bf[rs], qr_bf[rs]) * dmat).astype(BF16))
    r_intra = [scores_times_values(rsc[n], rv_bf[rs]) for n, rs in enumerate(chunks)]
    rupd = [tile_outer((kr[rs] * kdec).astype(BF16), rv_bf[rs]) for rs in chunks]

    ud = slot(10)
    xpl_scr[PAD:PAD + tc, :] = ud
    assert POOL_WINDOWS == (2, 4, 8, 16) and WIDTH // len(POOL_WINDOWS) == 64 and PAD >= POOL_BUF
    halves = []
    for c in range(2):
        acc = xpl_scr[:, c * 128:(c + 1) * 128]
        built = {}
        for sh in (1, 2, 4, 8):
            acc = acc + pltpu.roll(acc, sh, 0)
            built[2 * sh] = acc
        lo, hi = POOL_WINDOWS[2 * c], POOL_WINDOWS[2 * c + 1]
        halves.append(jnp.where(lane_t < 64, built[lo], built[hi])[PAD:PAD + tc])
    sel = jnp.concatenate(halves, axis=1)
    grp = lane // (WIDTH // len(POOL_WINDOWS))
    winl = jnp.full((1, WIDTH), float(POOL_WINDOWS[-1]), F32)
    for gi in range(len(POOL_WINDOWS) - 2, -1, -1):
        winl = jnp.where(grp == gi, float(POOL_WINDOWS[gi]), winl)
    pos = (row + t_idx * tc).astype(F32) + float(pos0)
    pooled = sel / jnp.minimum(winl, pos + 1.0)
    od = _dot((pooled - ud).astype(BF16), wp_ref[...]) * vec(V_PSCALE)
    od_ref[...] = od.astype(od_ref.dtype)
    xpl_scr[0:PAD, :] = xpl_scr[tc:tc + PAD, :]

    st = [hst_scr[c] for c in range(n_lt)]
    s_ret = [rst_scr[c] for c in range(n_lt)]
    for n, rs in enumerate(chunks):
        qe = (q[rs] * dec[n][0:CHUNK]).astype(BF16)
        o_c = jnp.concatenate(
            [_dot_nt(lane_tile(qe, c), st[c].astype(BF16) * ind_t) for c in range(n_lt)], axis=1)
        st = [st[c] * lane_tile(dec[n][CHUNK - 1:CHUNK], c) + upd[n][c] for c in range(n_lt)]
        oh_scr[rs, :] = o_c + o_diag[rs] + o_intra[n]
        qe = (qr[rs] * qdec).astype(BF16)
        o_c = jnp.concatenate(
            [_dot(lane_tile(qe, c), s_ret[c].astype(BF16) * ind_t) for c in range(n_lt)], axis=1)
        s_ret = [s_ret[c] * lane_tile(sdec, c) + rupd[n][c] for c in range(n_lt)]
        orr_scr[rs, :] = o_c + r_intra[n]
    for c in range(n_lt):
        hst_scr[c] = st[c]
        rst_scr[c] = s_ret[c]
    o_h = oh_scr[...]
    ms = head_sum(o_h * o_h) * (1.0 / HEAD_DIM)
    oa_ref[...] = (o_h * lax.rsqrt(ms + EPS) * vec(V_HNORM) * _silu(slot(3))).astype(oa_ref.dtype)
    o_r = orr_scr[...]
    mu = head_sum(o_r) * (1.0 / HEAD_DIM)
    dev = o_r - mu
    var = head_sum(dev * dev) * (1.0 / HEAD_DIM)
    oc_ref[...] = (dev * lax.rsqrt(var + EPS) * vec(V_RNORM) * _silu(slot(9))).astype(oc_ref.dtype)

    @pl.when(t_idx == pl.num_programs(1) - 1)
    def _():
        sh_ref[...] = hst_scr[...]
        sr_ref[...] = rst_scr[...]
        sl_ref[...] = hl_scr[0:1, :]
        scv_ref[...] = xcv_scr[8 - (CONV_WIDTH - 1):8, :]
        spl_ref[...] = xpl_scr[PAD + tc - POOL_BUF:PAD + tc, :]


def _mixer_prompt(u, lb_logits, vecs, wr_bd, wi_bd, wp_bd, cos_t, sin_t, layer, *, batch, seq, tc, pos0):
    nt = seq // tc
    ind = jnp.asarray(_head_block_ones(), BF16)
    dsum, pair = _level_tables()
    qdec, kdec, sdec, dmat = _retention_tables()
    depth = lb_logits.shape[0]

    def const(shape):
        return pl.BlockSpec(shape, lambda b, t: tuple(0 for _ in shape))

    row_spec = pl.BlockSpec((tc, WIDTH), lambda b, t: (b * nt + t, 0))
    tab_spec = pl.BlockSpec((tc, WIDTH), lambda b, t: (t, 0))

    def state_spec(rows):
        return pl.BlockSpec((None, rows, WIDTH), lambda b, t: (b, 0, 0))

    mat_tiles = (WIDTH // 128, 128, 128)
    mat_spec = pl.BlockSpec((None,) + mat_tiles, lambda b, t: (b, 0, 0, 0))
    n = batch * seq
    outs = pl.pallas_call(
        functools.partial(_mixer_prompt_kernel, layer=layer, tc=tc, pos0=pos0),
        grid=(batch, nt),
        in_specs=[
            pl.BlockSpec((tc, MIX_COLS), lambda b, t: (b * nt + t, 0)),
            const((depth, WIDTH)),
            pl.BlockSpec((None, N_VEC_ROWS, WIDTH), lambda b, t: (layer, 0, 0)),
            pl.BlockSpec((None, WIDTH, WIDTH), lambda b, t: (layer, 0, 0)),
            pl.BlockSpec((None, WIDTH, WIDTH), lambda b, t: (layer, 0, 0)),
            pl.BlockSpec((None, WIDTH, WIDTH), lambda b, t: (layer, 0, 0)),
            tab_spec, tab_spec,
            const((WIDTH, WIDTH)), const(dsum.shape), const(pair.shape),
            const((CHUNK, WIDTH)), const((CHUNK, WIDTH)), const((1, WIDTH)),
            const((CHUNK, N_HEADS * CHUNK)),
        ],
        out_specs=[row_spec, row_spec, row_spec, row_spec,
                   mat_spec, state_spec(1), state_spec(CONV_WIDTH - 1), mat_spec,
                   state_spec(POOL_BUF)],
        out_shape=[jax.ShapeDtypeStruct((n, WIDTH), BF16)] * 4 + [
            jax.ShapeDtypeStruct((batch,) + mat_tiles, F32),
            jax.ShapeDtypeStruct((batch, 1, WIDTH), F32),
            jax.ShapeDtypeStruct((batch, CONV_WIDTH - 1, WIDTH), F32),
            jax.ShapeDtypeStruct((batch,) + mat_tiles, F32),
            jax.ShapeDtypeStruct((batch, POOL_BUF, WIDTH), F32),
        ],
        scratch_shapes=[
            pltpu.VMEM(mat_tiles, F32), pltpu.VMEM(mat_tiles, F32), pltpu.VMEM((8, WIDTH), F32),
            pltpu.VMEM((8, WIDTH), F32), pltpu.VMEM((PAD + tc, WIDTH), F32),
            pltpu.VMEM((tc, WIDTH), F32), pltpu.VMEM((tc, WIDTH), F32), pltpu.VMEM((2, tc, 128), F32),
        ],
        compiler_params=_cparams("parallel", "arbitrary"),
        name="mixer_prompt",
    )(u, lb_logits, vecs, wr_bd, wi_bd, wp_bd, cos_t, sin_t, ind, dsum, pair, qdec, kdec, sdec, dmat)
    return outs


def _diag_blocks(s):
    per_tile = 128 // HEAD_DIM
    return jnp.stack([s[:, h // per_tile,
                        (h % per_tile) * HEAD_DIM:(h % per_tile + 1) * HEAD_DIM,
                        (h % per_tile) * HEAD_DIM:(h % per_tile + 1) * HEAD_DIM]
                      for h in range(N_HEADS)], axis=1)


def _sample_step_kernel(u_ref, lblt_ref, vec_ref, vect_ref, wr_ref, wi_ref, wp_ref, cost_ref, sint_ref,
                        sh_ref, sl_ref, scv_ref, sr_ref, spl_ref,
                        oa_ref, ob_ref, oc_ref, od_ref, nh_ref, nl_ref, ncv_ref, nr_ref, npl_ref,
                        fac_scr, ot_scr, *, layer, pos0):
    step = pl.program_id(0)
    F_, K_, Q_, V_, RQ_, RK_, RV_ = range(7)

    def slot_t(s):
        return u_ref[:, s * WIDTH:(s + 1) * WIDTH].T

    def vcol(r):
        return vect_ref[:, r:r + 1]

    @pl.when(step == 0)
    def _():
        lblt = lblt_ref[...]
        e = jnp.exp(lblt - jnp.max(lblt, axis=1, keepdims=True))
        soft = e / jnp.sum(e, axis=1, keepdims=True)
        acc = soft[:, 0:1]
        for l in range(1, layer + 1):
            acc = acc + soft[:, l:l + 1]
        lb = acc - soft[:, 0:1]
        sg = _sigmoid(slot_t(1))
        fac_scr[F_] = jnp.maximum(lb + (1.0 - lb) * sg, F_FLOOR)
        fac_scr[K_] = (1.0 - lb) * (1.0 - sg)
        fac_scr[Q_] = _silu(slot_t(0))
        fac_scr[V_] = slot_t(2)

        def swap_rows(x):
            half = HEAD_DIM // 2
            parts = []
            for h in range(N_HEADS):
                parts.append(x[h * HEAD_DIM + half:(h + 1) * HEAD_DIM])
                parts.append(x[h * HEAD_DIM:h * HEAD_DIM + half])
            return jnp.concatenate(parts, axis=0)

        cost, sint = cost_ref[...], sint_ref[...]
        cq, ck = slot_t(6), slot_t(7)
        fac_scr[RQ_] = cq * cost + swap_rows(cq) * sint
        fac_scr[RK_] = (ck * cost + swap_rows(ck) * sint) * (HEAD_DIM ** -0.5)
        fac_scr[RV_] = slot_t(8)

    @pl.when(step < N_HEADS)
    def _():
        base = pl.multiple_of(step * HEAD_DIM, HEAD_DIM)
        gam = [float(np.exp(g)) for g in _log_gamma()]
        gamma = jnp.float32(gam[-1])
        for h in range(N_HEADS - 2, -1, -1):
            gamma = jnp.where(step == h, jnp.float32(gam[h]), gamma)
        v_t = fac_scr[V_, pl.ds(base, HEAD_DIM), :]
        rv_t = fac_scr[RV_, pl.ds(base, HEAD_DIM), :]
        acc_a = jnp.zeros((HEAD_DIM, v_t.shape[1]), F32)
        acc_c = jnp.zeros((HEAD_DIM, v_t.shape[1]), F32)
        for k in range(HEAD_DIM):
            def row(which):
                return fac_scr[which, pl.ds(base + k, 1), :]
            s_new = row(F_) * sh_ref[k] + row(K_) * v_t
            nh_ref[k] = s_new
            acc_a = acc_a + row(Q_) * s_new
            r_new = gamma * sr_ref[k] + row(RK_) * rv_t
            nr_ref[k] = r_new
            acc_c = acc_c + row(RQ_) * r_new
        ot_scr[0, pl.ds(base, HEAD_DIM), :] = acc_a
        ot_scr[1, pl.ds(base, HEAD_DIM), :] = acc_c

    @pl.when(step == N_HEADS)
    def _():
        def vec(r):
            return vec_ref[r:r + 1, :]

        def slot(s):
            return u_ref[:, s * WIDTH:(s + 1) * WIDTH]

        def per_head(x, fn):
            return jnp.concatenate([fn(x[h * HEAD_DIM:(h + 1) * HEAD_DIM]) for h in range(N_HEADS)], axis=0)

        def rms_head(o):
            return o * lax.rsqrt(jnp.mean(o * o, axis=0, keepdims=True) + EPS)

        def group_head(o):
            dev = o - jnp.mean(o, axis=0, keepdims=True)
            return dev * lax.rsqrt(jnp.mean(dev * dev, axis=0, keepdims=True) + EPS)

        o_a = per_head(ot_scr[0], rms_head) * vcol(V_HNORM) * _silu(slot_t(3))
        oa_ref[...] = o_a.T
        o_c = per_head(ot_scr[1], group_head) * vcol(V_RNORM) * _silu(slot_t(9))
        oc_ref[...] = o_c.T

        ux = slot(4)
        xc = vec(V_CB) + vec(V_CW0 + CONV_WIDTH - 1) * ux
        for j in range(CONV_WIDTH - 1):
            xc = xc + vec(V_CW0 + j) * scv_ref[j]
        xc_bf = xc.astype(BF16)
        rg = _sigmoid(_dot(xc_bf, wr_ref[...]) + vec(V_BR))
        ig = _sigmoid(_dot(xc_bf, wi_ref[...]) + vec(V_BI))
        log_a = -LRU_C * rg * _softplus(-vec(V_LAM))
        b_in = jnp.sqrt(jnp.maximum(_neg_expm1(2.0 * log_a), 0.0)) * (ig * xc)
        hnew = jnp.exp(log_a) * sl_ref[...] + b_in
        nl_ref[...] = hnew
        ob_ref[...] = hnew * jax.nn.gelu(slot(5))
        for j in range(CONV_WIDTH - 2):
            ncv_ref[j] = scv_ref[j + 1]
        ncv_ref[CONV_WIDTH - 2] = ux

        ud = slot(10)
        wsum = ud
        sums = []
        nxt = 1
        for win in POOL_WINDOWS:
            while nxt < win:
                wsum = wsum + spl_ref[POOL_BUF - nxt]
                nxt += 1
            sums.append(wsum)
        lane = lax.broadcasted_iota(jnp.int32, (1, WIDTH), 1)
        grp = lane // (WIDTH // len(POOL_WINDOWS))
        sel = sums[-1]
        winl = jnp.full((1, WIDTH), float(POOL_WINDOWS[-1]), F32)
        for gi in range(len(POOL_WINDOWS) - 2, -1, -1):
            sel = jnp.where(grp == gi, sums[gi], sel)
            winl = jnp.where(grp == gi, float(POOL_WINDOWS[gi]), winl)
        pooled = sel / jnp.minimum(winl, float(pos0) + 1.0)
        od_ref[...] = _dot((pooled - ud).astype(BF16), wp_ref[...]) * vec(V_PSCALE)
        for j in range(POOL_BUF - 1):
            npl_ref[j] = spl_ref[j + 1]
        npl_ref[POOL_BUF - 1] = ud


def _sample_step(u, lb_logits, vecs, vecs_t, wr_bd, wi_bd, wp_bd, cos_t, sin_t, s_hgrn, s_lru, s_conv,
                 s_ret, s_pool, layer, *, pos0):
    nb = u.shape[0]
    depth = lb_logits.shape[0]
    cost = jnp.broadcast_to(cos_t.reshape(WIDTH, 1), (WIDTH, nb))
    sint = jnp.broadcast_to(sin_t.reshape(WIDTH, 1), (WIDTH, nb))

    def const(shape):
        return pl.BlockSpec(shape, lambda i: tuple(0 for _ in shape))

    def layer_block(shape):
        return pl.BlockSpec((None,) + shape, lambda i: (layer,) + tuple(0 for _ in shape))

    def head(i):
        return jnp.minimum(i, N_HEADS - 1)

    mat = (HEAD_DIM, HEAD_DIM, nb)
    mat_in = pl.BlockSpec((None, None) + mat, lambda i: (layer, head(i), 0, 0, 0))
    mat_out = pl.BlockSpec((None,) + mat, lambda i: (head(i), 0, 0, 0))
    rows = const((nb, WIDTH))
    return pl.pallas_call(
        functools.partial(_sample_step_kernel, layer=layer, pos0=pos0),
        grid=(N_HEADS + 1,),
        in_specs=[
            const((nb, MIX_COLS)), const((WIDTH, depth)),
            layer_block((N_VEC_ROWS, WIDTH)), layer_block((WIDTH, N_VEC_ROWS)),
            layer_block((WIDTH, WIDTH)), layer_block((WIDTH, WIDTH)), layer_block((WIDTH, WIDTH)),
            const((WIDTH, nb)), const((WIDTH, nb)),
            mat_in, layer_block((nb, WIDTH)), layer_block((CONV_WIDTH - 1, nb, WIDTH)), mat_in,
            layer_block((POOL_BUF, nb, WIDTH)),
        ],
        out_specs=[rows, rows, rows, rows, mat_out, rows, const((CONV_WIDTH - 1, nb, WIDTH)), mat_out,
                   const((POOL_BUF, nb, WIDTH))],
        out_shape=[jax.ShapeDtypeStruct((nb, WIDTH), F32)] * 4 + [
            jax.ShapeDtypeStruct((N_HEADS,) + mat, F32), jax.ShapeDtypeStruct((nb, WIDTH), F32),
            jax.ShapeDtypeStruct((CONV_WIDTH - 1, nb, WIDTH), F32), jax.ShapeDtypeStruct((N_HEADS,) + mat, F32),
            jax.ShapeDtypeStruct((POOL_BUF, nb, WIDTH), F32),
        ],
        scratch_shapes=[pltpu.VMEM((7, WIDTH, nb), F32), pltpu.VMEM((2, WIDTH, nb), F32)],
        compiler_params=_cparams("arbitrary"),
        name="sample_step",
    )(u, lb_logits.T, vecs, vecs_t, wr_bd, wi_bd, wp_bd, cost, sint, s_hgrn, s_lru, s_conv, s_ret, s_pool)


def _pick_tile(n, pref):
    t = min(n, pref)
    while n % t:
        t //= 2
    return t


def kernel(x_prompt, x_sample, state_hgrn, state_rglru, state_conv, state_retention, state_pool, lb_logits, ffn1_norm, ffn1_up, ffn1_down, mix_norm, w_in, hgrn_norm, conv_w, conv_b, w_rgate, b_rgate, w_igate, b_igate, lru_lambda, ret_norm, w_pool, pool_scale, w_branch, w_o, ffn2_norm, ffn2_up, ffn2_down, final_norm):
    batch, seq, d = x_prompt.shape
    nb, dec_seq, _ = x_sample.shape
    assert dec_seq == 1
    depth = w_in.shape[0]
    past_len = PAST_LEN

    up1, dn1 = ffn1_up.astype(BF16), ffn1_down.astype(BF16)
    up2, dn2 = ffn2_up.astype(BF16), ffn2_down.astype(BF16)
    w_in_bf = w_in.astype(BF16)
    w_br, w_out = w_branch.astype(BF16), w_o.astype(BF16)
    wr_bd = jax.vmap(_block_diag)(w_rgate).astype(BF16)
    wi_bd = jax.vmap(_block_diag)(w_igate).astype(BF16)
    wp_bd = jax.vmap(_block_diag)(w_pool).astype(BF16)
    vecs = jnp.concatenate(
        [hgrn_norm[:, None], conv_w, conv_b[:, None], b_rgate[:, None], b_igate[:, None],
         lru_lambda[:, None], ret_norm[:, None], pool_scale[:, None],
         jnp.zeros((depth, N_VEC_ROWS - 11, WIDTH), F32)], axis=1)
    vecs_t = jnp.swapaxes(vecs, 1, 2)
    hgrn_t = jnp.transpose(state_hgrn, (0, 2, 3, 4, 1))
    ret_t = jnp.transpose(state_retention, (0, 2, 3, 4, 1))
    conv_t = jnp.transpose(state_conv, (0, 2, 1, 3))
    pool_t = jnp.transpose(state_pool, (0, 2, 1, 3))
    n1 = ffn1_norm[:, None, :]
    n2 = ffn2_norm[:, None, :]
    nm = mix_norm[:, None, :]
    fg = final_norm[None, :]

    cos_p, sin_p = _rope_tables(jnp.arange(seq, dtype=F32))
    cos_s, sin_s = _rope_tables(past_len + jnp.arange(dec_seq, dtype=F32))

    xp = x_prompt.reshape(batch * seq, d)
    xs = x_sample.reshape(nb * dec_seq, d)
    tm_p = _pick_tile(batch * seq, DENSE_ROWS)
    tc = _pick_tile(seq, MIXER_ROWS)

    st_p = [[] for _ in range(5)]
    st_s = [[] for _ in range(5)]
    for l in range(depth):
        last = l == depth - 1
        xp, xs = _ffn(xp, xs, n1, up1, dn1, l, fg, final_norm=False, tm=tm_p)
        u_p, u_s = _inproj(xp, xs, nm, w_in_bf, l, tm=tm_p)
        *br_p, sh, sl, scv, sr, spl = _mixer_prompt(
            u_p, lb_logits, vecs, wr_bd, wi_bd, wp_bd, cos_p, sin_p, l, batch=batch, seq=seq, tc=tc, pos0=0)
        *br_s, nh, nl, ncv, nr, npl = _sample_step(
            u_s, lb_logits, vecs, vecs_t, wr_bd, wi_bd, wp_bd, cos_s, sin_s, hgrn_t, state_rglru,
            conv_t, ret_t, pool_t, l, pos0=past_len)
        xp, xs = _merge(xp, xs, nm, br_p, br_s, w_in_bf, w_br, w_out, l, tm=tm_p)
        xp, xs = _ffn(xp, xs, n2, up2, dn2, l, fg, final_norm=last, tm=tm_p)
        for lst, s in zip(st_p, (jnp.swapaxes(_diag_blocks(sh), 2, 3), sl[:, 0], scv, _diag_blocks(sr), spl)):
            lst.append(s)
        for lst, s in zip(st_s, (nh, nl, ncv, nr, npl)):
            lst.append(s)

    y_p = xp.reshape(batch, seq, d)
    y_s = xs.reshape(nb, dec_seq, d)
    nh, nl, ncv, nr, npl = (jnp.stack(s) for s in st_s)
    sample_states = (jnp.transpose(nh, (0, 4, 1, 2, 3)), nl, jnp.transpose(ncv, (0, 2, 1, 3)),
                     jnp.transpose(nr, (0, 4, 1, 2, 3)), jnp.transpose(npl, (0, 2, 1, 3)))
    return (y_p, y_s) + tuple(jnp.stack(s) for s in st_p) + sample_states
```

```python
import functools

import numpy as np
import jax
import jax.numpy as jnp
from jax import lax
from jax.experimental import pallas as pl
from jax.experimental.pallas import tpu as pltpu

F32 = jnp.float32
BF16 = jnp.bfloat16

N_HEADS = 4
HEAD_DIM = 64
WIDTH = N_HEADS * HEAD_DIM
N_MIX_SLOTS = 11
MIX_COLS = N_MIX_SLOTS * WIDTH
CHUNK = 64
N_LEVELS = 6
PAD = 16
CONV_WIDTH = 4
LRU_C = 8.0
POOL_WINDOWS = (2, 4, 8, 16)
POOL_BUF = 15
ROPE_BASE = 10000.0
EPS = 1e-6
F_FLOOR = 1e-30
SAFE_LOG = 80.0
PAST_LEN = 16384
VMEM_LIMIT = 56 * 1024 * 1024
DENSE_ROWS = 512
MIXER_ROWS = 512


def _cparams(*sem):
    return pltpu.CompilerParams(dimension_semantics=sem, vmem_limit_bytes=VMEM_LIMIT)


def _dot(a, b):
    return jnp.dot(a, b, preferred_element_type=F32)


def _dot_nt(a, b):
    return lax.dot_general(a, b, (((1,), (1,)), ((), ())), preferred_element_type=F32)


def _dot_tn(a, b):
    return lax.dot_general(a, b, (((0,), (0,)), ((), ())), preferred_element_type=F32)


def _rms(x, g):
    return x * lax.rsqrt(jnp.mean(x * x, axis=-1, keepdims=True) + EPS) * g


def _sigmoid(x):
    return 0.5 * jnp.tanh(0.5 * x) + 0.5


def _silu(x):
    hx = 0.5 * x
    return hx * jnp.tanh(hx) + hx


def _softplus(x):
    return jnp.maximum(x, 0.0) + jnp.log1p(jnp.exp(-jnp.abs(x)))


def _neg_expm1(y):
    t = jnp.tanh(0.5 * y)
    return -2.0 * t / (1.0 - t)


def _split2(x):
    hi = x.astype(BF16)
    lo = (x - hi.astype(F32)).astype(BF16)
    return hi, lo


def _split3(x):
    hi = x.astype(BF16)
    r = x - hi.astype(F32)
    mid = r.astype(BF16)
    lo = (r - mid.astype(F32)).astype(BF16)
    return hi, mid, lo


def _lower_bound(lbl, layer):
    m = jnp.max(lbl, axis=0, keepdims=True)
    e = jnp.exp(lbl - m)
    soft = e / jnp.sum(e, axis=0, keepdims=True)
    acc = soft[0:1]
    for l in range(1, layer + 1):
        acc = acc + soft[l:l + 1]
    return acc - soft[0:1]


def _resident(shape, index_map):
    return pl.BlockSpec(shape, index_map, pipeline_mode=pl.Buffered(1))


def _two_group_kernel(*refs, body, n_rows, n_tiles):
    p_refs, s_refs = refs[:n_rows], refs[n_rows:2 * n_rows]
    params, (op_ref, os_ref) = refs[2 * n_rows:-2], refs[-2:]
    i = pl.program_id(0)

    @pl.when(i < n_tiles)
    def _():
        op_ref[...] = body([r[...] for r in p_refs], params)

    @pl.when(i == n_tiles)
    def _():
        os_ref[...] = body([r[...] for r in s_refs], params)


def _two_group_call(body, rows_p, rows_s, params, param_specs, out_cols, tm, name):
    n_p, n_s = rows_p[0].shape[0], rows_s[0].shape[0]
    n_tiles = n_p // tm

    def tile(i):
        return (jnp.minimum(i, n_tiles - 1), 0)

    return pl.pallas_call(
        functools.partial(_two_group_kernel, body=body, n_rows=len(rows_p), n_tiles=n_tiles),
        grid=(n_tiles + 1,),
        in_specs=[pl.BlockSpec((tm, a.shape[1]), tile) for a in rows_p]
        + [pl.BlockSpec(a.shape, lambda i: (0, 0)) for a in rows_s] + list(param_specs),
        out_specs=[pl.BlockSpec((tm, out_cols), tile), pl.BlockSpec((n_s, out_cols), lambda i: (0, 0))],
        out_shape=[jax.ShapeDtypeStruct((n_p, out_cols), F32), jax.ShapeDtypeStruct((n_s, out_cols), F32)],
        compiler_params=_cparams("arbitrary"),
        name=name,
    )(*rows_p, *rows_s, *params)


def _ffn_body(rows, params, *, final_norm):
    (x,), (g_ref, wu_ref, wd_ref, fg_ref) = rows, params
    dff = wd_ref.shape[0]
    h = _rms(x, g_ref[...]).astype(BF16)
    a = _dot(h, wu_ref[:, :dff])
    b = _dot(h, wu_ref[:, dff:])
    act = (_silu(a) * b).astype(BF16)
    y = x + 0.5 * _dot(act, wd_ref[...])
    return _rms(y, fg_ref[...]) if final_norm else y


def _ffn(xp, xs, g, w_up, w_down, layer, final_g, *, final_norm, tm):
    d = xp.shape[1]
    dff = w_down.shape[1]
    specs = [
        _resident((None, 1, d), lambda i: (layer, 0, 0)),
        _resident((None, d, 2 * dff), lambda i: (layer, 0, 0)),
        _resident((None, dff, d), lambda i: (layer, 0, 0)),
        _resident((1, d), lambda i: (0, 0)),
    ]
    return _two_group_call(functools.partial(_ffn_body, final_norm=final_norm), [xp], [xs],
                           (g, w_up, w_down, final_g), specs, d, tm, "ffn")


def _inproj_body(rows, params):
    (x,), (g_ref, w_ref) = rows, params
    return _dot(_rms(x, g_ref[...]).astype(BF16), w_ref[...])


def _inproj(xp, xs, g, w_in, layer, *, tm):
    d = xp.shape[1]
    specs = [
        _resident((None, 1, d), lambda i: (layer, 0, 0)),
        _resident((None, d, MIX_COLS), lambda i: (layer, 0, 0)),
    ]
    return _two_group_call(_inproj_body, [xp], [xs], (g, w_in), specs, MIX_COLS, tm, "inproj")


def _merge_body(rows, params):
    x, branches = rows[0], rows[1:]
    g_ref, win_ref, wb_ref, wo_ref = params
    d = x.shape[1]
    h = _rms(x, g_ref[...]).astype(BF16)
    merged = None
    for b, o_b in enumerate(branches):
        gate = _sigmoid(_dot(h, win_ref[:, MIX_COLS + b * d:MIX_COLS + (b + 1) * d]))
        y = _dot(o_b.astype(BF16), wb_ref[b])
        merged = gate * y if merged is None else merged + gate * y
    return x + _dot(merged.astype(BF16), wo_ref[...])


def _merge(xp, xs, g, branches_p, branches_s, w_in, w_branch, w_o, layer, *, tm):
    d = xp.shape[1]
    w = branches_p[0].shape[1]
    nb = len(branches_p)
    assert w_in.shape[2] == MIX_COLS + nb * d and MIX_COLS % 128 == 0
    specs = [
        _resident((None, 1, d), lambda i: (layer, 0, 0)),
        _resident((None, d, w_in.shape[2]), lambda i: (layer, 0, 0)),
        _resident((None, nb, w, d), lambda i: (layer, 0, 0, 0)),
        _resident((None, d, d), lambda i: (layer, 0, 0)),
    ]
    return _two_group_call(_merge_body, [xp, *branches_p], [xs, *branches_s],
                           (g, w_in, w_branch, w_o), specs, d, tm, "merge")


def _head_block_ones():
    h = np.arange(WIDTH) // HEAD_DIM
    return (h[:, None] == h[None, :]).astype(np.float32)


def _log_gamma():
    return np.log1p(-(2.0 ** (-5.0 - np.arange(N_HEADS, dtype=np.float64))))


def _retention_tables():
    lg = np.repeat(_log_gamma(), HEAD_DIM)[None, :]
    t = np.arange(CHUNK, dtype=np.float64)[:, None]
    q_dec = np.exp((t + 1.0) * lg)
    k_dec = np.exp((CHUNK - 1.0 - t) * lg)
    s_dec = np.exp(CHUNK * lg)
    dt = t.T - t
    dmat = np.concatenate(
        [np.where(dt >= 0, np.exp(dt * g), 0.0) for g in _log_gamma()], axis=1)
    return (jnp.asarray(q_dec, F32), jnp.asarray(k_dec, F32), jnp.asarray(s_dec, F32),
            jnp.asarray(dmat, F32))


def _level_tables():
    t = np.arange(CHUNK)[:, None]
    r = np.arange(CHUNK)[None, :]
    blocks = [r <= t, r > t]
    pairs = []
    for i in range(N_LEVELS):
        m = 1 << i
        mid = t - (t % (2 * m)) + m
        up = (t % (2 * m)) >= m
        blocks.append(np.where(up, (r >= mid) & (r <= t), (r > t) & (r <= mid - 1)))
        same = (t // (2 * m)) == (r // (2 * m))
        pr = same & ((t % (2 * m)) < m) & ((r % (2 * m)) >= m)
        pairs.append(np.tile(pr, (1, N_HEADS)))
    dsum = np.tile(np.concatenate(blocks, axis=0), (1, 3))
    causal = np.tile(t < r, (1, N_HEADS))
    assert (np.sum(np.stack(pairs), axis=0) == causal).all()
    return jnp.asarray(dsum, BF16), jnp.asarray(np.stack(pairs), F32), jnp.asarray(causal, F32)


def _rope_tables(pos):
    half = HEAD_DIM // 2
    freq = ROPE_BASE ** (-jnp.arange(half, dtype=F32) / half)
    ang = pos[:, None] * freq[None, :]
    cos, sin = jnp.cos(ang), jnp.sin(ang)
    cos_h = jnp.concatenate([cos, cos], axis=-1)
    sin_h = jnp.concatenate([-sin, sin], axis=-1)
    return jnp.tile(cos_h, (1, N_HEADS)), jnp.tile(sin_h, (1, N_HEADS))


def _block_diag(w):
    h, dh, _ = w.shape
    eye = jnp.eye(h, dtype=w.dtype)
    return (eye[:, None, :, None] * w[:, :, None, :]).reshape(h * dh, h * dh)


V_HNORM, V_CW0, V_CB, V_BR, V_BI, V_LAM, V_RNORM, V_PSCALE = 0, 1, 5, 6, 7, 8, 9, 10
N_VEC_ROWS = 16


def _swap_halves(x):
    half = HEAD_DIM // 2
    lane = lax.broadcasted_iota(jnp.int32, (1, 128), 1)
    first = (lane % HEAD_DIM) < half
    parts = []
    for c in range(x.shape[1] // 128):
        xc = x[:, c * 128:(c + 1) * 128]
        parts.append(jnp.where(first, pltpu.roll(xc, 128 - half, 1), pltpu.roll(xc, half, 1)))
    return jnp.concatenate(parts, axis=1)


def _mixer_prompt_kernel(u_ref, lbl_ref, vec_ref, wr_ref, wi_ref, wp_ref, cos_ref, sin_ref,
                         ind_ref, dsum_ref, pair_ref, causal_ref, qdec_ref, kdec_ref, sdec_ref, dmat_ref,
                         oa_ref, ob_ref, oc_ref, od_ref, sh_ref, sl_ref, scv_ref, sr_ref, spl_ref,
                         hst_scr, rst_scr, hl_scr, xcv_scr, xpl_scr, oh_scr, orr_scr, perm_scr, sct_scr,
                         *, layer, tc, pos0):
    t_idx = pl.program_id(1)
    nchunk = tc // CHUNK

    @pl.when(t_idx == 0)
    def _():
        hst_scr[...] = jnp.zeros_like(hst_scr)
        rst_scr[...] = jnp.zeros_like(rst_scr)
        hl_scr[...] = jnp.zeros_like(hl_scr)
        xcv_scr[...] = jnp.zeros_like(xcv_scr)
        xpl_scr[0:PAD, :] = jnp.zeros((PAD, WIDTH), F32)

    def slot(s):
        return u_ref[:, s * WIDTH:(s + 1) * WIDTH]

    def vec(r):
        return vec_ref[r:r + 1, :]

    ind = ind_ref[...]
    lane = lax.broadcasted_iota(jnp.int32, (1, WIDTH), 1)
    hm_bf = [(lane // HEAD_DIM == h).astype(F32).astype(BF16) for h in range(N_HEADS)]
    lane_t = lax.broadcasted_iota(jnp.int32, (1, 128), 1)
    row = lax.broadcasted_iota(jnp.int32, (tc, WIDTH), 0)
    chunks = [slice(n * CHUNK, (n + 1) * CHUNK) for n in range(nchunk)]

    def head_scores(keys, queries):
        stacked = jnp.concatenate([queries * hm_bf[h] for h in range(N_HEADS)], axis=0)
        return _dot_nt(keys, stacked)

    def scores_times_values(sct, val):
        assert HEAD_DIM == 64 and WIDTH % 128 == 0
        tiles = []
        for c in range(WIDTH // 128):
            pair = _dot_tn(sct[:, 2 * c * CHUNK:(2 * c + 2) * CHUNK], val[:, c * 128:(c + 1) * 128])
            tiles.append(jnp.where(lane_t < HEAD_DIM, pair[0:CHUNK], pair[CHUNK:2 * CHUNK]))
        return jnp.concatenate(tiles, axis=1)

    n_lt = WIDTH // 128
    ind_t = ind[0:128, 0:128]

    def lane_tile(x, c):
        return x[:, c * 128:(c + 1) * 128]

    def tile_outer(a, b):
        return [_dot_tn(lane_tile(a, c), lane_tile(b, c)) for c in range(n_lt)]

    ind2 = jnp.concatenate([ind, ind], axis=0)

    def head_sum(x):
        return _dot(jnp.concatenate(_split2(x), axis=1), ind2)

    lb = _lower_bound(lbl_ref[...], layer)
    uq, z, v = slot(0), slot(1), slot(2)
    q = _silu(uq)
    sg = _sigmoid(z)
    f = lb + (1.0 - lb) * sg
    logf = jnp.log(jnp.maximum(f, F_FLOOR))
    k = (1.0 - lb) * (1.0 - sg)
    l_hi, l_mid, l_lo = _split3(logf)
    v_bf = v.astype(BF16)
    dsum = dsum_ref[...]
    rowc = lax.broadcasted_iota(jnp.int32, (CHUNK, WIDTH), 0)
    upper = [((rowc >> i) & 1) == 1 for i in range(N_LEVELS)]
    lsplit = [jnp.concatenate([l_hi[rs], l_mid[rs], l_lo[rs]], axis=0) for rs in chunks]
    logs = [_dot(dsum[0:2 * CHUNK], lsplit[n]) for n in range(nchunk)]
    dec = [jnp.exp(jnp.minimum(lg, 0.0)) for lg in logs]
    qe = [(q[rs] * dec[n][0:CHUNK]).astype(BF16) for n, rs in enumerate(chunks)]
    total_decay = logs[0][CHUNK - 1:CHUNK]
    for lg in logs[1:]:
        total_decay = jnp.minimum(total_decay, lg[CHUNK - 1:CHUNK])
    bounded = jnp.min(total_decay) > -SAFE_LOG

    @pl.when(bounded)
    def _():
        for n, rs in enumerate(chunks):
            k_up = (k[rs] * jnp.exp(-logs[n][0:CHUNK])).astype(BF16)
            sct_scr[rs, :] = head_scores(k_up, qe[n]) * causal_ref[...]

    @pl.when(jnp.logical_not(bounded))
    def _():
        sct = [None] * nchunk
        for n, rs in enumerate(chunks):
            lev = jnp.exp(jnp.minimum(_dot(dsum[2 * CHUNK:], lsplit[n]), 0.0))
            for i in range(N_LEVELS):
                zz = (jnp.where(upper[i], q[rs], k[rs]) * lev[i * CHUNK:(i + 1) * CHUNK]).astype(BF16)
                term = head_scores(zz, zz) * pair_ref[i]
                sct[n] = term if sct[n] is None else sct[n] + term
            sct_scr[rs, :] = sct[n]

    o_diag = _dot((q * k).astype(BF16), ind) * v
    upd = [tile_outer(v_bf[rs], (k[rs] * dec[n][CHUNK:2 * CHUNK]).astype(BF16)) for n, rs in enumerate(chunks)]
    o_intra = [scores_times_values(sct_scr[rs, :].astype(BF16), v_bf[rs]) for rs in chunks]

    seg_len = tc // 8
    def strided_rows(ref, start, stride):
        return jnp.concatenate([ref[c, pl.ds(start, 8, stride=stride), :] for c in range(2)], axis=1)

    for c in range(2):
        perm_scr[c] = u_ref[:, 4 * WIDTH + c * 128:4 * WIDTH + (c + 1) * 128]
    slabs = [strided_rows(perm_scr, i, seg_len) for i in range(seg_len)]
    sub0 = lax.broadcasted_iota(jnp.int32, (8, WIDTH), 0) == 0
    front = [jnp.where(sub0, xcv_scr[8 + k:9 + k, :], pltpu.roll(slabs[seg_len + k], 1, 0))
             for k in range(-(CONV_WIDTH - 1), 0)]
    ext = front + slabs
    cw = [vec(V_CW0 + j) for j in range(CONV_WIDTH)]
    xc = jnp.concatenate(
        [vec(V_CB) + sum(cw[j] * ext[i + j] for j in range(CONV_WIDTH)) for i in range(seg_len)], axis=0)
    xc_bf = xc.astype(BF16)
    rg = _sigmoid(_dot(xc_bf, wr_ref[...]) + vec(V_BR))
    ig = _sigmoid(_dot(xc_bf, wi_ref[...]) + vec(V_BI))
    log_a = -LRU_C * rg * _softplus(-vec(V_LAM))
    a_all = jnp.exp(log_a)
    b_all = jnp.sqrt(jnp.maximum(_neg_expm1(2.0 * log_a), 0.0)) * (ig * xc)
    h_loc, a_loc = [b_all[0:8]], [a_all[0:8]]
    for i in range(1, seg_len):
        a_i = a_all[8 * i:8 * i + 8]
        h_loc.append(a_i * h_loc[-1] + b_all[8 * i:8 * i + 8])
        a_loc.append(a_i * a_loc[-1])
    carry = [hl_scr[0:1, :]]
    for s in range(8):
        carry.append(a_loc[-1][s:s + 1] * carry[-1] + h_loc[-1][s:s + 1])
    hl_scr[0:1, :] = carry[8]
    carry8 = jnp.concatenate(carry[0:8], axis=0)
    hperm = jnp.concatenate([h_loc[i] + a_loc[i] * carry8 for i in range(seg_len)], axis=0)
    for c in range(2):
        perm_scr[c] = hperm[:, c * 128:(c + 1) * 128]
    hseq = jnp.concatenate(
        [strided_rows(perm_scr, ((8 * r) % seg_len) * 8 + (8 * r) // seg_len, 8) for r in range(seg_len)],
        axis=0)
    ob_ref[...] = (hseq * jax.nn.gelu(slot(5))).astype(ob_ref.dtype)
    xcv_scr[...] = u_ref[tc - 8:tc, 4 * WIDTH:5 * WIDTH]

    cosv, sinv = cos_ref[...], sin_ref[...]
    cq, ck = slot(6), slot(7)
    qr = cq * cosv + _swap_halves(cq) * sinv
    kr = (ck * cosv + _swap_halves(ck) * sinv) * (HEAD_DIM ** -0.5)
    rv_bf = slot(8).astype(BF16)
    qdec, kdec, sdec, dmat = qdec_ref[...], kdec_ref[...], sdec_ref[...], dmat_ref[...]
    qr_bf, kr_bf = qr.astype(BF16), kr.astype(BF16)
    rsc = []
    for rs in chunks:
        rsc.append((head_scores(kr_bf[rs], qr_bf[rs]) * dmat).astype(BF16))
    r_intra = [scores_times_values(rsc[n], rv_bf[rs]) for n, rs in enumerate(chunks)]
    rupd = [tile_outer((kr[rs] * kdec).astype(BF16), rv_bf[rs]) for rs in chunks]

    ud = slot(10)
    xpl_scr[PAD:PAD + tc, :] = ud
    assert POOL_WINDOWS == (2, 4, 8, 16) and WIDTH // len(POOL_WINDOWS) == 64 and PAD >= POOL_BUF
    halves = []
    for c in range(2):
        acc = xpl_scr[:, c * 128:(c + 1) * 128]
        built = {}
        for sh in (1, 2, 4, 8):
            acc = acc + pltpu.roll(acc, sh, 0)
            built[2 * sh] = acc
        lo, hi = POOL_WINDOWS[2 * c], POOL_WINDOWS[2 * c + 1]
        halves.append(jnp.where(lane_t < 64, built[lo], built[hi])[PAD:PAD + tc])
    sel = jnp.concatenate(halves, axis=1)
    grp = lane // (WIDTH // len(POOL_WINDOWS))
    winl = jnp.full((1, WIDTH), float(POOL_WINDOWS[-1]), F32)
    for gi in range(len(POOL_WINDOWS) - 2, -1, -1):
        winl = jnp.where(grp == gi, float(POOL_WINDOWS[gi]), winl)
    pos = (row + t_idx * tc).astype(F32) + float(pos0)
    pooled = sel / jnp.minimum(winl, pos + 1.0)
    od = _dot((pooled - ud).astype(BF16), wp_ref[...]) * vec(V_PSCALE)
    od_ref[...] = od.astype(od_ref.dtype)
    xpl_scr[0:PAD, :] = xpl_scr[tc:tc + PAD, :]

    st = [hst_scr[c] for c in range(n_lt)]
    s_ret = [rst_scr[c] for c in range(n_lt)]
    for n, rs in enumerate(chunks):
        o_c = jnp.concatenate(
            [_dot_nt(lane_tile(qe[n], c), st[c].astype(BF16) * ind_t) for c in range(n_lt)], axis=1)
        st = [st[c] * lane_tile(dec[n][CHUNK - 1:CHUNK], c) + upd[n][c] for c in range(n_lt)]
        oh_scr[rs, :] = o_c + o_diag[rs] + o_intra[n]
        rqe = (qr[rs] * qdec).astype(BF16)
        o_c = jnp.concatenate(
            [_dot(lane_tile(rqe, c), s_ret[c].astype(BF16) * ind_t) for c in range(n_lt)], axis=1)
        s_ret = [s_ret[c] * lane_tile(sdec, c) + rupd[n][c] for c in range(n_lt)]
        orr_scr[rs, :] = o_c + r_intra[n]
    for c in range(n_lt):
        hst_scr[c] = st[c]
        rst_scr[c] = s_ret[c]
    o_h = oh_scr[...]
    ms = head_sum(o_h * o_h) * (1.0 / HEAD_DIM)
    oa_ref[...] = (o_h * lax.rsqrt(ms + EPS) * vec(V_HNORM) * _silu(slot(3))).astype(oa_ref.dtype)
    o_r = orr_scr[...]
    mu = head_sum(o_r) * (1.0 / HEAD_DIM)
    dev = o_r - mu
    var = head_sum(dev * dev) * (1.0 / HEAD_DIM)
    oc_ref[...] = (dev * lax.rsqrt(var + EPS) * vec(V_RNORM) * _silu(slot(9))).astype(oc_ref.dtype)

    @pl.when(t_idx == pl.num_programs(1) - 1)
    def _():
        sh_ref[...] = hst_scr[...]
        sr_ref[...] = rst_scr[...]
        sl_ref[...] = hl_scr[0:1, :]
        scv_ref[...] = xcv_scr[8 - (CONV_WIDTH - 1):8, :]
        spl_ref[...] = xpl_scr[PAD + tc - POOL_BUF:PAD + tc, :]


def _mixer_prompt(u, lb_logits, vecs, wr_bd, wi_bd, wp_bd, cos_t, sin_t, layer, *, batch, seq, tc, pos0):
    nt = seq // tc
    ind = jnp.asarray(_head_block_ones(), BF16)
    dsum, pair, causal = _level_tables()
    qdec, kdec, sdec, dmat = _retention_tables()
    depth = lb_logits.shape[0]

    def const(shape):
        return pl.BlockSpec(shape, lambda b, t: tuple(0 for _ in shape))

    row_spec = pl.BlockSpec((tc, WIDTH), lambda b, t: (b * nt + t, 0))
    tab_spec = pl.BlockSpec((tc, WIDTH), lambda b, t: (t, 0))

    def state_spec(rows):
        return pl.BlockSpec((None, rows, WIDTH), lambda b, t: (b, 0, 0))

    mat_tiles = (WIDTH // 128, 128, 128)
    mat_spec = pl.BlockSpec((None,) + mat_tiles, lambda b, t: (b, 0, 0, 0))
    n = batch * seq
    outs = pl.pallas_call(
        functools.partial(_mixer_prompt_kernel, layer=layer, tc=tc, pos0=pos0),
        grid=(batch, nt),
        in_specs=[
            pl.BlockSpec((tc, MIX_COLS), lambda b, t: (b * nt + t, 0)),
            const((depth, WIDTH)),
            pl.BlockSpec((None, N_VEC_ROWS, WIDTH), lambda b, t: (layer, 0, 0)),
            pl.BlockSpec((None, WIDTH, WIDTH), lambda b, t: (layer, 0, 0)),
            pl.BlockSpec((None, WIDTH, WIDTH), lambda b, t: (layer, 0, 0)),
            pl.BlockSpec((None, WIDTH, WIDTH), lambda b, t: (layer, 0, 0)),
            tab_spec, tab_spec,
            const((WIDTH, WIDTH)), const(dsum.shape), const(pair.shape), const(causal.shape),
            const((CHUNK, WIDTH)), const((CHUNK, WIDTH)), const((1, WIDTH)),
            const((CHUNK, N_HEADS * CHUNK)),
        ],
        out_specs=[row_spec, row_spec, row_spec, row_spec,
                   mat_spec, state_spec(1), state_spec(CONV_WIDTH - 1), mat_spec,
                   state_spec(POOL_BUF)],
        out_shape=[jax.ShapeDtypeStruct((n, WIDTH), BF16)] * 4 + [
            jax.ShapeDtypeStruct((batch,) + mat_tiles, F32),
            jax.ShapeDtypeStruct((batch, 1, WIDTH), F32),
            jax.ShapeDtypeStruct((batch, CONV_WIDTH - 1, WIDTH), F32),
            jax.ShapeDtypeStruct((batch,) + mat_tiles, F32),
            jax.ShapeDtypeStruct((batch, POOL_BUF, WIDTH), F32),
        ],
        scratch_shapes=[
            pltpu.VMEM(mat_tiles, F32), pltpu.VMEM(mat_tiles, F32), pltpu.VMEM((8, WIDTH), F32),
            pltpu.VMEM((8, WIDTH), F32), pltpu.VMEM((PAD + tc, WIDTH), F32),
            pltpu.VMEM((tc, WIDTH), F32), pltpu.VMEM((tc, WIDTH), F32), pltpu.VMEM((2, tc, 128), F32),
            pltpu.VMEM((tc, N_HEADS * CHUNK), F32),
        ],
        compiler_params=_cparams("parallel", "arbitrary"),
        name="mixer_prompt",
    )(u, lb_logits, vecs, wr_bd, wi_bd, wp_bd, cos_t, sin_t, ind, dsum, pair, causal, qdec, kdec, sdec, dmat)
    return outs


def _diag_blocks(s):
    per_tile = 128 // HEAD_DIM
    return jnp.stack([s[:, h // per_tile,
                        (h % per_tile) * HEAD_DIM:(h % per_tile + 1) * HEAD_DIM,
                        (h % per_tile) * HEAD_DIM:(h % per_tile + 1) * HEAD_DIM]
                      for h in range(N_HEADS)], axis=1)


def _sample_step_kernel(u_ref, lblt_ref, vec_ref, vect_ref, wr_ref, wi_ref, wp_ref, cost_ref, sint_ref,
                        sh_ref, sl_ref, scv_ref, sr_ref, spl_ref,
                        oa_ref, ob_ref, oc_ref, od_ref, nh_ref, nl_ref, ncv_ref, nr_ref, npl_ref,
                        fac_scr, ot_scr, *, layer, pos0):
    step = pl.program_id(0)
    F_, K_, Q_, V_, RQ_, RK_, RV_ = range(7)

    def slot_t(s):
        return u_ref[:, s * WIDTH:(s + 1) * WIDTH].T

    def vcol(r):
        return vect_ref[:, r:r + 1]

    @pl.when(step == 0)
    def _():
        lblt = lblt_ref[...]
        e = jnp.exp(lblt - jnp.max(lblt, axis=1, keepdims=True))
        soft = e / jnp.sum(e, axis=1, keepdims=True)
        acc = soft[:, 0:1]
        for l in range(1, layer + 1):
            acc = acc + soft[:, l:l + 1]
        lb = acc - soft[:, 0:1]
        sg = _sigmoid(slot_t(1))
        fac_scr[F_] = jnp.maximum(lb + (1.0 - lb) * sg, F_FLOOR)
        fac_scr[K_] = (1.0 - lb) * (1.0 - sg)
        fac_scr[Q_] = _silu(slot_t(0))
        fac_scr[V_] = slot_t(2)

        def swap_rows(x):
            half = HEAD_DIM // 2
            parts = []
            for h in range(N_HEADS):
                parts.append(x[h * HEAD_DIM + half:(h + 1) * HEAD_DIM])
                parts.append(x[h * HEAD_DIM:h * HEAD_DIM + half])
            return jnp.concatenate(parts, axis=0)

        cost, sint = cost_ref[...], sint_ref[...]
        cq, ck = slot_t(6), slot_t(7)
        fac_scr[RQ_] = cq * cost + swap_rows(cq) * sint
        fac_scr[RK_] = (ck * cost + swap_rows(ck) * sint) * (HEAD_DIM ** -0.5)
        fac_scr[RV_] = slot_t(8)

    @pl.when(step < N_HEADS)
    def _():
        base = pl.multiple_of(step * HEAD_DIM, HEAD_DIM)
        gam = [float(np.exp(g)) for g in _log_gamma()]
        gamma = jnp.float32(gam[-1])
        for h in range(N_HEADS - 2, -1, -1):
            gamma = jnp.where(step == h, jnp.float32(gam[h]), gamma)
        v_t = fac_scr[V_, pl.ds(base, HEAD_DIM), :]
        rv_t = fac_scr[RV_, pl.ds(base, HEAD_DIM), :]
        acc_a = jnp.zeros((HEAD_DIM, v_t.shape[1]), F32)
        acc_c = jnp.zeros((HEAD_DIM, v_t.shape[1]), F32)
        for k in range(HEAD_DIM):
            def row(which):
                return fac_scr[which, pl.ds(base + k, 1), :]
            s_new = row(F_) * sh_ref[k] + row(K_) * v_t
            nh_ref[k] = s_new
            acc_a = acc_a + row(Q_) * s_new
            r_new = gamma * sr_ref[k] + row(RK_) * rv_t
            nr_ref[k] = r_new
            acc_c = acc_c + row(RQ_) * r_new
        ot_scr[0, pl.ds(base, HEAD_DIM), :] = acc_a
        ot_scr[1, pl.ds(base, HEAD_DIM), :] = acc_c

    @pl.when(step == N_HEADS)
    def _():
        def vec(r):
            return vec_ref[r:r + 1, :]

        def slot(s):
            return u_ref[:, s * WIDTH:(s + 1) * WIDTH]

        def per_head(x, fn):
            return jnp.concatenate([fn(x[h * HEAD_DIM:(h + 1) * HEAD_DIM]) for h in range(N_HEADS)], axis=0)

        def rms_head(o):
            return o * lax.rsqrt(jnp.mean(o * o, axis=0, keepdims=True) + EPS)

        def group_head(o):
            dev = o - jnp.mean(o, axis=0, keepdims=True)
            return dev * lax.rsqrt(jnp.mean(dev * dev, axis=0, keepdims=True) + EPS)

        o_a = per_head(ot_scr[0], rms_head) * vcol(V_HNORM) * _silu(slot_t(3))
        oa_ref[...] = o_a.T
        o_c = per_head(ot_scr[1], group_head) * vcol(V_RNORM) * _silu(slot_t(9))
        oc_ref[...] = o_c.T

        ux = slot(4)
        xc = vec(V_CB) + vec(V_CW0 + CONV_WIDTH - 1) * ux
        for j in range(CONV_WIDTH - 1):
            xc = xc + vec(V_CW0 + j) * scv_ref[j]
        xc_bf = xc.astype(BF16)
        rg = _sigmoid(_dot(xc_bf, wr_ref[...]) + vec(V_BR))
        ig = _sigmoid(_dot(xc_bf, wi_ref[...]) + vec(V_BI))
        log_a = -LRU_C * rg * _softplus(-vec(V_LAM))
        b_in = jnp.sqrt(jnp.maximum(_neg_expm1(2.0 * log_a), 0.0)) * (ig * xc)
        hnew = jnp.exp(log_a) * sl_ref[...] + b_in
        nl_ref[...] = hnew
        ob_ref[...] = hnew * jax.nn.gelu(slot(5))
        for j in range(CONV_WIDTH - 2):
            ncv_ref[j] = scv_ref[j + 1]
        ncv_ref[CONV_WIDTH - 2] = ux

        ud = slot(10)
        wsum = ud
        sums = []
        nxt = 1
        for win in POOL_WINDOWS:
            while nxt < win:
                wsum = wsum + spl_ref[POOL_BUF - nxt]
                nxt += 1
            sums.append(wsum)
        lane = lax.broadcasted_iota(jnp.int32, (1, WIDTH), 1)
        grp = lane // (WIDTH // len(POOL_WINDOWS))
        sel = sums[-1]
        winl = jnp.full((1, WIDTH), float(POOL_WINDOWS[-1]), F32)
        for gi in range(len(POOL_WINDOWS) - 2, -1, -1):
            sel = jnp.where(grp == gi, sums[gi], sel)
            winl = jnp.where(grp == gi, float(POOL_WINDOWS[gi]), winl)
        pooled = sel / jnp.minimum(winl, float(pos0) + 1.0)
        od_ref[...] = _dot((pooled - ud).astype(BF16), wp_ref[...]) * vec(V_PSCALE)
        for j in range(POOL_BUF - 1):
            npl_ref[j] = spl_ref[j + 1]
        npl_ref[POOL_BUF - 1] = ud


def _sample_step(u, lb_logits, vecs, vecs_t, wr_bd, wi_bd, wp_bd, cos_t, sin_t, s_hgrn, s_lru, s_conv,
                 s_ret, s_pool, layer, *, pos0):
    nb = u.shape[0]
    depth = lb_logits.shape[0]
    cost = jnp.broadcast_to(cos_t.reshape(WIDTH, 1), (WIDTH, nb))
    sint = jnp.broadcast_to(sin_t.reshape(WIDTH, 1), (WIDTH, nb))

    def const(shape):
        return pl.BlockSpec(shape, lambda i: tuple(0 for _ in shape))

    def layer_block(shape):
        return pl.BlockSpec((None,) + shape, lambda i: (layer,) + tuple(0 for _ in shape))

    def head(i):
        return jnp.minimum(i, N_HEADS - 1)

    mat = (HEAD_DIM, HEAD_DIM, nb)
    mat_in = pl.BlockSpec((None, None) + mat, lambda i: (layer, head(i), 0, 0, 0))
    mat_out = pl.BlockSpec((None,) + mat, lambda i: (head(i), 0, 0, 0))
    rows = const((nb, WIDTH))
    return pl.pallas_call(
        functools.partial(_sample_step_kernel, layer=layer, pos0=pos0),
        grid=(N_HEADS + 1,),
        in_specs=[
            const((nb, MIX_COLS)), const((WIDTH, depth)),
            layer_block((N_VEC_ROWS, WIDTH)), layer_block((WIDTH, N_VEC_ROWS)),
            layer_block((WIDTH, WIDTH)), layer_block((WIDTH, WIDTH)), layer_block((WIDTH, WIDTH)),
            const((WIDTH, nb)), const((WIDTH, nb)),
            mat_in, layer_block((nb, WIDTH)), layer_block((CONV_WIDTH - 1, nb, WIDTH)), mat_in,
            layer_block((POOL_BUF, nb, WIDTH)),
        ],
        out_specs=[rows, rows, rows, rows, mat_out, rows, const((CONV_WIDTH - 1, nb, WIDTH)), mat_out,
                   const((POOL_BUF, nb, WIDTH))],
        out_shape=[jax.ShapeDtypeStruct((nb, WIDTH), F32)] * 4 + [
            jax.ShapeDtypeStruct((N_HEADS,) + mat, F32), jax.ShapeDtypeStruct((nb, WIDTH), F32),
            jax.ShapeDtypeStruct((CONV_WIDTH - 1, nb, WIDTH), F32), jax.ShapeDtypeStruct((N_HEADS,) + mat, F32),
            jax.ShapeDtypeStruct((POOL_BUF, nb, WIDTH), F32),
        ],
        scratch_shapes=[pltpu.VMEM((7, WIDTH, nb), F32), pltpu.VMEM((2, WIDTH, nb), F32)],
        compiler_params=_cparams("arbitrary"),
        name="sample_step",
    )(u, lb_logits.T, vecs, vecs_t, wr_bd, wi_bd, wp_bd, cost, sint, s_hgrn, s_lru, s_conv, s_ret, s_pool)


def _pick_tile(n, pref):
    t = min(n, pref)
    while n % t:
        t //= 2
    return t


def kernel(x_prompt, x_sample, state_hgrn, state_rglru, state_conv, state_retention, state_pool, lb_logits, ffn1_norm, ffn1_up, ffn1_down, mix_norm, w_in, hgrn_norm, conv_w, conv_b, w_rgate, b_rgate, w_igate, b_igate, lru_lambda, ret_norm, w_pool, pool_scale, w_branch, w_o, ffn2_norm, ffn2_up, ffn2_down, final_norm):
    batch, seq, d = x_prompt.shape
    nb, dec_seq, _ = x_sample.shape
    assert dec_seq == 1
    depth = w_in.shape[0]
    past_len = PAST_LEN

    up1, dn1 = ffn1_up.astype(BF16), ffn1_down.astype(BF16)
    up2, dn2 = ffn2_up.astype(BF16), ffn2_down.astype(BF16)
    w_in_bf = w_in.astype(BF16)
    w_br, w_out = w_branch.astype(BF16), w_o.astype(BF16)
    wr_bd = jax.vmap(_block_diag)(w_rgate).astype(BF16)
    wi_bd = jax.vmap(_block_diag)(w_igate).astype(BF16)
    wp_bd = jax.vmap(_block_diag)(w_pool).astype(BF16)
    vecs = jnp.concatenate(
        [hgrn_norm[:, None], conv_w, conv_b[:, None], b_rgate[:, None], b_igate[:, None],
         lru_lambda[:, None], ret_norm[:, None], pool_scale[:, None],
         jnp.zeros((depth, N_VEC_ROWS - 11, WIDTH), F32)], axis=1)
    vecs_t = jnp.swapaxes(vecs, 1, 2)
    hgrn_t = jnp.transpose(state_hgrn, (0, 2, 3, 4, 1))
    ret_t = jnp.transpose(state_retention, (0, 2, 3, 4, 1))
    conv_t = jnp.transpose(state_conv, (0, 2, 1, 3))
    pool_t = jnp.transpose(state_pool, (0, 2, 1, 3))
    n1 = ffn1_norm[:, None, :]
    n2 = ffn2_norm[:, None, :]
    nm = mix_norm[:, None, :]
    fg = final_norm[None, :]

    cos_p, sin_p = _rope_tables(jnp.arange(seq, dtype=F32))
    cos_s, sin_s = _rope_tables(past_len + jnp.arange(dec_seq, dtype=F32))

    xp = x_prompt.reshape(batch * seq, d)
    xs = x_sample.reshape(nb * dec_seq, d)
    tm_p = _pick_tile(batch * seq, DENSE_ROWS)
    tc = _pick_tile(seq, MIXER_ROWS)

    st_p = [[] for _ in range(5)]
    st_s = [[] for _ in range(5)]
    for l in range(depth):
        last = l == depth - 1
        xp, xs = _ffn(xp, xs, n1, up1, dn1, l, fg, final_norm=False, tm=tm_p)
        u_p, u_s = _inproj(xp, xs, nm, w_in_bf, l, tm=tm_p)
        *br_p, sh, sl, scv, sr, spl = _mixer_prompt(
            u_p, lb_logits, vecs, wr_bd, wi_bd, wp_bd, cos_p, sin_p, l, batch=batch, seq=seq, tc=tc, pos0=0)
        *br_s, nh, nl, ncv, nr, npl = _sample_step(
            u_s, lb_logits, vecs, vecs_t, wr_bd, wi_bd, wp_bd, cos_s, sin_s, hgrn_t, state_rglru,
            conv_t, ret_t, pool_t, l, pos0=past_len)
        xp, xs = _merge(xp, xs, nm, br_p, br_s, w_in_bf, w_br, w_out, l, tm=tm_p)
        xp, xs = _ffn(xp, xs, n2, up2, dn2, l, fg, final_norm=last, tm=tm_p)
        for lst, s in zip(st_p, (jnp.swapaxes(_diag_blocks(sh), 2, 3), sl[:, 0], scv, _diag_blocks(sr), spl)):
            lst.append(s)
        for lst, s in zip(st_s, (nh, nl, ncv, nr, npl)):
            lst.append(s)

    y_p = xp.reshape(batch, seq, d)
    y_s = xs.reshape(nb, dec_seq, d)
    nh, nl, ncv, nr, npl = (jnp.stack(s) for s in st_s)
    sample_states = (jnp.transpose(nh, (0, 4, 1, 2, 3)), nl, jnp.transpose(ncv, (0, 2, 1, 3)),
                     jnp.transpose(nr, (0, 4, 1, 2, 3)), jnp.transpose(npl, (0, 2, 1, 3)))
    return (y_p, y_s) + tuple(jnp.stack(s) for s in st_p) + sample_states
```

```python
import functools

import numpy as np
import jax
import jax.numpy as jnp
from jax import lax
from jax.experimental import pallas as pl
from jax.experimental.pallas import tpu as pltpu

F32 = jnp.float32
BF16 = jnp.bfloat16

N_HEADS = 4
HEAD_DIM = 64
WIDTH = N_HEADS * HEAD_DIM
N_MIX_SLOTS = 11
MIX_COLS = N_MIX_SLOTS * WIDTH
CHUNK = 64
N_LEVELS = 6
PAD = 16
CONV_WIDTH = 4
LRU_C = 8.0
POOL_WINDOWS = (2, 4, 8, 16)
POOL_BUF = 15
ROPE_BASE = 10000.0
EPS = 1e-6
F_FLOOR = 1e-30
SAFE_LOG = 80.0
PAST_LEN = 16384
VMEM_LIMIT = 56 * 1024 * 1024
DENSE_ROWS = 512
WIDE_ROWS = 1024
FFN_UP_CHUNK = 512
FFN_DOWN_CHUNK = 256
MIXER_ROWS = 512


def _cparams(*sem):
    return pltpu.CompilerParams(dimension_semantics=sem, vmem_limit_bytes=VMEM_LIMIT)


def _dot(a, b):
    return jnp.dot(a, b, preferred_element_type=F32)


def _dot_nt(a, b):
    return lax.dot_general(a, b, (((1,), (1,)), ((), ())), preferred_element_type=F32)


def _dot_tn(a, b):
    return lax.dot_general(a, b, (((0,), (0,)), ((), ())), preferred_element_type=F32)


def _rms(x, g):
    return x * lax.rsqrt(jnp.mean(x * x, axis=-1, keepdims=True) + EPS) * g


def _sigmoid(x):
    return 0.5 * jnp.tanh(0.5 * x) + 0.5


def _silu(x):
    hx = 0.5 * x
    return hx * jnp.tanh(hx) + hx


def _softplus(x):
    return jnp.maximum(x, 0.0) + jnp.log1p(jnp.exp(-jnp.abs(x)))


def _neg_expm1(y):
    t = jnp.tanh(0.5 * y)
    return -2.0 * t / (1.0 - t)


def _split2(x):
    hi = x.astype(BF16)
    lo = (x - hi.astype(F32)).astype(BF16)
    return hi, lo


def _split3(x):
    hi = x.astype(BF16)
    r = x - hi.astype(F32)
    mid = r.astype(BF16)
    lo = (r - mid.astype(F32)).astype(BF16)
    return hi, mid, lo


def _lower_bound(lbl, layer):
    m = jnp.max(lbl, axis=0, keepdims=True)
    e = jnp.exp(lbl - m)
    soft = e / jnp.sum(e, axis=0, keepdims=True)
    acc = soft[0:1]
    for l in range(1, layer + 1):
        acc = acc + soft[l:l + 1]
    return acc - soft[0:1]


def _resident(shape, index_map):
    return pl.BlockSpec(shape, index_map, pipeline_mode=pl.Buffered(1))


def _two_group_kernel(*refs, body, n_rows, n_tiles):
    p_refs, s_refs = refs[:n_rows], refs[n_rows:2 * n_rows]
    params, (op_ref, os_ref) = refs[2 * n_rows:-2], refs[-2:]
    i = pl.program_id(0)

    @pl.when(i < n_tiles)
    def _():
        op_ref[...] = body([r[...] for r in p_refs], params)

    @pl.when(i == n_tiles)
    def _():
        os_ref[...] = body([r[...] for r in s_refs], params)


def _two_group_call(body, rows_p, rows_s, params, param_specs, out_cols, tm, name):
    n_p, n_s = rows_p[0].shape[0], rows_s[0].shape[0]
    n_tiles = n_p // tm

    def tile(i):
        return (jnp.minimum(i, n_tiles - 1), 0)

    return pl.pallas_call(
        functools.partial(_two_group_kernel, body=body, n_rows=len(rows_p), n_tiles=n_tiles),
        grid=(n_tiles + 1,),
        in_specs=[pl.BlockSpec((tm, a.shape[1]), tile) for a in rows_p]
        + [pl.BlockSpec(a.shape, lambda i: (0, 0)) for a in rows_s] + list(param_specs),
        out_specs=[pl.BlockSpec((tm, out_cols), tile), pl.BlockSpec((n_s, out_cols), lambda i: (0, 0))],
        out_shape=[jax.ShapeDtypeStruct((n_p, out_cols), F32), jax.ShapeDtypeStruct((n_s, out_cols), F32)],
        compiler_params=_cparams("arbitrary"),
        name=name,
    )(*rows_p, *rows_s, *params)


def _ffn_body(rows, params, *, final_norm):
    (x,), (g_ref, wu_ref, wd_ref, fg_ref) = rows, params
    dff = wd_ref.shape[0]
    h = _rms(x, g_ref[...]).astype(BF16)
    a = _dot(h, wu_ref[:, :dff])
    b = _dot(h, wu_ref[:, dff:])
    act = (_silu(a) * b).astype(BF16)
    y = x + 0.5 * _dot(act, wd_ref[...])
    return _rms(y, fg_ref[...]) if final_norm else y


def _cast_chunks(n_chunks, src_chunk, dst_chunk, stage, sem):
    def copy(c):
        return pltpu.make_async_copy(src_chunk(c), stage.at[c % 2], sem.at[c % 2])

    copy(0).start()
    for c in range(n_chunks):
        if c + 1 < n_chunks:
            copy(c + 1).start()
        copy(c).wait()
        dst_chunk(c)[...] = stage[c % 2].astype(BF16)


def _ffn_kernel(xp_ref, xs_ref, g_ref, wu_hbm, wd_hbm, fg_ref, op_ref, os_ref,
                wu_bf, wd_bf, stage_u, stage_d, sem_u, sem_d, *, layer, n_tiles, final_norm):
    i = pl.program_id(0)
    cw, rw = stage_u.shape[2], stage_d.shape[1]

    @pl.when(i == 0)
    def _():
        _cast_chunks(wu_bf.shape[1] // cw,
                     lambda c: wu_hbm.at[layer, :, pl.ds(c * cw, cw)],
                     lambda c: wu_bf.at[:, pl.ds(c * cw, cw)], stage_u, sem_u)
        _cast_chunks(wd_bf.shape[0] // rw,
                     lambda c: wd_hbm.at[layer, pl.ds(c * rw, rw), :],
                     lambda c: wd_bf.at[pl.ds(c * rw, rw), :], stage_d, sem_d)

    body = functools.partial(_ffn_body, params=(g_ref, wu_bf, wd_bf, fg_ref), final_norm=final_norm)

    @pl.when(i < n_tiles)
    def _():
        op_ref[...] = body([xp_ref[...]])

    @pl.when(i == n_tiles)
    def _():
        os_ref[...] = body([xs_ref[...]])


def _ffn(xp, xs, g, w_up, w_down, layer, final_g, *, final_norm, tm):
    (n_p, d), n_s = xp.shape, xs.shape[0]
    dff = w_down.shape[1]
    n_tiles = n_p // tm
    assert (2 * dff) % FFN_UP_CHUNK == 0 and dff % FFN_DOWN_CHUNK == 0

    def tile(i):
        return (jnp.minimum(i, n_tiles - 1), 0)

    return pl.pallas_call(
        functools.partial(_ffn_kernel, layer=layer, n_tiles=n_tiles, final_norm=final_norm),
        grid=(n_tiles + 1,),
        in_specs=[
            pl.BlockSpec((tm, d), tile),
            pl.BlockSpec((n_s, d), lambda i: (0, 0)),
            _resident((None, 1, d), lambda i: (layer, 0, 0)),
            pl.BlockSpec(memory_space=pl.ANY),
            pl.BlockSpec(memory_space=pl.ANY),
            _resident((1, d), lambda i: (0, 0)),
        ],
        out_specs=[pl.BlockSpec((tm, d), tile), pl.BlockSpec((n_s, d), lambda i: (0, 0))],
        out_shape=[jax.ShapeDtypeStruct((n_p, d), F32), jax.ShapeDtypeStruct((n_s, d), F32)],
        scratch_shapes=[
            pltpu.VMEM((d, 2 * dff), BF16), pltpu.VMEM((dff, d), BF16),
            pltpu.VMEM((2, d, FFN_UP_CHUNK), F32), pltpu.VMEM((2, FFN_DOWN_CHUNK, d), F32),
            pltpu.SemaphoreType.DMA((2,)), pltpu.SemaphoreType.DMA((2,)),
        ],
        compiler_params=_cparams("arbitrary"),
        name="ffn",
    )(xp, xs, g, w_up, w_down, final_g)


def _inproj_body(rows, params):
    (x,), (g_ref, w_ref) = rows, params
    return _dot(_rms(x, g_ref[...]).astype(BF16), w_ref[...])


def _inproj(xp, xs, g, w_in, layer, *, tm):
    d = xp.shape[1]
    specs = [
        _resident((None, 1, d), lambda i: (layer, 0, 0)),
        _resident((None, d, MIX_COLS), lambda i: (layer, 0, 0)),
    ]
    return _two_group_call(_inproj_body, [xp], [xs], (g, w_in), specs, MIX_COLS, tm, "inproj")


def _merge_body(rows, params):
    x, branches = rows[0], rows[1:]
    g_ref, win_ref, wb_ref, wo_ref = params
    d = x.shape[1]
    h = _rms(x, g_ref[...]).astype(BF16)
    merged = None
    for b, o_b in enumerate(branches):
        gate = _sigmoid(_dot(h, win_ref[:, MIX_COLS + b * d:MIX_COLS + (b + 1) * d]))
        y = _dot(o_b.astype(BF16), wb_ref[b])
        merged = gate * y if merged is None else merged + gate * y
    return x + _dot(merged.astype(BF16), wo_ref[...])


def _merge(xp, xs, g, branches_p, branches_s, w_in, w_branch, w_o, layer, *, tm):
    d = xp.shape[1]
    w = branches_p[0].shape[1]
    nb = len(branches_p)
    assert w_in.shape[2] == MIX_COLS + nb * d and MIX_COLS % 128 == 0
    specs = [
        _resident((None, 1, d), lambda i: (layer, 0, 0)),
        _resident((None, d, w_in.shape[2]), lambda i: (layer, 0, 0)),
        _resident((None, nb, w, d), lambda i: (layer, 0, 0, 0)),
        _resident((None, d, d), lambda i: (layer, 0, 0)),
    ]
    return _two_group_call(_merge_body, [xp, *branches_p], [xs, *branches_s],
                           (g, w_in, w_branch, w_o), specs, d, tm, "merge")


def _head_block_ones():
    h = np.arange(WIDTH) // HEAD_DIM
    return (h[:, None] == h[None, :]).astype(np.float32)


def _log_gamma():
    return np.log1p(-(2.0 ** (-5.0 - np.arange(N_HEADS, dtype=np.float64))))


def _retention_tables():
    lg = np.repeat(_log_gamma(), HEAD_DIM)[None, :]
    t = np.arange(CHUNK, dtype=np.float64)[:, None]
    q_dec = np.exp((t + 1.0) * lg)
    k_dec = np.exp((CHUNK - 1.0 - t) * lg)
    s_dec = np.exp(CHUNK * lg)
    dt = t.T - t
    dmat = np.concatenate(
        [np.where(dt >= 0, np.exp(dt * g), 0.0) for g in _log_gamma()], axis=1)
    return (jnp.asarray(q_dec, F32), jnp.asarray(k_dec, F32), jnp.asarray(s_dec, F32),
            jnp.asarray(dmat, F32))


def _level_tables():
    t = np.arange(CHUNK)[:, None]
    r = np.arange(CHUNK)[None, :]
    blocks = [r <= t, r > t]
    pairs = []
    for i in range(N_LEVELS):
        m = 1 << i
        mid = t - (t % (2 * m)) + m
        up = (t % (2 * m)) >= m
        blocks.append(np.where(up, (r >= mid) & (r <= t), (r > t) & (r <= mid - 1)))
        same = (t // (2 * m)) == (r // (2 * m))
        pr = same & ((t % (2 * m)) < m) & ((r % (2 * m)) >= m)
        pairs.append(np.tile(pr, (1, N_HEADS)))
    dsum = np.tile(np.concatenate(blocks, axis=0), (1, 3))
    causal = np.tile(t < r, (1, N_HEADS))
    assert (np.sum(np.stack(pairs), axis=0) == causal).all()
    return jnp.asarray(dsum, BF16), jnp.asarray(np.stack(pairs), F32), jnp.asarray(causal, F32)


def _rope_tables(pos):
    half = HEAD_DIM // 2
    freq = ROPE_BASE ** (-jnp.arange(half, dtype=F32) / half)
    ang = pos[:, None] * freq[None, :]
    cos, sin = jnp.cos(ang), jnp.sin(ang)
    cos_h = jnp.concatenate([cos, cos], axis=-1)
    sin_h = jnp.concatenate([-sin, sin], axis=-1)
    return jnp.tile(cos_h, (1, N_HEADS)), jnp.tile(sin_h, (1, N_HEADS))


def _block_diag(w):
    h, dh, _ = w.shape
    eye = jnp.eye(h, dtype=w.dtype)
    return (eye[:, None, :, None] * w[:, :, None, :]).reshape(h * dh, h * dh)


V_HNORM, V_CW0, V_CB, V_BR, V_BI, V_LAM, V_RNORM, V_PSCALE = 0, 1, 5, 6, 7, 8, 9, 10
N_VEC_ROWS = 16


def _swap_halves(x):
    half = HEAD_DIM // 2
    lane = lax.broadcasted_iota(jnp.int32, (1, 128), 1)
    first = (lane % HEAD_DIM) < half
    parts = []
    for c in range(x.shape[1] // 128):
        xc = x[:, c * 128:(c + 1) * 128]
        parts.append(jnp.where(first, pltpu.roll(xc, 128 - half, 1), pltpu.roll(xc, half, 1)))
    return jnp.concatenate(parts, axis=1)


def _mixer_prompt_kernel(u_ref, lbl_ref, vec_ref, wr_ref, wi_ref, wp_ref, cos_ref, sin_ref,
                         ind_ref, dsum_ref, pair_ref, causal_ref, qdec_ref, kdec_ref, sdec_ref, dmat_ref,
                         oa_ref, ob_ref, oc_ref, od_ref, sh_ref, sl_ref, scv_ref, sr_ref, spl_ref,
                         hst_scr, rst_scr, hl_scr, xcv_scr, xpl_scr, oh_scr, orr_scr, perm_scr, sct_scr,
                         *, layer, tc, pos0):
    t_idx = pl.program_id(1)
    nchunk = tc // CHUNK

    @pl.when(t_idx == 0)
    def _():
        hst_scr[...] = jnp.zeros_like(hst_scr)
        rst_scr[...] = jnp.zeros_like(rst_scr)
        hl_scr[...] = jnp.zeros_like(hl_scr)
        xcv_scr[...] = jnp.zeros_like(xcv_scr)
        xpl_scr[0:PAD, :] = jnp.zeros((PAD, WIDTH), F32)

    def slot(s):
        return u_ref[:, s * WIDTH:(s + 1) * WIDTH]

    def vec(r):
        return vec_ref[r:r + 1, :]

    ind = ind_ref[...]
    lane = lax.broadcasted_iota(jnp.int32, (1, WIDTH), 1)
    hm_bf = [(lane // HEAD_DIM == h).astype(F32).astype(BF16) for h in range(N_HEADS)]
    lane_t = lax.broadcasted_iota(jnp.int32, (1, 128), 1)
    row = lax.broadcasted_iota(jnp.int32, (tc, WIDTH), 0)
    chunks = [slice(n * CHUNK, (n + 1) * CHUNK) for n in range(nchunk)]

    def head_scores(keys, queries):
        stacked = jnp.concatenate([queries * hm_bf[h] for h in range(N_HEADS)], axis=0)
        return _dot_nt(keys, stacked)

    def scores_times_values(sct, val):
        assert HEAD_DIM == 64 and WIDTH % 128 == 0
        tiles = []
        for c in range(WIDTH // 128):
            pair = _dot_tn(sct[:, 2 * c * CHUNK:(2 * c + 2) * CHUNK], val[:, c * 128:(c + 1) * 128])
            tiles.append(jnp.where(lane_t < HEAD_DIM, pair[0:CHUNK], pair[CHUNK:2 * CHUNK]))
        return jnp.concatenate(tiles, axis=1)

    n_lt = WIDTH // 128
    ind_t = ind[0:128, 0:128]

    def lane_tile(x, c):
        return x[:, c * 128:(c + 1) * 128]

    def tile_outer(a, b):
        return [_dot_tn(lane_tile(a, c), lane_tile(b, c)) for c in range(n_lt)]

    ind2 = jnp.concatenate([ind, ind], axis=0)

    def head_sum(x):
        return _dot(jnp.concatenate(_split2(x), axis=1), ind2)

    lb = _lower_bound(lbl_ref[...], layer)
    uq, z, v = slot(0), slot(1), slot(2)
    q = _silu(uq)
    sg = _sigmoid(z)
    f = lb + (1.0 - lb) * sg
    logf = jnp.log(jnp.maximum(f, F_FLOOR))
    k = (1.0 - lb) * (1.0 - sg)
    l_hi, l_mid, l_lo = _split3(logf)
    v_bf = v.astype(BF16)
    dsum = dsum_ref[...]
    rowc = lax.broadcasted_iota(jnp.int32, (CHUNK, WIDTH), 0)
    upper = [((rowc >> i) & 1) == 1 for i in range(N_LEVELS)]
    lsplit = [jnp.concatenate([l_hi[rs], l_mid[rs], l_lo[rs]], axis=0) for rs in chunks]
    logs = [_dot(dsum[0:2 * CHUNK], lsplit[n]) for n in range(nchunk)]
    dec = [jnp.exp(jnp.minimum(lg, 0.0)) for lg in logs]
    qe = [(q[rs] * dec[n][0:CHUNK]).astype(BF16) for n, rs in enumerate(chunks)]
    total_decay = logs[0][CHUNK - 1:CHUNK]
    for lg in logs[1:]:
        total_decay = jnp.minimum(total_decay, lg[CHUNK - 1:CHUNK])
    bounded = jnp.min(total_decay) > -SAFE_LOG

    @pl.when(bounded)
    def _():
        for n, rs in enumerate(chunks):
            k_up = (k[rs] * jnp.exp(-logs[n][0:CHUNK])).astype(BF16)
            sct_scr[rs, :] = head_scores(k_up, qe[n]) * causal_ref[...]

    @pl.when(jnp.logical_not(bounded))
    def _():
        sct = [None] * nchunk
        for n, rs in enumerate(chunks):
            lev = jnp.exp(jnp.minimum(_dot(dsum[2 * CHUNK:], lsplit[n]), 0.0))
            for i in range(N_LEVELS):
                zz = (jnp.where(upper[i], q[rs], k[rs]) * lev[i * CHUNK:(i + 1) * CHUNK]).astype(BF16)
                term = head_scores(zz, zz) * pair_ref[i]
                sct[n] = term if sct[n] is None else sct[n] + term
            sct_scr[rs, :] = sct[n]

    o_diag = _dot((q * k).astype(BF16), ind) * v
    upd = [tile_outer(v_bf[rs], (k[rs] * dec[n][CHUNK:2 * CHUNK]).astype(BF16)) for n, rs in enumerate(chunks)]
    o_intra = [scores_times_values(sct_scr[rs, :].astype(BF16), v_bf[rs]) for rs in chunks]

    seg_len = tc // 8
    def strided_rows(ref, start, stride):
        return jnp.concatenate([ref[c, pl.ds(start, 8, stride=stride), :] for c in range(2)], axis=1)

    for c in range(2):
        perm_scr[c] = u_ref[:, 4 * WIDTH + c * 128:4 * WIDTH + (c + 1) * 128]
    slabs = [strided_rows(perm_scr, i, seg_len) for i in range(seg_len)]
    sub0 = lax.broadcasted_iota(jnp.int32, (8, WIDTH), 0) == 0
    front = [jnp.where(sub0, xcv_scr[8 + k:9 + k, :], pltpu.roll(slabs[seg_len + k], 1, 0))
             for k in range(-(CONV_WIDTH - 1), 0)]
    ext = front + slabs
    cw = [vec(V_CW0 + j) for j in range(CONV_WIDTH)]
    xc = jnp.concatenate(
        [vec(V_CB) + sum(cw[j] * ext[i + j] for j in range(CONV_WIDTH)) for i in range(seg_len)], axis=0)
    xc_bf = xc.astype(BF16)
    rg = _sigmoid(_dot(xc_bf, wr_ref[...]) + vec(V_BR))
    ig = _sigmoid(_dot(xc_bf, wi_ref[...]) + vec(V_BI))
    log_a = -LRU_C * rg * _softplus(-vec(V_LAM))
    a_all = jnp.exp(log_a)
    b_all = jnp.sqrt(jnp.maximum(_neg_expm1(2.0 * log_a), 0.0)) * (ig * xc)
    h_loc, a_loc = [b_all[0:8]], [a_all[0:8]]
    for i in range(1, seg_len):
        a_i = a_all[8 * i:8 * i + 8]
        h_loc.append(a_i * h_loc[-1] + b_all[8 * i:8 * i + 8])
        a_loc.append(a_i * a_loc[-1])
    carry = [hl_scr[0:1, :]]
    for s in range(8):
        carry.append(a_loc[-1][s:s + 1] * carry[-1] + h_loc[-1][s:s + 1])
    hl_scr[0:1, :] = carry[8]
    carry8 = jnp.concatenate(carry[0:8], axis=0)
    hperm = jnp.concatenate([h_loc[i] + a_loc[i] * carry8 for i in range(seg_len)], axis=0)
    for c in range(2):
        perm_scr[c] = hperm[:, c * 128:(c + 1) * 128]
    hseq = jnp.concatenate(
        [strided_rows(perm_scr, ((8 * r) % seg_len) * 8 + (8 * r) // seg_len, 8) for r in range(seg_len)],
        axis=0)
    ob_ref[...] = (hseq * jax.nn.gelu(slot(5))).astype(ob_ref.dtype)
    xcv_scr[...] = u_ref[tc - 8:tc, 4 * WIDTH:5 * WIDTH]

    cosv, sinv = cos_ref[...], sin_ref[...]
    cq, ck = slot(6), slot(7)
    qr = cq * cosv + _swap_halves(cq) * sinv
    kr = (ck * cosv + _swap_halves(ck) * sinv) * (HEAD_DIM ** -0.5)
    rv_bf = slot(8).astype(BF16)
    qdec, kdec, sdec, dmat = qdec_ref[...], kdec_ref[...], sdec_ref[...], dmat_ref[...]
    qr_bf, kr_bf = qr.astype(BF16), kr.astype(BF16)
    rsc = []
    for rs in chunks:
        rsc.append((head_scores(kr_bf[rs], qr_bf[rs]) * dmat).astype(BF16))
    r_intra = [scores_times_values(rsc[n], rv_bf[rs]) for n, rs in enumerate(chunks)]
    rupd = [tile_outer((kr[rs] * kdec).astype(BF16), rv_bf[rs]) for rs in chunks]

    ud = slot(10)
    xpl_scr[PAD:PAD + tc, :] = ud
    assert POOL_WINDOWS == (2, 4, 8, 16) and WIDTH // len(POOL_WINDOWS) == 64 and PAD >= POOL_BUF
    halves = []
    for c in range(2):
        acc = xpl_scr[:, c * 128:(c + 1) * 128]
        built = {}
        for sh in (1, 2, 4, 8):
            acc = acc + pltpu.roll(acc, sh, 0)
            built[2 * sh] = acc
        lo, hi = POOL_WINDOWS[2 * c], POOL_WINDOWS[2 * c + 1]
        halves.append(jnp.where(lane_t < 64, built[lo], built[hi])[PAD:PAD + tc])
    sel = jnp.concatenate(halves, axis=1)
    grp = lane // (WIDTH // len(POOL_WINDOWS))
    winl = jnp.full((1, WIDTH), float(POOL_WINDOWS[-1]), F32)
    for gi in range(len(POOL_WINDOWS) - 2, -1, -1):
        winl = jnp.where(grp == gi, float(POOL_WINDOWS[gi]), winl)
    pos = (row + t_idx * tc).astype(F32) + float(pos0)
    pooled = sel / jnp.minimum(winl, pos + 1.0)
    od = _dot((pooled - ud).astype(BF16), wp_ref[...]) * vec(V_PSCALE)
    od_ref[...] = od.astype(od_ref.dtype)
    xpl_scr[0:PAD, :] = xpl_scr[tc:tc + PAD, :]

    st = [hst_scr[c] for c in range(n_lt)]
    s_ret = [rst_scr[c] for c in range(n_lt)]
    for n, rs in enumerate(chunks):
        o_c = jnp.concatenate(
            [_dot_nt(lane_tile(qe[n], c), st[c].astype(BF16) * ind_t) for c in range(n_lt)], axis=1)
        st = [st[c] * lane_tile(dec[n][CHUNK - 1:CHUNK], c) + upd[n][c] for c in range(n_lt)]
        oh_scr[rs, :] = o_c + o_diag[rs] + o_intra[n]
        rqe = (qr[rs] * qdec).astype(BF16)
        o_c = jnp.concatenate(
            [_dot(lane_tile(rqe, c), s_ret[c].astype(BF16) * ind_t) for c in range(n_lt)], axis=1)
        s_ret = [s_ret[c] * lane_tile(sdec, c) + rupd[n][c] for c in range(n_lt)]
        orr_scr[rs, :] = o_c + r_intra[n]
    for c in range(n_lt):
        hst_scr[c] = st[c]
        rst_scr[c] = s_ret[c]
    o_h = oh_scr[...]
    ms = head_sum(o_h * o_h) * (1.0 / HEAD_DIM)
    oa_ref[...] = (o_h * lax.rsqrt(ms + EPS) * vec(V_HNORM) * _silu(slot(3))).astype(oa_ref.dtype)
    o_r = orr_scr[...]
    mu = head_sum(o_r) * (1.0 / HEAD_DIM)
    dev = o_r - mu
    var = head_sum(dev * dev) * (1.0 / HEAD_DIM)
    oc_ref[...] = (dev * lax.rsqrt(var + EPS) * vec(V_RNORM) * _silu(slot(9))).astype(oc_ref.dtype)

    @pl.when(t_idx == pl.num_programs(1) - 1)
    def _():
        sh_ref[...] = hst_scr[...]
        sr_ref[...] = rst_scr[...]
        sl_ref[...] = hl_scr[0:1, :]
        scv_ref[...] = xcv_scr[8 - (CONV_WIDTH - 1):8, :]
        spl_ref[...] = xpl_scr[PAD + tc - POOL_BUF:PAD + tc, :]


def _mixer_prompt(u, lb_logits, vecs, wr_bd, wi_bd, wp_bd, cos_t, sin_t, layer, *, batch, seq, tc, pos0):
    nt = seq // tc
    ind = jnp.asarray(_head_block_ones(), BF16)
    dsum, pair, causal = _level_tables()
    qdec, kdec, sdec, dmat = _retention_tables()
    depth = lb_logits.shape[0]

    def const(shape):
        return pl.BlockSpec(shape, lambda b, t: tuple(0 for _ in shape))

    row_spec = pl.BlockSpec((tc, WIDTH), lambda b, t: (b * nt + t, 0))
    tab_spec = pl.BlockSpec((tc, WIDTH), lambda b, t: (t, 0))

    def state_spec(rows):
        return pl.BlockSpec((None, rows, WIDTH), lambda b, t: (b, 0, 0))

    mat_tiles = (WIDTH // 128, 128, 128)
    mat_spec = pl.BlockSpec((None,) + mat_tiles, lambda b, t: (b, 0, 0, 0))
    n = batch * seq
    outs = pl.pallas_call(
        functools.partial(_mixer_prompt_kernel, layer=layer, tc=tc, pos0=pos0),
        grid=(batch, nt),
        in_specs=[
            pl.BlockSpec((tc, MIX_COLS), lambda b, t: (b * nt + t, 0)),
            const((depth, WIDTH)),
            pl.BlockSpec((None, N_VEC_ROWS, WIDTH), lambda b, t: (layer, 0, 0)),
            pl.BlockSpec((None, WIDTH, WIDTH), lambda b, t: (layer, 0, 0)),
            pl.BlockSpec((None, WIDTH, WIDTH), lambda b, t: (layer, 0, 0)),
            pl.BlockSpec((None, WIDTH, WIDTH), lambda b, t: (layer, 0, 0)),
            tab_spec, tab_spec,
            const((WIDTH, WIDTH)), const(dsum.shape), const(pair.shape), const(causal.shape),
            const((CHUNK, WIDTH)), const((CHUNK, WIDTH)), const((1, WIDTH)),
            const((CHUNK, N_HEADS * CHUNK)),
        ],
        out_specs=[row_spec, row_spec, row_spec, row_spec,
                   mat_spec, state_spec(1), state_spec(CONV_WIDTH - 1), mat_spec,
                   state_spec(POOL_BUF)],
        out_shape=[jax.ShapeDtypeStruct((n, WIDTH), BF16)] * 4 + [
            jax.ShapeDtypeStruct((batch,) + mat_tiles, F32),
            jax.ShapeDtypeStruct((batch, 1, WIDTH), F32),
            jax.ShapeDtypeStruct((batch, CONV_WIDTH - 1, WIDTH), F32),
            jax.ShapeDtypeStruct((batch,) + mat_tiles, F32),
            jax.ShapeDtypeStruct((batch, POOL_BUF, WIDTH), F32),
        ],
        scratch_shapes=[
            pltpu.VMEM(mat_tiles, F32), pltpu.VMEM(mat_tiles, F32), pltpu.VMEM((8, WIDTH), F32),
            pltpu.VMEM((8, WIDTH), F32), pltpu.VMEM((PAD + tc, WIDTH), F32),
            pltpu.VMEM((tc, WIDTH), F32), pltpu.VMEM((tc, WIDTH), F32), pltpu.VMEM((2, tc, 128), F32),
            pltpu.VMEM((tc, N_HEADS * CHUNK), F32),
        ],
        compiler_params=_cparams("parallel", "arbitrary"),
        name="mixer_prompt",
    )(u, lb_logits, vecs, wr_bd, wi_bd, wp_bd, cos_t, sin_t, ind, dsum, pair, causal, qdec, kdec, sdec, dmat)
    return outs


def _diag_blocks(s):
    per_tile = 128 // HEAD_DIM
    return jnp.stack([s[:, h // per_tile,
                        (h % per_tile) * HEAD_DIM:(h % per_tile + 1) * HEAD_DIM,
                        (h % per_tile) * HEAD_DIM:(h % per_tile + 1) * HEAD_DIM]
                      for h in range(N_HEADS)], axis=1)


def _sample_step_kernel(u_ref, lblt_ref, vec_ref, vect_ref, wr_ref, wi_ref, wp_ref, cost_ref, sint_ref,
                        sh_ref, sl_ref, scv_ref, sr_ref, spl_ref,
                        oa_ref, ob_ref, oc_ref, od_ref, nh_ref, nl_ref, ncv_ref, nr_ref, npl_ref,
                        fac_scr, ot_scr, *, layer, pos0):
    step = pl.program_id(0)
    F_, K_, Q_, V_, RQ_, RK_, RV_ = range(7)

    def slot_t(s):
        return u_ref[:, s * WIDTH:(s + 1) * WIDTH].T

    def vcol(r):
        return vect_ref[:, r:r + 1]

    @pl.when(step == 0)
    def _():
        lblt = lblt_ref[...]
        e = jnp.exp(lblt - jnp.max(lblt, axis=1, keepdims=True))
        soft = e / jnp.sum(e, axis=1, keepdims=True)
        acc = soft[:, 0:1]
        for l in range(1, layer + 1):
            acc = acc + soft[:, l:l + 1]
        lb = acc - soft[:, 0:1]
        sg = _sigmoid(slot_t(1))
        fac_scr[F_] = jnp.maximum(lb + (1.0 - lb) * sg, F_FLOOR)
        fac_scr[K_] = (1.0 - lb) * (1.0 - sg)
        fac_scr[Q_] = _silu(slot_t(0))
        fac_scr[V_] = slot_t(2)

        def swap_rows(x):
            half = HEAD_DIM // 2
            parts = []
            for h in range(N_HEADS):
                parts.append(x[h * HEAD_DIM + half:(h + 1) * HEAD_DIM])
                parts.append(x[h * HEAD_DIM:h * HEAD_DIM + half])
            return jnp.concatenate(parts, axis=0)

        cost, sint = cost_ref[...], sint_ref[...]
        cq, ck = slot_t(6), slot_t(7)
        fac_scr[RQ_] = cq * cost + swap_rows(cq) * sint
        fac_scr[RK_] = (ck * cost + swap_rows(ck) * sint) * (HEAD_DIM ** -0.5)
        fac_scr[RV_] = slot_t(8)

    @pl.when(step < N_HEADS)
    def _():
        base = pl.multiple_of(step * HEAD_DIM, HEAD_DIM)
        gam = [float(np.exp(g)) for g in _log_gamma()]
        gamma = jnp.float32(gam[-1])
        for h in range(N_HEADS - 2, -1, -1):
            gamma = jnp.where(step == h, jnp.float32(gam[h]), gamma)
        v_t = fac_scr[V_, pl.ds(base, HEAD_DIM), :]
        rv_t = fac_scr[RV_, pl.ds(base, HEAD_DIM), :]
        acc_a = jnp.zeros((HEAD_DIM, v_t.shape[1]), F32)
        acc_c = jnp.zeros((HEAD_DIM, v_t.shape[1]), F32)
        for k in range(HEAD_DIM):
            def row(which):
                return fac_scr[which, pl.ds(base + k, 1), :]
            s_new = row(F_) * sh_ref[k] + row(K_) * v_t
            nh_ref[k] = s_new
            acc_a = acc_a + row(Q_) * s_new
            r_new = gamma * sr_ref[k] + row(RK_) * rv_t
            nr_ref[k] = r_new
            acc_c = acc_c + row(RQ_) * r_new
        ot_scr[0, pl.ds(base, HEAD_DIM), :] = acc_a
        ot_scr[1, pl.ds(base, HEAD_DIM), :] = acc_c

    @pl.when(step == N_HEADS)
    def _():
        def vec(r):
            return vec_ref[r:r + 1, :]

        def slot(s):
            return u_ref[:, s * WIDTH:(s + 1) * WIDTH]

        def per_head(x, fn):
            return jnp.concatenate([fn(x[h * HEAD_DIM:(h + 1) * HEAD_DIM]) for h in range(N_HEADS)], axis=0)

        def rms_head(o):
            return o * lax.rsqrt(jnp.mean(o * o, axis=0, keepdims=True) + EPS)

        def group_head(o):
            dev = o - jnp.mean(o, axis=0, keepdims=True)
            return dev * lax.rsqrt(jnp.mean(dev * dev, axis=0, keepdims=True) + EPS)

        o_a = per_head(ot_scr[0], rms_head) * vcol(V_HNORM) * _silu(slot_t(3))
        oa_ref[...] = o_a.T
        o_c = per_head(ot_scr[1], group_head) * vcol(V_RNORM) * _silu(slot_t(9))
        oc_ref[...] = o_c.T

        ux = slot(4)
        xc = vec(V_CB) + vec(V_CW0 + CONV_WIDTH - 1) * ux
        for j in range(CONV_WIDTH - 1):
            xc = xc + vec(V_CW0 + j) * scv_ref[j]
        xc_bf = xc.astype(BF16)
        rg = _sigmoid(_dot(xc_bf, wr_ref[...]) + vec(V_BR))
        ig = _sigmoid(_dot(xc_bf, wi_ref[...]) + vec(V_BI))
        log_a = -LRU_C * rg * _softplus(-vec(V_LAM))
        b_in = jnp.sqrt(jnp.maximum(_neg_expm1(2.0 * log_a), 0.0)) * (ig * xc)
        hnew = jnp.exp(log_a) * sl_ref[...] + b_in
        nl_ref[...] = hnew
        ob_ref[...] = hnew * jax.nn.gelu(slot(5))
        for j in range(CONV_WIDTH - 2):
            ncv_ref[j] = scv_ref[j + 1]
        ncv_ref[CONV_WIDTH - 2] = ux

        ud = slot(10)
        wsum = ud
        sums = []
        nxt = 1
        for win in POOL_WINDOWS:
            while nxt < win:
                wsum = wsum + spl_ref[POOL_BUF - nxt]
                nxt += 1
            sums.append(wsum)
        lane = lax.broadcasted_iota(jnp.int32, (1, WIDTH), 1)
        grp = lane // (WIDTH // len(POOL_WINDOWS))
        sel = sums[-1]
        winl = jnp.full((1, WIDTH), float(POOL_WINDOWS[-1]), F32)
        for gi in range(len(POOL_WINDOWS) - 2, -1, -1):
            sel = jnp.where(grp == gi, sums[gi], sel)
            winl = jnp.where(grp == gi, float(POOL_WINDOWS[gi]), winl)
        pooled = sel / jnp.minimum(winl, float(pos0) + 1.0)
        od_ref[...] = _dot((pooled - ud).astype(BF16), wp_ref[...]) * vec(V_PSCALE)
        for j in range(POOL_BUF - 1):
            npl_ref[j] = spl_ref[j + 1]
        npl_ref[POOL_BUF - 1] = ud


def _sample_step(u, lb_logits, vecs, vecs_t, wr_bd, wi_bd, wp_bd, cos_t, sin_t, s_hgrn, s_lru, s_conv,
                 s_ret, s_pool, layer, *, pos0):
    nb = u.shape[0]
    depth = lb_logits.shape[0]
    cost = jnp.broadcast_to(cos_t.reshape(WIDTH, 1), (WIDTH, nb))
    sint = jnp.broadcast_to(sin_t.reshape(WIDTH, 1), (WIDTH, nb))

    def const(shape):
        return pl.BlockSpec(shape, lambda i: tuple(0 for _ in shape))

    def layer_block(shape):
        return pl.BlockSpec((None,) + shape, lambda i: (layer,) + tuple(0 for _ in shape))

    def head(i):
        return jnp.minimum(i, N_HEADS - 1)

    mat = (HEAD_DIM, HEAD_DIM, nb)
    mat_in = pl.BlockSpec((None, None) + mat, lambda i: (layer, head(i), 0, 0, 0))
    mat_out = pl.BlockSpec((None,) + mat, lambda i: (head(i), 0, 0, 0))
    rows = const((nb, WIDTH))
    return pl.pallas_call(
        functools.partial(_sample_step_kernel, layer=layer, pos0=pos0),
        grid=(N_HEADS + 1,),
        in_specs=[
            const((nb, MIX_COLS)), const((WIDTH, depth)),
            layer_block((N_VEC_ROWS, WIDTH)), layer_block((WIDTH, N_VEC_ROWS)),
            layer_block((WIDTH, WIDTH)), layer_block((WIDTH, WIDTH)), layer_block((WIDTH, WIDTH)),
            const((WIDTH, nb)), const((WIDTH, nb)),
            mat_in, layer_block((nb, WIDTH)), layer_block((CONV_WIDTH - 1, nb, WIDTH)), mat_in,
            layer_block((POOL_BUF, nb, WIDTH)),
        ],
        out_specs=[rows, rows, rows, rows, mat_out, rows, const((CONV_WIDTH - 1, nb, WIDTH)), mat_out,
                   const((POOL_BUF, nb, WIDTH))],
        out_shape=[jax.ShapeDtypeStruct((nb, WIDTH), F32)] * 4 + [
            jax.ShapeDtypeStruct((N_HEADS,) + mat, F32), jax.ShapeDtypeStruct((nb, WIDTH), F32),
            jax.ShapeDtypeStruct((CONV_WIDTH - 1, nb, WIDTH), F32), jax.ShapeDtypeStruct((N_HEADS,) + mat, F32),
            jax.ShapeDtypeStruct((POOL_BUF, nb, WIDTH), F32),
        ],
        scratch_shapes=[pltpu.VMEM((7, WIDTH, nb), F32), pltpu.VMEM((2, WIDTH, nb), F32)],
        compiler_params=_cparams("arbitrary"),
        name="sample_step",
    )(u, lb_logits.T, vecs, vecs_t, wr_bd, wi_bd, wp_bd, cost, sint, s_hgrn, s_lru, s_conv, s_ret, s_pool)


def _pick_tile(n, pref):
    t = min(n, pref)
    while n % t:
        t //= 2
    return t


def kernel(x_prompt, x_sample, state_hgrn, state_rglru, state_conv, state_retention, state_pool, lb_logits, ffn1_norm, ffn1_up, ffn1_down, mix_norm, w_in, hgrn_norm, conv_w, conv_b, w_rgate, b_rgate, w_igate, b_igate, lru_lambda, ret_norm, w_pool, pool_scale, w_branch, w_o, ffn2_norm, ffn2_up, ffn2_down, final_norm):
    batch, seq, d = x_prompt.shape
    nb, dec_seq, _ = x_sample.shape
    assert dec_seq == 1
    depth = w_in.shape[0]
    past_len = PAST_LEN

    w_in_bf = w_in.astype(BF16)
    w_br, w_out = w_branch.astype(BF16), w_o.astype(BF16)
    wr_bd = jax.vmap(_block_diag)(w_rgate).astype(BF16)
    wi_bd = jax.vmap(_block_diag)(w_igate).astype(BF16)
    wp_bd = jax.vmap(_block_diag)(w_pool).astype(BF16)
    vecs = jnp.concatenate(
        [hgrn_norm[:, None], conv_w, conv_b[:, None], b_rgate[:, None], b_igate[:, None],
         lru_lambda[:, None], ret_norm[:, None], pool_scale[:, None],
         jnp.zeros((depth, N_VEC_ROWS - 11, WIDTH), F32)], axis=1)
    vecs_t = jnp.swapaxes(vecs, 1, 2)
    hgrn_t = jnp.transpose(state_hgrn, (0, 2, 3, 4, 1))
    ret_t = jnp.transpose(state_retention, (0, 2, 3, 4, 1))
    conv_t = jnp.transpose(state_conv, (0, 2, 1, 3))
    pool_t = jnp.transpose(state_pool, (0, 2, 1, 3))
    n1 = ffn1_norm[:, None, :]
    n2 = ffn2_norm[:, None, :]
    nm = mix_norm[:, None, :]
    fg = final_norm[None, :]

    cos_p, sin_p = _rope_tables(jnp.arange(seq, dtype=F32))
    cos_s, sin_s = _rope_tables(past_len + jnp.arange(dec_seq, dtype=F32))

    xp = x_prompt.reshape(batch * seq, d)
    xs = x_sample.reshape(nb * dec_seq, d)
    tm_p = _pick_tile(batch * seq, DENSE_ROWS)
    tm_w = _pick_tile(batch * seq, WIDE_ROWS)
    tc = _pick_tile(seq, MIXER_ROWS)

    st_p = [[] for _ in range(5)]
    st_s = [[] for _ in range(5)]
    for l in range(depth):
        last = l == depth - 1
        xp, xs = _ffn(xp, xs, n1, ffn1_up, ffn1_down, l, fg, final_norm=False, tm=tm_p)
        u_p, u_s = _inproj(xp, xs, nm, w_in_bf, l, tm=tm_w)
        *br_p, sh, sl, scv, sr, spl = _mixer_prompt(
            u_p, lb_logits, vecs, wr_bd, wi_bd, wp_bd, cos_p, sin_p, l, batch=batch, seq=seq, tc=tc, pos0=0)
        *br_s, nh, nl, ncv, nr, npl = _sample_step(
            u_s, lb_logits, vecs, vecs_t, wr_bd, wi_bd, wp_bd, cos_s, sin_s, hgrn_t, state_rglru,
            conv_t, ret_t, pool_t, l, pos0=past_len)
        xp, xs = _merge(xp, xs, nm, br_p, br_s, w_in_bf, w_br, w_out, l, tm=tm_w)
        xp, xs = _ffn(xp, xs, n2, ffn2_up, ffn2_down, l, fg, final_norm=last, tm=tm_p)
        for lst, s in zip(st_p, (jnp.swapaxes(_diag_blocks(sh), 2, 3), sl[:, 0], scv, _diag_blocks(sr), spl)):
            lst.append(s)
        for lst, s in zip(st_s, (nh, nl, ncv, nr, npl)):
            lst.append(s)

    y_p = xp.reshape(batch, seq, d)
    y_s = xs.reshape(nb, dec_seq, d)
    nh, nl, ncv, nr, npl = (jnp.stack(s) for s in st_s)
    sample_states = (jnp.transpose(nh, (0, 4, 1, 2, 3)), nl, jnp.transpose(ncv, (0, 2, 1, 3)),
                     jnp.transpose(nr, (0, 4, 1, 2, 3)), jnp.transpose(npl, (0, 2, 1, 3)))
    return (y_p, y_s) + tuple(jnp.stack(s) for s in st_p) + sample_states
```

```python
import functools
from typing import NamedTuple

import numpy as np
import jax
import jax.numpy as jnp
from jax import lax
from jax.experimental import pallas as pl
from jax.experimental.pallas import tpu as pltpu

F32 = jnp.float32
BF16 = jnp.bfloat16

N_HEADS = 4
HEAD_DIM = 64
WIDTH = N_HEADS * HEAD_DIM
N_MIX_SLOTS = 11
MIX_COLS = N_MIX_SLOTS * WIDTH
CHUNK = 64
N_LEVELS = 6
PAD = 16
CONV_WIDTH = 4
LRU_C = 8.0
POOL_WINDOWS = (2, 4, 8, 16)
POOL_BUF = 15
ROPE_BASE = 10000.0
EPS = 1e-6
F_FLOOR = 1e-30
SAFE_LOG = 80.0
PAST_LEN = 16384
VMEM_LIMIT = 56 * 1024 * 1024
DENSE_ROWS = 512
WIDE_ROWS = 1024
WEIGHT_CHUNK = 512
MIXER_ROWS = 512


def _cparams(*sem):
    return pltpu.CompilerParams(dimension_semantics=sem, vmem_limit_bytes=VMEM_LIMIT)


def _dot(a, b):
    return jnp.dot(a, b, preferred_element_type=F32)


def _dot_nt(a, b):
    return lax.dot_general(a, b, (((1,), (1,)), ((), ())), preferred_element_type=F32)


def _dot_tn(a, b):
    return lax.dot_general(a, b, (((0,), (0,)), ((), ())), preferred_element_type=F32)


def _rms(x, g):
    return x * lax.rsqrt(jnp.mean(x * x, axis=-1, keepdims=True) + EPS) * g


def _sigmoid(x):
    return 0.5 * jnp.tanh(0.5 * x) + 0.5


def _silu(x):
    hx = 0.5 * x
    return hx * jnp.tanh(hx) + hx


def _softplus(x):
    return jnp.maximum(x, 0.0) + jnp.log1p(jnp.exp(-jnp.abs(x)))


def _neg_expm1(y):
    t = jnp.tanh(0.5 * y)
    return -2.0 * t / (1.0 - t)


def _split2(x):
    hi = x.astype(BF16)
    lo = (x - hi.astype(F32)).astype(BF16)
    return hi, lo


def _split3(x):
    hi = x.astype(BF16)
    r = x - hi.astype(F32)
    mid = r.astype(BF16)
    lo = (r - mid.astype(F32)).astype(BF16)
    return hi, mid, lo


def _lower_bound(lbl, layer):
    m = jnp.max(lbl, axis=0, keepdims=True)
    e = jnp.exp(lbl - m)
    soft = e / jnp.sum(e, axis=0, keepdims=True)
    acc = soft[0:1]
    for l in range(1, layer + 1):
        acc = acc + soft[l:l + 1]
    return acc - soft[0:1]


def _resident(shape, index_map):
    return pl.BlockSpec(shape, index_map, pipeline_mode=pl.Buffered(1))


class _Weight(NamedTuple):
    src: jax.Array
    index: tuple
    row0: int
    rows: int
    col0: int
    cols: int
    chunk: int
    by_rows: bool

    def piece(self, ref, c, in_src):
        if in_src:
            ref = ref.at[self.index] if self.index else ref
        r0, c0 = (self.row0, self.col0) if in_src else (0, 0)
        if self.by_rows:
            return ref.at[pl.ds(r0 + c * self.chunk, self.chunk), pl.ds(c0, self.cols)]
        return ref.at[pl.ds(r0, self.rows), pl.ds(c0 + c * self.chunk, self.chunk)]

    @property
    def n_chunks(self):
        total = self.rows if self.by_rows else self.cols
        assert total % self.chunk == 0
        return total // self.chunk

    @property
    def stage_shape(self):
        return (2, self.chunk, self.cols) if self.by_rows else (2, self.rows, self.chunk)


def _cast_weight(w, src_ref, dst_ref, stage, sem):
    def copy(c):
        return pltpu.make_async_copy(w.piece(src_ref, c, True), stage.at[c % 2], sem.at[c % 2])

    copy(0).start()
    for c in range(w.n_chunks):
        if c + 1 < w.n_chunks:
            copy(c + 1).start()
        copy(c).wait()
        w.piece(dst_ref, c, False)[...] = stage[c % 2].astype(BF16)


def _two_group_kernel(*refs, body, n_rows, n_params, weights, n_tiles):
    nw = len(weights)
    p_refs, s_refs = refs[:n_rows], refs[n_rows:2 * n_rows]
    params = refs[2 * n_rows:2 * n_rows + n_params]
    hbm = refs[2 * n_rows + n_params:2 * n_rows + n_params + nw]
    op_ref, os_ref = refs[2 * n_rows + n_params + nw:2 * n_rows + n_params + nw + 2]
    scratch = refs[2 * n_rows + n_params + nw + 2:]
    copies, stages, sems = scratch[0::3], scratch[1::3], scratch[2::3]
    i = pl.program_id(0)

    @pl.when(i == 0)
    def _():
        for w, src, dst, stage, sem in zip(weights, hbm, copies, stages, sems):
            _cast_weight(w, src, dst, stage, sem)

    @pl.when(i < n_tiles)
    def _():
        op_ref[...] = body([r[...] for r in p_refs], tuple(params) + tuple(copies))

    @pl.when(i == n_tiles)
    def _():
        os_ref[...] = body([r[...] for r in s_refs], tuple(params) + tuple(copies))


def _two_group_call(body, rows_p, rows_s, params, param_specs, weights, out_cols, tm, name):
    n_p, n_s = rows_p[0].shape[0], rows_s[0].shape[0]
    n_tiles = n_p // tm

    def tile(i):
        return (jnp.minimum(i, n_tiles - 1), 0)

    scratch = []
    for w in weights:
        scratch += [pltpu.VMEM((w.rows, w.cols), BF16), pltpu.VMEM(w.stage_shape, F32),
                    pltpu.SemaphoreType.DMA((2,))]
    return pl.pallas_call(
        functools.partial(_two_group_kernel, body=body, n_rows=len(rows_p), n_params=len(params),
                          weights=tuple(w._replace(src=None) for w in weights), n_tiles=n_tiles),
        grid=(n_tiles + 1,),
        in_specs=[pl.BlockSpec((tm, a.shape[1]), tile) for a in rows_p]
        + [pl.BlockSpec(a.shape, lambda i: (0, 0)) for a in rows_s] + list(param_specs)
        + [pl.BlockSpec(memory_space=pl.ANY) for _ in weights],
        out_specs=[pl.BlockSpec((tm, out_cols), tile), pl.BlockSpec((n_s, out_cols), lambda i: (0, 0))],
        out_shape=[jax.ShapeDtypeStruct((n_p, out_cols), F32), jax.ShapeDtypeStruct((n_s, out_cols), F32)],
        scratch_shapes=scratch,
        compiler_params=_cparams("arbitrary"),
        name=name,
    )(*rows_p, *rows_s, *params, *[w.src for w in weights])


def _ffn_body(rows, params, *, final_norm):
    (x,), (g_ref, fg_ref, wu_ref, wd_ref) = rows, params
    dff = wd_ref.shape[0]
    h = _rms(x, g_ref[...]).astype(BF16)
    a = _dot(h, wu_ref[:, :dff])
    b = _dot(h, wu_ref[:, dff:])
    act = (_silu(a) * b).astype(BF16)
    y = x + 0.5 * _dot(act, wd_ref[...])
    return _rms(y, fg_ref[...]) if final_norm else y


def _ffn(xp, xs, g, w_up, w_down, layer, final_g, *, final_norm, tm):
    d = xp.shape[1]
    dff = w_down.shape[1]
    specs = [_resident((None, 1, d), lambda i: (layer, 0, 0)), _resident((1, d), lambda i: (0, 0))]
    weights = [_Weight(w_up, (layer,), 0, d, 0, 2 * dff, WEIGHT_CHUNK, by_rows=False),
               _Weight(w_down, (layer,), 0, dff, 0, d, WEIGHT_CHUNK // 2, by_rows=True)]
    return _two_group_call(functools.partial(_ffn_body, final_norm=final_norm), [xp], [xs],
                           (g, final_g), specs, weights, d, tm, "ffn")


def _inproj_body(rows, params):
    (x,), (g_ref, w_ref) = rows, params
    return _dot(_rms(x, g_ref[...]).astype(BF16), w_ref[...])


def _inproj(xp, xs, g, w_in, layer, *, tm):
    d = xp.shape[1]
    specs = [_resident((None, 1, d), lambda i: (layer, 0, 0))]
    weights = [_Weight(w_in, (layer,), 0, d, 0, MIX_COLS, WEIGHT_CHUNK // 2, by_rows=False)]
    return _two_group_call(_inproj_body, [xp], [xs], (g,), specs, weights, MIX_COLS, tm, "inproj")


def _merge_body(rows, params):
    x, branches = rows[0], rows[1:]
    g_ref, wg_ref, wb_ref, wo_ref = params
    d = x.shape[1]
    w = wb_ref.shape[0] // len(branches)
    h = _rms(x, g_ref[...]).astype(BF16)
    merged = None
    for b, o_b in enumerate(branches):
        gate = _sigmoid(_dot(h, wg_ref[:, b * d:(b + 1) * d]))
        y = _dot(o_b.astype(BF16), wb_ref[b * w:(b + 1) * w, :])
        merged = gate * y if merged is None else merged + gate * y
    return x + _dot(merged.astype(BF16), wo_ref[...])


def _merge(xp, xs, g, branches_p, branches_s, w_in, w_branch, w_o, layer, *, tm):
    d = xp.shape[1]
    depth, nb, w, _ = w_branch.shape
    assert w_in.shape[2] == MIX_COLS + nb * d and MIX_COLS % 128 == 0
    specs = [_resident((None, 1, d), lambda i: (layer, 0, 0))]
    weights = [_Weight(w_in, (layer,), 0, d, MIX_COLS, nb * d, WEIGHT_CHUNK // 2, by_rows=False),
               _Weight(w_branch.reshape(depth, nb * w, d), (layer,), 0, nb * w, 0, d, WEIGHT_CHUNK // 4, by_rows=True),
               _Weight(w_o, (layer,), 0, d, 0, d, WEIGHT_CHUNK // 4, by_rows=True)]
    return _two_group_call(_merge_body, [xp, *branches_p], [xs, *branches_s],
                           (g,), specs, weights, d, tm, "merge")


def _head_block_ones():
    h = np.arange(WIDTH) // HEAD_DIM
    return (h[:, None] == h[None, :]).astype(np.float32)


def _log_gamma():
    return np.log1p(-(2.0 ** (-5.0 - np.arange(N_HEADS, dtype=np.float64))))


def _retention_tables():
    lg = np.repeat(_log_gamma(), HEAD_DIM)[None, :]
    t = np.arange(CHUNK, dtype=np.float64)[:, None]
    q_dec = np.exp((t + 1.0) * lg)
    k_dec = np.exp((CHUNK - 1.0 - t) * lg)
    s_dec = np.exp(CHUNK * lg)
    dt = t.T - t
    dmat = np.concatenate(
        [np.where(dt >= 0, np.exp(dt * g), 0.0) for g in _log_gamma()], axis=1)
    return (jnp.asarray(q_dec, F32), jnp.asarray(k_dec, F32), jnp.asarray(s_dec, F32),
            jnp.asarray(dmat, F32))


def _level_tables():
    t = np.arange(CHUNK)[:, None]
    r = np.arange(CHUNK)[None, :]
    blocks = [r <= t, r > t]
    pairs = []
    for i in range(N_LEVELS):
        m = 1 << i
        mid = t - (t % (2 * m)) + m
        up = (t % (2 * m)) >= m
        blocks.append(np.where(up, (r >= mid) & (r <= t), (r > t) & (r <= mid - 1)))
        same = (t // (2 * m)) == (r // (2 * m))
        pr = same & ((t % (2 * m)) < m) & ((r % (2 * m)) >= m)
        pairs.append(np.tile(pr, (1, N_HEADS)))
    dsum = np.tile(np.concatenate(blocks, axis=0), (1, 3))
    causal = np.tile(t < r, (1, N_HEADS))
    assert (np.sum(np.stack(pairs), axis=0) == causal).all()
    return jnp.asarray(dsum, BF16), jnp.asarray(np.stack(pairs), F32), jnp.asarray(causal, F32)


def _rope_tables(pos):
    half = HEAD_DIM // 2
    freq = ROPE_BASE ** (-jnp.arange(half, dtype=F32) / half)
    ang = pos[:, None] * freq[None, :]
    cos, sin = jnp.cos(ang), jnp.sin(ang)
    cos_h = jnp.concatenate([cos, cos], axis=-1)
    sin_h = jnp.concatenate([-sin, sin], axis=-1)
    return jnp.tile(cos_h, (1, N_HEADS)), jnp.tile(sin_h, (1, N_HEADS))


def _block_diag(w):
    h, dh, _ = w.shape
    eye = jnp.eye(h, dtype=w.dtype)
    return (eye[:, None, :, None] * w[:, :, None, :]).reshape(h * dh, h * dh)


V_HNORM, V_CW0, V_CB, V_BR, V_BI, V_LAM, V_RNORM, V_PSCALE = 0, 1, 5, 6, 7, 8, 9, 10
N_VEC_ROWS = 16


def _swap_halves(x):
    half = HEAD_DIM // 2
    lane = lax.broadcasted_iota(jnp.int32, (1, 128), 1)
    first = (lane % HEAD_DIM) < half
    parts = []
    for c in range(x.shape[1] // 128):
        xc = x[:, c * 128:(c + 1) * 128]
        parts.append(jnp.where(first, pltpu.roll(xc, 128 - half, 1), pltpu.roll(xc, half, 1)))
    return jnp.concatenate(parts, axis=1)


def _mixer_prompt_kernel(u_ref, lbl_ref, vec_ref, wr_ref, wi_ref, wp_ref, cos_ref, sin_ref,
                         ind_ref, dsum_ref, pair_ref, causal_ref, qdec_ref, kdec_ref, sdec_ref, dmat_ref,
                         oa_ref, ob_ref, oc_ref, od_ref, sh_ref, sl_ref, scv_ref, sr_ref, spl_ref,
                         hst_scr, rst_scr, hl_scr, xcv_scr, xpl_scr, oh_scr, orr_scr, perm_scr, sct_scr,
                         *, layer, tc, pos0):
    t_idx = pl.program_id(1)
    nchunk = tc // CHUNK

    @pl.when(t_idx == 0)
    def _():
        hst_scr[...] = jnp.zeros_like(hst_scr)
        rst_scr[...] = jnp.zeros_like(rst_scr)
        hl_scr[...] = jnp.zeros_like(hl_scr)
        xcv_scr[...] = jnp.zeros_like(xcv_scr)
        xpl_scr[0:PAD, :] = jnp.zeros((PAD, WIDTH), F32)

    def slot(s):
        return u_ref[:, s * WIDTH:(s + 1) * WIDTH]

    def vec(r):
        return vec_ref[r:r + 1, :]

    ind = ind_ref[...]
    lane = lax.broadcasted_iota(jnp.int32, (1, WIDTH), 1)
    hm_bf = [(lane // HEAD_DIM == h).astype(F32).astype(BF16) for h in range(N_HEADS)]
    lane_t = lax.broadcasted_iota(jnp.int32, (1, 128), 1)
    row = lax.broadcasted_iota(jnp.int32, (tc, WIDTH), 0)
    chunks = [slice(n * CHUNK, (n + 1) * CHUNK) for n in range(nchunk)]

    def head_scores(keys, queries):
        stacked = jnp.concatenate([queries * hm_bf[h] for h in range(N_HEADS)], axis=0)
        return _dot_nt(keys, stacked)

    def scores_times_values(sct, val):
        assert HEAD_DIM == 64 and WIDTH % 128 == 0
        tiles = []
        for c in range(WIDTH // 128):
            pair = _dot_tn(sct[:, 2 * c * CHUNK:(2 * c + 2) * CHUNK], val[:, c * 128:(c + 1) * 128])
            tiles.append(jnp.where(lane_t < HEAD_DIM, pair[0:CHUNK], pair[CHUNK:2 * CHUNK]))
        return jnp.concatenate(tiles, axis=1)

    n_lt = WIDTH // 128
    ind_t = ind[0:128, 0:128]

    def lane_tile(x, c):
        return x[:, c * 128:(c + 1) * 128]

    def tile_outer(a, b):
        return [_dot_tn(lane_tile(a, c), lane_tile(b, c)) for c in range(n_lt)]

    ind2 = jnp.concatenate([ind, ind], axis=0)

    def head_sum(x):
        return _dot(jnp.concatenate(_split2(x), axis=1), ind2)

    lb = _lower_bound(lbl_ref[...], layer)
    uq, z, v = slot(0), slot(1), slot(2)
    q = _silu(uq)
    sg = _sigmoid(z)
    f = lb + (1.0 - lb) * sg
    logf = jnp.log(jnp.maximum(f, F_FLOOR))
    k = (1.0 - lb) * (1.0 - sg)
    l_hi, l_mid, l_lo = _split3(logf)
    v_bf = v.astype(BF16)
    dsum = dsum_ref[...]
    rowc = lax.broadcasted_iota(jnp.int32, (CHUNK, WIDTH), 0)
    upper = [((rowc >> i) & 1) == 1 for i in range(N_LEVELS)]
    lsplit = [jnp.concatenate([l_hi[rs], l_mid[rs], l_lo[rs]], axis=0) for rs in chunks]
    logs = [_dot(dsum[0:2 * CHUNK], lsplit[n]) for n in range(nchunk)]
    dec = [jnp.exp(jnp.minimum(lg, 0.0)) for lg in logs]
    qe = [(q[rs] * dec[n][0:CHUNK]).astype(BF16) for n, rs in enumerate(chunks)]
    total_decay = logs[0][CHUNK - 1:CHUNK]
    for lg in logs[1:]:
        total_decay = jnp.minimum(total_decay, lg[CHUNK - 1:CHUNK])
    bounded = jnp.min(total_decay) > -SAFE_LOG

    @pl.when(bounded)
    def _():
        for n, rs in enumerate(chunks):
            k_up = (k[rs] * jnp.exp(-logs[n][0:CHUNK])).astype(BF16)
            sct_scr[rs, :] = head_scores(k_up, qe[n]) * causal_ref[...]

    @pl.when(jnp.logical_not(bounded))
    def _():
        sct = [None] * nchunk
        for n, rs in enumerate(chunks):
            lev = jnp.exp(jnp.minimum(_dot(dsum[2 * CHUNK:], lsplit[n]), 0.0))
            for i in range(N_LEVELS):
                zz = (jnp.where(upper[i], q[rs], k[rs]) * lev[i * CHUNK:(i + 1) * CHUNK]).astype(BF16)
                term = head_scores(zz, zz) * pair_ref[i]
                sct[n] = term if sct[n] is None else sct[n] + term
            sct_scr[rs, :] = sct[n]

    o_diag = _dot((q * k).astype(BF16), ind) * v
    upd = [tile_outer(v_bf[rs], (k[rs] * dec[n][CHUNK:2 * CHUNK]).astype(BF16)) for n, rs in enumerate(chunks)]
    o_intra = [scores_times_values(sct_scr[rs, :].astype(BF16), v_bf[rs]) for rs in chunks]

    seg_len = tc // 8
    def strided_rows(ref, start, stride):
        return jnp.concatenate([ref[c, pl.ds(start, 8, stride=stride), :] for c in range(2)], axis=1)

    for c in range(2):
        perm_scr[c] = u_ref[:, 4 * WIDTH + c * 128:4 * WIDTH + (c + 1) * 128]
    slabs = [strided_rows(perm_scr, i, seg_len) for i in range(seg_len)]
    sub0 = lax.broadcasted_iota(jnp.int32, (8, WIDTH), 0) == 0
    front = [jnp.where(sub0, xcv_scr[8 + k:9 + k, :], pltpu.roll(slabs[seg_len + k], 1, 0))
             for k in range(-(CONV_WIDTH - 1), 0)]
    ext = front + slabs
    cw = [vec(V_CW0 + j) for j in range(CONV_WIDTH)]
    xc = jnp.concatenate(
        [vec(V_CB) + sum(cw[j] * ext[i + j] for j in range(CONV_WIDTH)) for i in range(seg_len)], axis=0)
    xc_bf = xc.astype(BF16)
    rg = _sigmoid(_dot(xc_bf, wr_ref[...]) + vec(V_BR))
    ig = _sigmoid(_dot(xc_bf, wi_ref[...]) + vec(V_BI))
    log_a = -LRU_C * rg * _softplus(-vec(V_LAM))
    a_all = jnp.exp(log_a)
    b_all = jnp.sqrt(jnp.maximum(_neg_expm1(2.0 * log_a), 0.0)) * (ig * xc)
    h_loc, a_loc = [b_all[0:8]], [a_all[0:8]]
    for i in range(1, seg_len):
        a_i = a_all[8 * i:8 * i + 8]
        h_loc.append(a_i * h_loc[-1] + b_all[8 * i:8 * i + 8])
        a_loc.append(a_i * a_loc[-1])
    carry = [hl_scr[0:1, :]]
    for s in range(8):
        carry.append(a_loc[-1][s:s + 1] * carry[-1] + h_loc[-1][s:s + 1])
    hl_scr[0:1, :] = carry[8]
    carry8 = jnp.concatenate(carry[0:8], axis=0)
    hperm = jnp.concatenate([h_loc[i] + a_loc[i] * carry8 for i in range(seg_len)], axis=0)
    for c in range(2):
        perm_scr[c] = hperm[:, c * 128:(c + 1) * 128]
    hseq = jnp.concatenate(
        [strided_rows(perm_scr, ((8 * r) % seg_len) * 8 + (8 * r) // seg_len, 8) for r in range(seg_len)],
        axis=0)
    ob_ref[...] = (hseq * jax.nn.gelu(slot(5))).astype(ob_ref.dtype)
    xcv_scr[...] = u_ref[tc - 8:tc, 4 * WIDTH:5 * WIDTH]

    cosv, sinv = cos_ref[...], sin_ref[...]
    cq, ck = slot(6), slot(7)
    qr = cq * cosv + _swap_halves(cq) * sinv
    kr = (ck * cosv + _swap_halves(ck) * sinv) * (HEAD_DIM ** -0.5)
    rv_bf = slot(8).astype(BF16)
    qdec, kdec, sdec, dmat = qdec_ref[...], kdec_ref[...], sdec_ref[...], dmat_ref[...]
    qr_bf, kr_bf = qr.astype(BF16), kr.astype(BF16)
    rsc = []
    for rs in chunks:
        rsc.append((head_scores(kr_bf[rs], qr_bf[rs]) * dmat).astype(BF16))
    r_intra = [scores_times_values(rsc[n], rv_bf[rs]) for n, rs in enumerate(chunks)]
    rupd = [tile_outer((kr[rs] * kdec).astype(BF16), rv_bf[rs]) for rs in chunks]

    ud = slot(10)
    xpl_scr[PAD:PAD + tc, :] = ud
    assert POOL_WINDOWS == (2, 4, 8, 16) and WIDTH // len(POOL_WINDOWS) == 64 and PAD >= POOL_BUF
    halves = []
    for c in range(2):
        acc = xpl_scr[:, c * 128:(c + 1) * 128]
        built = {}
        for sh in (1, 2, 4, 8):
            acc = acc + pltpu.roll(acc, sh, 0)
            built[2 * sh] = acc
        lo, hi = POOL_WINDOWS[2 * c], POOL_WINDOWS[2 * c + 1]
        halves.append(jnp.where(lane_t < 64, built[lo], built[hi])[PAD:PAD + tc])
    sel = jnp.concatenate(halves, axis=1)
    grp = lane // (WIDTH // len(POOL_WINDOWS))
    winl = jnp.full((1, WIDTH), float(POOL_WINDOWS[-1]), F32)
    for gi in range(len(POOL_WINDOWS) - 2, -1, -1):
        winl = jnp.where(grp == gi, float(POOL_WINDOWS[gi]), winl)
    pos = (row + t_idx * tc).astype(F32) + float(pos0)
    pooled = sel / jnp.minimum(winl, pos + 1.0)
    od = _dot((pooled - ud).astype(BF16), wp_ref[...]) * vec(V_PSCALE)
    od_ref[...] = od.astype(od_ref.dtype)
    xpl_scr[0:PAD, :] = xpl_scr[tc:tc + PAD, :]

    st = [hst_scr[c] for c in range(n_lt)]
    s_ret = [rst_scr[c] for c in range(n_lt)]
    for n, rs in enumerate(chunks):
        o_c = jnp.concatenate(
            [_dot_nt(lane_tile(qe[n], c), st[c].astype(BF16) * ind_t) for c in range(n_lt)], axis=1)
        st = [st[c] * lane_tile(dec[n][CHUNK - 1:CHUNK], c) + upd[n][c] for c in range(n_lt)]
        oh_scr[rs, :] = o_c + o_diag[rs] + o_intra[n]
        rqe = (qr[rs] * qdec).astype(BF16)
        o_c = jnp.concatenate(
            [_dot(lane_tile(rqe, c), s_ret[c].astype(BF16) * ind_t) for c in range(n_lt)], axis=1)
        s_ret = [s_ret[c] * lane_tile(sdec, c) + rupd[n][c] for c in range(n_lt)]
        orr_scr[rs, :] = o_c + r_intra[n]
    for c in range(n_lt):
        hst_scr[c] = st[c]
        rst_scr[c] = s_ret[c]
    o_h = oh_scr[...]
    ms = head_sum(o_h * o_h) * (1.0 / HEAD_DIM)
    oa_ref[...] = (o_h * lax.rsqrt(ms + EPS) * vec(V_HNORM) * _silu(slot(3))).astype(oa_ref.dtype)
    o_r = orr_scr[...]
    mu = head_sum(o_r) * (1.0 / HEAD_DIM)
    dev = o_r - mu
    var = head_sum(dev * dev) * (1.0 / HEAD_DIM)
    oc_ref[...] = (dev * lax.rsqrt(var + EPS) * vec(V_RNORM) * _silu(slot(9))).astype(oc_ref.dtype)

    @pl.when(t_idx == pl.num_programs(1) - 1)
    def _():
        sh_ref[...] = hst_scr[...]
        sr_ref[...] = rst_scr[...]
        sl_ref[...] = hl_scr[0:1, :]
        scv_ref[...] = xcv_scr[8 - (CONV_WIDTH - 1):8, :]
        spl_ref[...] = xpl_scr[PAD + tc - POOL_BUF:PAD + tc, :]


def _mixer_prompt(u, lb_logits, vecs, wr_bd, wi_bd, wp_bd, cos_t, sin_t, layer, *, batch, seq, tc, pos0):
    nt = seq // tc
    ind = jnp.asarray(_head_block_ones(), BF16)
    dsum, pair, causal = _level_tables()
    qdec, kdec, sdec, dmat = _retention_tables()
    depth = lb_logits.shape[0]

    def const(shape):
        return pl.BlockSpec(shape, lambda b, t: tuple(0 for _ in shape))

    row_spec = pl.BlockSpec((tc, WIDTH), lambda b, t: (b * nt + t, 0))
    tab_spec = pl.BlockSpec((tc, WIDTH), lambda b, t: (t, 0))

    def state_spec(rows):
        return pl.BlockSpec((None, rows, WIDTH), lambda b, t: (b, 0, 0))

    mat_tiles = (WIDTH // 128, 128, 128)
    mat_spec = pl.BlockSpec((None,) + mat_tiles, lambda b, t: (b, 0, 0, 0))
    n = batch * seq
    outs = pl.pallas_call(
        functools.partial(_mixer_prompt_kernel, layer=layer, tc=tc, pos0=pos0),
        grid=(batch, nt),
        in_specs=[
            pl.BlockSpec((tc, MIX_COLS), lambda b, t: (b * nt + t, 0)),
            const((depth, WIDTH)),
            pl.BlockSpec((None, N_VEC_ROWS, WIDTH), lambda b, t: (layer, 0, 0)),
            pl.BlockSpec((None, WIDTH, WIDTH), lambda b, t: (layer, 0, 0)),
            pl.BlockSpec((None, WIDTH, WIDTH), lambda b, t: (layer, 0, 0)),
            pl.BlockSpec((None, WIDTH, WIDTH), lambda b, t: (layer, 0, 0)),
            tab_spec, tab_spec,
            const((WIDTH, WIDTH)), const(dsum.shape), const(pair.shape), const(causal.shape),
            const((CHUNK, WIDTH)), const((CHUNK, WIDTH)), const((1, WIDTH)),
            const((CHUNK, N_HEADS * CHUNK)),
        ],
        out_specs=[row_spec, row_spec, row_spec, row_spec,
                   mat_spec, state_spec(1), state_spec(CONV_WIDTH - 1), mat_spec,
                   state_spec(POOL_BUF)],
        out_shape=[jax.ShapeDtypeStruct((n, WIDTH), BF16)] * 4 + [
            jax.ShapeDtypeStruct((batch,) + mat_tiles, F32),
            jax.ShapeDtypeStruct((batch, 1, WIDTH), F32),
            jax.ShapeDtypeStruct((batch, CONV_WIDTH - 1, WIDTH), F32),
            jax.ShapeDtypeStruct((batch,) + mat_tiles, F32),
            jax.ShapeDtypeStruct((batch, POOL_BUF, WIDTH), F32),
        ],
        scratch_shapes=[
            pltpu.VMEM(mat_tiles, F32), pltpu.VMEM(mat_tiles, F32), pltpu.VMEM((8, WIDTH), F32),
            pltpu.VMEM((8, WIDTH), F32), pltpu.VMEM((PAD + tc, WIDTH), F32),
            pltpu.VMEM((tc, WIDTH), F32), pltpu.VMEM((tc, WIDTH), F32), pltpu.VMEM((2, tc, 128), F32),
            pltpu.VMEM((tc, N_HEADS * CHUNK), F32),
        ],
        compiler_params=_cparams("parallel", "arbitrary"),
        name="mixer_prompt",
    )(u, lb_logits, vecs, wr_bd, wi_bd, wp_bd, cos_t, sin_t, ind, dsum, pair, causal, qdec, kdec, sdec, dmat)
    return outs


def _diag_blocks(s):
    per_tile = 128 // HEAD_DIM
    return jnp.stack([s[:, h // per_tile,
                        (h % per_tile) * HEAD_DIM:(h % per_tile + 1) * HEAD_DIM,
                        (h % per_tile) * HEAD_DIM:(h % per_tile + 1) * HEAD_DIM]
                      for h in range(N_HEADS)], axis=1)


def _sample_step_kernel(u_ref, lblt_ref, vec_ref, vect_ref, wr_ref, wi_ref, wp_ref, cost_ref, sint_ref,
                        sh_ref, sl_ref, scv_ref, sr_ref, spl_ref,
                        oa_ref, ob_ref, oc_ref, od_ref, nh_ref, nl_ref, ncv_ref, nr_ref, npl_ref,
                        fac_scr, ot_scr, *, layer, pos0):
    step = pl.program_id(0)
    F_, K_, Q_, V_, RQ_, RK_, RV_ = range(7)

    def slot_t(s):
        return u_ref[:, s * WIDTH:(s + 1) * WIDTH].T

    def vcol(r):
        return vect_ref[:, r:r + 1]

    @pl.when(step == 0)
    def _():
        lblt = lblt_ref[...]
        e = jnp.exp(lblt - jnp.max(lblt, axis=1, keepdims=True))
        soft = e / jnp.sum(e, axis=1, keepdims=True)
        acc = soft[:, 0:1]
        for l in range(1, layer + 1):
            acc = acc + soft[:, l:l + 1]
        lb = acc - soft[:, 0:1]
        sg = _sigmoid(slot_t(1))
        fac_scr[F_] = jnp.maximum(lb + (1.0 - lb) * sg, F_FLOOR)
        fac_scr[K_] = (1.0 - lb) * (1.0 - sg)
        fac_scr[Q_] = _silu(slot_t(0))
        fac_scr[V_] = slot_t(2)

        def swap_rows(x):
            half = HEAD_DIM // 2
            parts = []
            for h in range(N_HEADS):
                parts.append(x[h * HEAD_DIM + half:(h + 1) * HEAD_DIM])
                parts.append(x[h * HEAD_DIM:h * HEAD_DIM + half])
            return jnp.concatenate(parts, axis=0)

        cost, sint = cost_ref[...], sint_ref[...]
        cq, ck = slot_t(6), slot_t(7)
        fac_scr[RQ_] = cq * cost + swap_rows(cq) * sint
        fac_scr[RK_] = (ck * cost + swap_rows(ck) * sint) * (HEAD_DIM ** -0.5)
        fac_scr[RV_] = slot_t(8)

    @pl.when(step < N_HEADS)
    def _():
        base = pl.multiple_of(step * HEAD_DIM, HEAD_DIM)
        gam = [float(np.exp(g)) for g in _log_gamma()]
        gamma = jnp.float32(gam[-1])
        for h in range(N_HEADS - 2, -1, -1):
            gamma = jnp.where(step == h, jnp.float32(gam[h]), gamma)
        v_t = fac_scr[V_, pl.ds(base, HEAD_DIM), :]
        rv_t = fac_scr[RV_, pl.ds(base, HEAD_DIM), :]
        acc_a = jnp.zeros((HEAD_DIM, v_t.shape[1]), F32)
        acc_c = jnp.zeros((HEAD_DIM, v_t.shape[1]), F32)
        for k in range(HEAD_DIM):
            def row(which):
                return fac_scr[which, pl.ds(base + k, 1), :]
            s_new = row(F_) * sh_ref[k] + row(K_) * v_t
            nh_ref[k] = s_new
            acc_a = acc_a + row(Q_) * s_new
            r_new = gamma * sr_ref[k] + row(RK_) * rv_t
            nr_ref[k] = r_new
            acc_c = acc_c + row(RQ_) * r_new
        ot_scr[0, pl.ds(base, HEAD_DIM), :] = acc_a
        ot_scr[1, pl.ds(base, HEAD_DIM), :] = acc_c

    @pl.when(step == N_HEADS)
    def _():
        def vec(r):
            return vec_ref[r:r + 1, :]

        def slot(s):
            return u_ref[:, s * WIDTH:(s + 1) * WIDTH]

        def per_head(x, fn):
            return jnp.concatenate([fn(x[h * HEAD_DIM:(h + 1) * HEAD_DIM]) for h in range(N_HEADS)], axis=0)

        def rms_head(o):
            return o * lax.rsqrt(jnp.mean(o * o, axis=0, keepdims=True) + EPS)

        def group_head(o):
            dev = o - jnp.mean(o, axis=0, keepdims=True)
            return dev * lax.rsqrt(jnp.mean(dev * dev, axis=0, keepdims=True) + EPS)

        o_a = per_head(ot_scr[0], rms_head) * vcol(V_HNORM) * _silu(slot_t(3))
        oa_ref[...] = o_a.T
        o_c = per_head(ot_scr[1], group_head) * vcol(V_RNORM) * _silu(slot_t(9))
        oc_ref[...] = o_c.T

        ux = slot(4)
        xc = vec(V_CB) + vec(V_CW0 + CONV_WIDTH - 1) * ux
        for j in range(CONV_WIDTH - 1):
            xc = xc + vec(V_CW0 + j) * scv_ref[j]
        xc_bf = xc.astype(BF16)
        rg = _sigmoid(_dot(xc_bf, wr_ref[...]) + vec(V_BR))
        ig = _sigmoid(_dot(xc_bf, wi_ref[...]) + vec(V_BI))
        log_a = -LRU_C * rg * _softplus(-vec(V_LAM))
        b_in = jnp.sqrt(jnp.maximum(_neg_expm1(2.0 * log_a), 0.0)) * (ig * xc)
        hnew = jnp.exp(log_a) * sl_ref[...] + b_in
        nl_ref[...] = hnew
        ob_ref[...] = hnew * jax.nn.gelu(slot(5))
        for j in range(CONV_WIDTH - 2):
            ncv_ref[j] = scv_ref[j + 1]
        ncv_ref[CONV_WIDTH - 2] = ux

        ud = slot(10)
        wsum = ud
        sums = []
        nxt = 1
        for win in POOL_WINDOWS:
            while nxt < win:
                wsum = wsum + spl_ref[POOL_BUF - nxt]
                nxt += 1
            sums.append(wsum)
        lane = lax.broadcasted_iota(jnp.int32, (1, WIDTH), 1)
        grp = lane // (WIDTH // len(POOL_WINDOWS))
        sel = sums[-1]
        winl = jnp.full((1, WIDTH), float(POOL_WINDOWS[-1]), F32)
        for gi in range(len(POOL_WINDOWS) - 2, -1, -1):
            sel = jnp.where(grp == gi, sums[gi], sel)
            winl = jnp.where(grp == gi, float(POOL_WINDOWS[gi]), winl)
        pooled = sel / jnp.minimum(winl, float(pos0) + 1.0)
        od_ref[...] = _dot((pooled - ud).astype(BF16), wp_ref[...]) * vec(V_PSCALE)
        for j in range(POOL_BUF - 1):
            npl_ref[j] = spl_ref[j + 1]
        npl_ref[POOL_BUF - 1] = ud


def _sample_step(u, lb_logits, vecs, vecs_t, wr_bd, wi_bd, wp_bd, cos_t, sin_t, s_hgrn, s_lru, s_conv,
                 s_ret, s_pool, layer, *, pos0):
    nb = u.shape[0]
    depth = lb_logits.shape[0]
    cost = jnp.broadcast_to(cos_t.reshape(WIDTH, 1), (WIDTH, nb))
    sint = jnp.broadcast_to(sin_t.reshape(WIDTH, 1), (WIDTH, nb))

    def const(shape):
        return pl.BlockSpec(shape, lambda i: tuple(0 for _ in shape))

    def layer_block(shape):
        return pl.BlockSpec((None,) + shape, lambda i: (layer,) + tuple(0 for _ in shape))

    def head(i):
        return jnp.minimum(i, N_HEADS - 1)

    mat = (HEAD_DIM, HEAD_DIM, nb)
    mat_in = pl.BlockSpec((None, None) + mat, lambda i: (layer, head(i), 0, 0, 0))
    mat_out = pl.BlockSpec((None,) + mat, lambda i: (head(i), 0, 0, 0))
    rows = const((nb, WIDTH))
    return pl.pallas_call(
        functools.partial(_sample_step_kernel, layer=layer, pos0=pos0),
        grid=(N_HEADS + 1,),
        in_specs=[
            const((nb, MIX_COLS)), const((WIDTH, depth)),
            layer_block((N_VEC_ROWS, WIDTH)), layer_block((WIDTH, N_VEC_ROWS)),
            layer_block((WIDTH, WIDTH)), layer_block((WIDTH, WIDTH)), layer_block((WIDTH, WIDTH)),
            const((WIDTH, nb)), const((WIDTH, nb)),
            mat_in, layer_block((nb, WIDTH)), layer_block((CONV_WIDTH - 1, nb, WIDTH)), mat_in,
            layer_block((POOL_BUF, nb, WIDTH)),
        ],
        out_specs=[rows, rows, rows, rows, mat_out, rows, const((CONV_WIDTH - 1, nb, WIDTH)), mat_out,
                   const((POOL_BUF, nb, WIDTH))],
        out_shape=[jax.ShapeDtypeStruct((nb, WIDTH), F32)] * 4 + [
            jax.ShapeDtypeStruct((N_HEADS,) + mat, F32), jax.ShapeDtypeStruct((nb, WIDTH), F32),
            jax.ShapeDtypeStruct((CONV_WIDTH - 1, nb, WIDTH), F32), jax.ShapeDtypeStruct((N_HEADS,) + mat, F32),
            jax.ShapeDtypeStruct((POOL_BUF, nb, WIDTH), F32),
        ],
        scratch_shapes=[pltpu.VMEM((7, WIDTH, nb), F32), pltpu.VMEM((2, WIDTH, nb), F32)],
        compiler_params=_cparams("arbitrary"),
        name="sample_step",
    )(u, lb_logits.T, vecs, vecs_t, wr_bd, wi_bd, wp_bd, cost, sint, s_hgrn, s_lru, s_conv, s_ret, s_pool)


def _pick_tile(n, pref):
    t = min(n, pref)
    while n % t:
        t //= 2
    return t


def kernel(x_prompt, x_sample, state_hgrn, state_rglru, state_conv, state_retention, state_pool, lb_logits, ffn1_norm, ffn1_up, ffn1_down, mix_norm, w_in, hgrn_norm, conv_w, conv_b, w_rgate, b_rgate, w_igate, b_igate, lru_lambda, ret_norm, w_pool, pool_scale, w_branch, w_o, ffn2_norm, ffn2_up, ffn2_down, final_norm):
    batch, seq, d = x_prompt.shape
    nb, dec_seq, _ = x_sample.shape
    assert dec_seq == 1
    depth = w_in.shape[0]
    past_len = PAST_LEN

    wr_bd = jax.vmap(_block_diag)(w_rgate).astype(BF16)
    wi_bd = jax.vmap(_block_diag)(w_igate).astype(BF16)
    wp_bd = jax.vmap(_block_diag)(w_pool).astype(BF16)
    vecs = jnp.concatenate(
        [hgrn_norm[:, None], conv_w, conv_b[:, None], b_rgate[:, None], b_igate[:, None],
         lru_lambda[:, None], ret_norm[:, None], pool_scale[:, None],
         jnp.zeros((depth, N_VEC_ROWS - 11, WIDTH), F32)], axis=1)
    vecs_t = jnp.swapaxes(vecs, 1, 2)
    hgrn_t = jnp.transpose(state_hgrn, (0, 2, 3, 4, 1))
    ret_t = jnp.transpose(state_retention, (0, 2, 3, 4, 1))
    conv_t = jnp.transpose(state_conv, (0, 2, 1, 3))
    pool_t = jnp.transpose(state_pool, (0, 2, 1, 3))
    n1 = ffn1_norm[:, None, :]
    n2 = ffn2_norm[:, None, :]
    nm = mix_norm[:, None, :]
    fg = final_norm[None, :]

    cos_p, sin_p = _rope_tables(jnp.arange(seq, dtype=F32))
    cos_s, sin_s = _rope_tables(past_len + jnp.arange(dec_seq, dtype=F32))

    xp = x_prompt.reshape(batch * seq, d)
    xs = x_sample.reshape(nb * dec_seq, d)
    tm_p = _pick_tile(batch * seq, DENSE_ROWS)
    tm_w = _pick_tile(batch * seq, WIDE_ROWS)
    tc = _pick_tile(seq, MIXER_ROWS)

    st_p = [[] for _ in range(5)]
    st_s = [[] for _ in range(5)]
    for l in range(depth):
        last = l == depth - 1
        xp, xs = _ffn(xp, xs, n1, ffn1_up, ffn1_down, l, fg, final_norm=False, tm=tm_p)
        u_p, u_s = _inproj(xp, xs, nm, w_in, l, tm=tm_w)
        *br_p, sh, sl, scv, sr, spl = _mixer_prompt(
            u_p, lb_logits, vecs, wr_bd, wi_bd, wp_bd, cos_p, sin_p, l, batch=batch, seq=seq, tc=tc, pos0=0)
        *br_s, nh, nl, ncv, nr, npl = _sample_step(
            u_s, lb_logits, vecs, vecs_t, wr_bd, wi_bd, wp_bd, cos_s, sin_s, hgrn_t, state_rglru,
            conv_t, ret_t, pool_t, l, pos0=past_len)
        xp, xs = _merge(xp, xs, nm, br_p, br_s, w_in, w_branch, w_o, l, tm=tm_w)
        xp, xs = _ffn(xp, xs, n2, ffn2_up, ffn2_down, l, fg, final_norm=last, tm=tm_p)
        for lst, s in zip(st_p, (jnp.swapaxes(_diag_blocks(sh), 2, 3), sl[:, 0], scv, _diag_blocks(sr), spl)):
            lst.append(s)
        for lst, s in zip(st_s, (nh, nl, ncv, nr, npl)):
            lst.append(s)

    y_p = xp.reshape(batch, seq, d)
    y_s = xs.reshape(nb, dec_seq, d)
    nh, nl, ncv, nr, npl = (jnp.stack(s) for s in st_s)
    sample_states = (jnp.transpose(nh, (0, 4, 1, 2, 3)), nl, jnp.transpose(ncv, (0, 2, 1, 3)),
                     jnp.transpose(nr, (0, 4, 1, 2, 3)), jnp.transpose(npl, (0, 2, 1, 3)))
    return (y_p, y_s) + tuple(jnp.stack(s) for s in st_p) + sample_states
```

```python
import functools

import numpy as np
import jax
import jax.numpy as jnp
from jax import lax
from jax.experimental import pallas as pl
from jax.experimental.pallas import tpu as pltpu

F32 = jnp.float32
BF16 = jnp.bfloat16

N_HEADS = 4
HEAD_DIM = 64
WIDTH = N_HEADS * HEAD_DIM
N_MIX_SLOTS = 11
MIX_COLS = N_MIX_SLOTS * WIDTH
CHUNK = 64
N_LEVELS = 6
PAD = 16
CONV_WIDTH = 4
LRU_C = 8.0
POOL_WINDOWS = (2, 4, 8, 16)
POOL_BUF = 15
ROPE_BASE = 10000.0
EPS = 1e-6
F_FLOOR = 1e-30
SAFE_LOG = 80.0
PAST_LEN = 16384
VMEM_LIMIT = 56 * 1024 * 1024
DENSE_ROWS = 512
WIDE_ROWS = 1024
FFN_UP_CHUNK = 512
FFN_DOWN_CHUNK = 256
MIXER_ROWS = 1024


def _cparams(*sem):
    return pltpu.CompilerParams(dimension_semantics=sem, vmem_limit_bytes=VMEM_LIMIT)


def _dot(a, b):
    return jnp.dot(a, b, preferred_element_type=F32)


def _dot_nt(a, b):
    return lax.dot_general(a, b, (((1,), (1,)), ((), ())), preferred_element_type=F32)


def _dot_tn(a, b):
    return lax.dot_general(a, b, (((0,), (0,)), ((), ())), preferred_element_type=F32)


def _rms(x, g):
    return x * lax.rsqrt(jnp.mean(x * x, axis=-1, keepdims=True) + EPS) * g


def _sigmoid(x):
    return 0.5 * jnp.tanh(0.5 * x) + 0.5


def _silu(x):
    hx = 0.5 * x
    return hx * jnp.tanh(hx) + hx


def _softplus(x):
    return jnp.maximum(x, 0.0) + jnp.log1p(jnp.exp(-jnp.abs(x)))


def _neg_expm1(y):
    t = jnp.tanh(0.5 * y)
    return -2.0 * t / (1.0 - t)


def _split2(x):
    hi = x.astype(BF16)
    lo = (x - hi.astype(F32)).astype(BF16)
    return hi, lo


def _split3(x):
    hi = x.astype(BF16)
    r = x - hi.astype(F32)
    mid = r.astype(BF16)
    lo = (r - mid.astype(F32)).astype(BF16)
    return hi, mid, lo


def _lower_bound(lbl, layer):
    m = jnp.max(lbl, axis=0, keepdims=True)
    e = jnp.exp(lbl - m)
    soft = e / jnp.sum(e, axis=0, keepdims=True)
    acc = soft[0:1]
    for l in range(1, layer + 1):
        acc = acc + soft[l:l + 1]
    return acc - soft[0:1]


def _resident(shape, index_map):
    return pl.BlockSpec(shape, index_map, pipeline_mode=pl.Buffered(1))


def _two_group_kernel(*refs, body, n_rows, n_tiles):
    p_refs, s_refs = refs[:n_rows], refs[n_rows:2 * n_rows]
    params, (op_ref, os_ref) = refs[2 * n_rows:-2], refs[-2:]
    i = pl.program_id(0)

    @pl.when(i < n_tiles)
    def _():
        op_ref[...] = body([r[...] for r in p_refs], params)

    @pl.when(i == n_tiles)
    def _():
        os_ref[...] = body([r[...] for r in s_refs], params)


def _two_group_call(body, rows_p, rows_s, params, param_specs, out_cols, tm, name):
    n_p, n_s = rows_p[0].shape[0], rows_s[0].shape[0]
    n_tiles = n_p // tm

    def tile(i):
        return (jnp.minimum(i, n_tiles - 1), 0)

    return pl.pallas_call(
        functools.partial(_two_group_kernel, body=body, n_rows=len(rows_p), n_tiles=n_tiles),
        grid=(n_tiles + 1,),
        in_specs=[pl.BlockSpec((tm, a.shape[1]), tile) for a in rows_p]
        + [pl.BlockSpec(a.shape, lambda i: (0, 0)) for a in rows_s] + list(param_specs),
        out_specs=[pl.BlockSpec((tm, out_cols), tile), pl.BlockSpec((n_s, out_cols), lambda i: (0, 0))],
        out_shape=[jax.ShapeDtypeStruct((n_p, out_cols), F32), jax.ShapeDtypeStruct((n_s, out_cols), F32)],
        compiler_params=_cparams("arbitrary"),
        name=name,
    )(*rows_p, *rows_s, *params)


def _ffn_body(rows, params, *, final_norm):
    (x,), (g_ref, wu_ref, wd_ref, fg_ref) = rows, params
    dff = wd_ref.shape[0]
    h = _rms(x, g_ref[...]).astype(BF16)
    a = _dot(h, wu_ref[:, :dff])
    b = _dot(h, wu_ref[:, dff:])
    act = (_silu(a) * b).astype(BF16)
    y = x + 0.5 * _dot(act, wd_ref[...])
    return _rms(y, fg_ref[...]) if final_norm else y


def _cast_chunks(n_chunks, src_chunk, dst_chunk, stage, sem):
    def copy(c):
        return pltpu.make_async_copy(src_chunk(c), stage.at[c % 2], sem.at[c % 2])

    copy(0).start()
    for c in range(n_chunks):
        if c + 1 < n_chunks:
            copy(c + 1).start()
        copy(c).wait()
        dst_chunk(c)[...] = stage[c % 2].astype(BF16)


def _ffn_kernel(xp_ref, xs_ref, g_ref, wu_hbm, wd_hbm, fg_ref, op_ref, os_ref,
                wu_bf, wd_bf, stage_u, stage_d, sem_u, sem_d, *, layer, n_tiles, final_norm):
    i = pl.program_id(0)
    cw, rw = stage_u.shape[2], stage_d.shape[1]

    @pl.when(i == 0)
    def _():
        _cast_chunks(wu_bf.shape[1] // cw,
                     lambda c: wu_hbm.at[layer, :, pl.ds(c * cw, cw)],
                     lambda c: wu_bf.at[:, pl.ds(c * cw, cw)], stage_u, sem_u)
        _cast_chunks(wd_bf.shape[0] // rw,
                     lambda c: wd_hbm.at[layer, pl.ds(c * rw, rw), :],
                     lambda c: wd_bf.at[pl.ds(c * rw, rw), :], stage_d, sem_d)

    body = functools.partial(_ffn_body, params=(g_ref, wu_bf, wd_bf, fg_ref), final_norm=final_norm)

    @pl.when(i < n_tiles)
    def _():
        op_ref[...] = body([xp_ref[...]])

    @pl.when(i == n_tiles)
    def _():
        os_ref[...] = body([xs_ref[...]])


def _ffn(xp, xs, g, w_up, w_down, layer, final_g, *, final_norm, tm):
    (n_p, d), n_s = xp.shape, xs.shape[0]
    dff = w_down.shape[1]
    n_tiles = n_p // tm
    assert (2 * dff) % FFN_UP_CHUNK == 0 and dff % FFN_DOWN_CHUNK == 0

    def tile(i):
        return (jnp.minimum(i, n_tiles - 1), 0)

    return pl.pallas_call(
        functools.partial(_ffn_kernel, layer=layer, n_tiles=n_tiles, final_norm=final_norm),
        grid=(n_tiles + 1,),
        in_specs=[
            pl.BlockSpec((tm, d), tile),
            pl.BlockSpec((n_s, d), lambda i: (0, 0)),
            _resident((None, 1, d), lambda i: (layer, 0, 0)),
            pl.BlockSpec(memory_space=pl.ANY),
            pl.BlockSpec(memory_space=pl.ANY),
            _resident((1, d), lambda i: (0, 0)),
        ],
        out_specs=[pl.BlockSpec((tm, d), tile), pl.BlockSpec((n_s, d), lambda i: (0, 0))],
        out_shape=[jax.ShapeDtypeStruct((n_p, d), F32), jax.ShapeDtypeStruct((n_s, d), F32)],
        scratch_shapes=[
            pltpu.VMEM((d, 2 * dff), BF16), pltpu.VMEM((dff, d), BF16),
            pltpu.VMEM((2, d, FFN_UP_CHUNK), F32), pltpu.VMEM((2, FFN_DOWN_CHUNK, d), F32),
            pltpu.SemaphoreType.DMA((2,)), pltpu.SemaphoreType.DMA((2,)),
        ],
        compiler_params=_cparams("arbitrary"),
        name="ffn",
    )(xp, xs, g, w_up, w_down, final_g)


def _inproj_body(rows, params):
    (x,), (g_ref, w_ref) = rows, params
    return _dot(_rms(x, g_ref[...]).astype(BF16), w_ref[...])


def _inproj(xp, xs, g, w_in, layer, *, tm):
    d = xp.shape[1]
    specs = [
        _resident((None, 1, d), lambda i: (layer, 0, 0)),
        _resident((None, d, MIX_COLS), lambda i: (layer, 0, 0)),
    ]
    return _two_group_call(_inproj_body, [xp], [xs], (g, w_in), specs, MIX_COLS, tm, "inproj")


def _merge_body(rows, params):
    x, branches = rows[0], rows[1:]
    g_ref, win_ref, wb_ref, wo_ref = params
    d = x.shape[1]
    h = _rms(x, g_ref[...]).astype(BF16)
    merged = None
    for b, o_b in enumerate(branches):
        gate = _sigmoid(_dot(h, win_ref[:, MIX_COLS + b * d:MIX_COLS + (b + 1) * d]))
        y = _dot(o_b.astype(BF16), wb_ref[b])
        merged = gate * y if merged is None else merged + gate * y
    return x + _dot(merged.astype(BF16), wo_ref[...])


def _merge(xp, xs, g, branches_p, branches_s, w_in, w_branch, w_o, layer, *, tm):
    d = xp.shape[1]
    w = branches_p[0].shape[1]
    nb = len(branches_p)
    assert w_in.shape[2] == MIX_COLS + nb * d and MIX_COLS % 128 == 0
    specs = [
        _resident((None, 1, d), lambda i: (layer, 0, 0)),
        _resident((None, d, w_in.shape[2]), lambda i: (layer, 0, 0)),
        _resident((None, nb, w, d), lambda i: (layer, 0, 0, 0)),
        _resident((None, d, d), lambda i: (layer, 0, 0)),
    ]
    return _two_group_call(_merge_body, [xp, *branches_p], [xs, *branches_s],
                           (g, w_in, w_branch, w_o), specs, d, tm, "merge")


def _head_block_ones():
    h = np.arange(WIDTH) // HEAD_DIM
    return (h[:, None] == h[None, :]).astype(np.float32)


def _log_gamma():
    return np.log1p(-(2.0 ** (-5.0 - np.arange(N_HEADS, dtype=np.float64))))


def _retention_tables():
    lg = np.repeat(_log_gamma(), HEAD_DIM)[None, :]
    t = np.arange(CHUNK, dtype=np.float64)[:, None]
    q_dec = np.exp((t + 1.0) * lg)
    k_dec = np.exp((CHUNK - 1.0 - t) * lg)
    s_dec = np.exp(CHUNK * lg)
    dt = t.T - t
    dmat = np.concatenate(
        [np.where(dt >= 0, np.exp(dt * g), 0.0) for g in _log_gamma()], axis=1)
    return (jnp.asarray(q_dec, F32), jnp.asarray(k_dec, F32), jnp.asarray(s_dec, F32),
            jnp.asarray(dmat, F32))


def _level_tables():
    t = np.arange(CHUNK)[:, None]
    r = np.arange(CHUNK)[None, :]
    blocks = [r <= t, r > t]
    pairs = []
    for i in range(N_LEVELS):
        m = 1 << i
        mid = t - (t % (2 * m)) + m
        up = (t % (2 * m)) >= m
        blocks.append(np.where(up, (r >= mid) & (r <= t), (r > t) & (r <= mid - 1)))
        same = (t // (2 * m)) == (r // (2 * m))
        pr = same & ((t % (2 * m)) < m) & ((r % (2 * m)) >= m)
        pairs.append(np.tile(pr, (1, N_HEADS)))
    dsum = np.tile(np.concatenate(blocks, axis=0), (1, 3))
    causal = np.tile(t < r, (1, N_HEADS))
    assert (np.sum(np.stack(pairs), axis=0) == causal).all()
    return jnp.asarray(dsum, BF16), jnp.asarray(np.stack(pairs), F32), jnp.asarray(causal, F32)


def _rope_tables(pos):
    half = HEAD_DIM // 2
    freq = ROPE_BASE ** (-jnp.arange(half, dtype=F32) / half)
    ang = pos[:, None] * freq[None, :]
    cos, sin = jnp.cos(ang), jnp.sin(ang)
    cos_h = jnp.concatenate([cos, cos], axis=-1)
    sin_h = jnp.concatenate([-sin, sin], axis=-1)
    return jnp.tile(cos_h, (1, N_HEADS)), jnp.tile(sin_h, (1, N_HEADS))


def _block_diag(w):
    h, dh, _ = w.shape
    eye = jnp.eye(h, dtype=w.dtype)
    return (eye[:, None, :, None] * w[:, :, None, :]).reshape(h * dh, h * dh)


V_HNORM, V_CW0, V_CB, V_BR, V_BI, V_LAM, V_RNORM, V_PSCALE = 0, 1, 5, 6, 7, 8, 9, 10
N_VEC_ROWS = 16


def _swap_halves(x):
    half = HEAD_DIM // 2
    lane = lax.broadcasted_iota(jnp.int32, (1, 128), 1)
    first = (lane % HEAD_DIM) < half
    parts = []
    for c in range(x.shape[1] // 128):
        xc = x[:, c * 128:(c + 1) * 128]
        parts.append(jnp.where(first, pltpu.roll(xc, 128 - half, 1), pltpu.roll(xc, half, 1)))
    return jnp.concatenate(parts, axis=1)


def _mixer_prompt_kernel(u_ref, lbl_ref, vec_ref, wr_ref, wi_ref, wp_ref, cos_ref, sin_ref,
                         ind_ref, dsum_ref, pair_ref, causal_ref, qdec_ref, kdec_ref, sdec_ref, dmat_ref,
                         oa_ref, ob_ref, oc_ref, od_ref, sh_ref, sl_ref, scv_ref, sr_ref, spl_ref,
                         hst_scr, rst_scr, hl_scr, xcv_scr, xpl_scr, oh_scr, orr_scr, sct_scr,
                         *, layer, tc, pos0):
    t_idx = pl.program_id(1)
    nchunk = tc // CHUNK

    @pl.when(t_idx == 0)
    def _():
        hst_scr[...] = jnp.zeros_like(hst_scr)
        rst_scr[...] = jnp.zeros_like(rst_scr)
        hl_scr[...] = jnp.zeros_like(hl_scr)
        xcv_scr[...] = jnp.zeros_like(xcv_scr)
        xpl_scr[0:PAD, :] = jnp.zeros((PAD, WIDTH), F32)

    def slot(s):
        return u_ref[:, s * WIDTH:(s + 1) * WIDTH]

    def vec(r):
        return vec_ref[r:r + 1, :]

    ind = ind_ref[...]
    lane = lax.broadcasted_iota(jnp.int32, (1, WIDTH), 1)
    hm_bf = [(lane // HEAD_DIM == h).astype(F32).astype(BF16) for h in range(N_HEADS)]
    lane_t = lax.broadcasted_iota(jnp.int32, (1, 128), 1)
    row = lax.broadcasted_iota(jnp.int32, (tc, WIDTH), 0)
    chunks = [slice(n * CHUNK, (n + 1) * CHUNK) for n in range(nchunk)]

    def head_scores(keys, queries):
        stacked = jnp.concatenate([queries * hm_bf[h] for h in range(N_HEADS)], axis=0)
        return _dot_nt(keys, stacked)

    def scores_times_values(sct, val):
        assert HEAD_DIM == 64 and WIDTH % 128 == 0
        tiles = []
        for c in range(WIDTH // 128):
            pair = _dot_tn(sct[:, 2 * c * CHUNK:(2 * c + 2) * CHUNK], val[:, c * 128:(c + 1) * 128])
            tiles.append(jnp.where(lane_t < HEAD_DIM, pair[0:CHUNK], pair[CHUNK:2 * CHUNK]))
        return jnp.concatenate(tiles, axis=1)

    n_lt = WIDTH // 128
    ind_t = ind[0:128, 0:128]

    def lane_tile(x, c):
        return x[:, c * 128:(c + 1) * 128]

    def tile_outer(a, b):
        return [_dot_tn(lane_tile(a, c), lane_tile(b, c)) for c in range(n_lt)]

    ind2 = jnp.concatenate([ind, ind], axis=0)

    def head_sum(x):
        return _dot(jnp.concatenate(_split2(x), axis=1), ind2)

    lb = _lower_bound(lbl_ref[...], layer)
    uq, z, v = slot(0), slot(1), slot(2)
    q = _silu(uq)
    sg = _sigmoid(z)
    f = lb + (1.0 - lb) * sg
    logf = jnp.log(jnp.maximum(f, F_FLOOR))
    k = (1.0 - lb) * (1.0 - sg)
    l_hi, l_mid, l_lo = _split3(logf)
    v_bf = v.astype(BF16)
    dsum = dsum_ref[...]
    rowc = lax.broadcasted_iota(jnp.int32, (CHUNK, WIDTH), 0)
    upper = [((rowc >> i) & 1) == 1 for i in range(N_LEVELS)]
    lsplit = [jnp.concatenate([l_hi[rs], l_mid[rs], l_lo[rs]], axis=0) for rs in chunks]
    logs = [_dot(dsum[0:2 * CHUNK], lsplit[n]) for n in range(nchunk)]
    dec = [jnp.exp(jnp.minimum(lg, 0.0)) for lg in logs]
    qe = [(q[rs] * dec[n][0:CHUNK]).astype(BF16) for n, rs in enumerate(chunks)]
    total_decay = logs[0][CHUNK - 1:CHUNK]
    for lg in logs[1:]:
        total_decay = jnp.minimum(total_decay, lg[CHUNK - 1:CHUNK])
    bounded = jnp.min(total_decay) > -SAFE_LOG

    @pl.when(bounded)
    def _():
        for n, rs in enumerate(chunks):
            k_up = (k[rs] * jnp.exp(-logs[n][0:CHUNK])).astype(BF16)
            sct_scr[rs, :] = (head_scores(k_up, qe[n]) * causal_ref[...]).astype(BF16)

    @pl.when(jnp.logical_not(bounded))
    def _():
        sct = [None] * nchunk
        for n, rs in enumerate(chunks):
            lev = jnp.exp(jnp.minimum(_dot(dsum[2 * CHUNK:], lsplit[n]), 0.0))
            for i in range(N_LEVELS):
                zz = (jnp.where(upper[i], q[rs], k[rs]) * lev[i * CHUNK:(i + 1) * CHUNK]).astype(BF16)
                term = head_scores(zz, zz) * pair_ref[i]
                sct[n] = term if sct[n] is None else sct[n] + term
            sct_scr[rs, :] = sct[n].astype(BF16)

    o_diag = _dot((q * k).astype(BF16), ind) * v
    upd = [tile_outer(v_bf[rs], (k[rs] * dec[n][CHUNK:2 * CHUNK]).astype(BF16)) for n, rs in enumerate(chunks)]
    o_intra = [scores_times_values(sct_scr[rs, :], v_bf[rs]) for rs in chunks]

    seg_len = tc // 8
    ux_seg = jnp.swapaxes(slot(4).reshape(8, seg_len, WIDTH), 0, 1)
    slabs = [ux_seg[i] for i in range(seg_len)]
    sub0 = lax.broadcasted_iota(jnp.int32, (8, WIDTH), 0) == 0
    front = [jnp.where(sub0, xcv_scr[8 + k:9 + k, :], pltpu.roll(slabs[seg_len + k], 1, 0))
             for k in range(-(CONV_WIDTH - 1), 0)]
    ext = front + slabs
    cw = [vec(V_CW0 + j) for j in range(CONV_WIDTH)]
    xc = jnp.concatenate(
        [vec(V_CB) + sum(cw[j] * ext[i + j] for j in range(CONV_WIDTH)) for i in range(seg_len)], axis=0)
    xc_bf = xc.astype(BF16)
    rg = _sigmoid(_dot(xc_bf, wr_ref[...]) + vec(V_BR))
    ig = _sigmoid(_dot(xc_bf, wi_ref[...]) + vec(V_BI))
    log_a = -LRU_C * rg * _softplus(-vec(V_LAM))
    a_all = jnp.exp(log_a)
    b_all = jnp.sqrt(jnp.maximum(_neg_expm1(2.0 * log_a), 0.0)) * (ig * xc)
    h_loc, a_loc = [b_all[0:8]], [a_all[0:8]]
    for i in range(1, seg_len):
        a_i = a_all[8 * i:8 * i + 8]
        h_loc.append(a_i * h_loc[-1] + b_all[8 * i:8 * i + 8])
        a_loc.append(a_i * a_loc[-1])
    carry = [hl_scr[0:1, :]]
    for s in range(8):
        carry.append(a_loc[-1][s:s + 1] * carry[-1] + h_loc[-1][s:s + 1])
    hl_scr[0:1, :] = carry[8]
    carry8 = jnp.concatenate(carry[0:8], axis=0)
    hperm = jnp.concatenate([h_loc[i] + a_loc[i] * carry8 for i in range(seg_len)], axis=0)
    hseq = jnp.swapaxes(hperm.reshape(seg_len, 8, WIDTH), 0, 1).reshape(tc, WIDTH)
    ob_ref[...] = (hseq * jax.nn.gelu(slot(5))).astype(ob_ref.dtype)
    xcv_scr[...] = u_ref[tc - 8:tc, 4 * WIDTH:5 * WIDTH]

    cosv, sinv = cos_ref[...], sin_ref[...]
    cq, ck = slot(6), slot(7)
    qr = cq * cosv + _swap_halves(cq) * sinv
    kr = (ck * cosv + _swap_halves(ck) * sinv) * (HEAD_DIM ** -0.5)
    rv_bf = slot(8).astype(BF16)
    qdec, kdec, sdec, dmat = qdec_ref[...], kdec_ref[...], sdec_ref[...], dmat_ref[...]
    qr_bf, kr_bf = qr.astype(BF16), kr.astype(BF16)
    rsc = []
    for rs in chunks:
        rsc.append((head_scores(kr_bf[rs], qr_bf[rs]) * dmat).astype(BF16))
    r_intra = [scores_times_values(rsc[n], rv_bf[rs]) for n, rs in enumerate(chunks)]
    rupd = [tile_outer((kr[rs] * kdec).astype(BF16), rv_bf[rs]) for rs in chunks]

    ud = slot(10)
    xpl_scr[PAD:PAD + tc, :] = ud
    assert POOL_WINDOWS == (2, 4, 8, 16) and WIDTH // len(POOL_WINDOWS) == 64 and PAD >= POOL_BUF
    halves = []
    for c in range(2):
        acc = xpl_scr[:, c * 128:(c + 1) * 128]
        built = {}
        for sh in (1, 2, 4, 8):
            acc = acc + pltpu.roll(acc, sh, 0)
            built[2 * sh] = acc
        lo, hi = POOL_WINDOWS[2 * c], POOL_WINDOWS[2 * c + 1]
        halves.append(jnp.where(lane_t < 64, built[lo], built[hi])[PAD:PAD + tc])
    sel = jnp.concatenate(halves, axis=1)
    grp = lane // (WIDTH // len(POOL_WINDOWS))
    winl = jnp.full((1, WIDTH), float(POOL_WINDOWS[-1]), F32)
    for gi in range(len(POOL_WINDOWS) - 2, -1, -1):
        winl = jnp.where(grp == gi, float(POOL_WINDOWS[gi]), winl)
    pos = (row + t_idx * tc).astype(F32) + float(pos0)
    pooled = sel / jnp.minimum(winl, pos + 1.0)
    od = _dot((pooled - ud).astype(BF16), wp_ref[...]) * vec(V_PSCALE)
    od_ref[...] = od.astype(od_ref.dtype)
    xpl_scr[0:PAD, :] = xpl_scr[tc:tc + PAD, :]

    st = [hst_scr[c] for c in range(n_lt)]
    s_ret = [rst_scr[c] for c in range(n_lt)]
    for n, rs in enumerate(chunks):
        o_c = jnp.concatenate(
            [_dot_nt(lane_tile(qe[n], c), st[c].astype(BF16) * ind_t) for c in range(n_lt)], axis=1)
        st = [st[c] * lane_tile(dec[n][CHUNK - 1:CHUNK], c) + upd[n][c] for c in range(n_lt)]
        oh_scr[rs, :] = o_c + o_diag[rs] + o_intra[n]
        rqe = (qr[rs] * qdec).astype(BF16)
        o_c = jnp.concatenate(
            [_dot(lane_tile(rqe, c), s_ret[c].astype(BF16) * ind_t) for c in range(n_lt)], axis=1)
        s_ret = [s_ret[c] * lane_tile(sdec, c) + rupd[n][c] for c in range(n_lt)]
        orr_scr[rs, :] = o_c + r_intra[n]
    for c in range(n_lt):
        hst_scr[c] = st[c]
        rst_scr[c] = s_ret[c]
    o_h = oh_scr[...]
    ms = head_sum(o_h * o_h) * (1.0 / HEAD_DIM)
    oa_ref[...] = (o_h * lax.rsqrt(ms + EPS) * vec(V_HNORM) * _silu(slot(3))).astype(oa_ref.dtype)
    o_r = orr_scr[...]
    mu = head_sum(o_r) * (1.0 / HEAD_DIM)
    dev = o_r - mu
    var = head_sum(dev * dev) * (1.0 / HEAD_DIM)
    oc_ref[...] = (dev * lax.rsqrt(var + EPS) * vec(V_RNORM) * _silu(slot(9))).astype(oc_ref.dtype)

    @pl.when(t_idx == pl.num_programs(1) - 1)
    def _():
        sh_ref[...] = hst_scr[...]
        sr_ref[...] = rst_scr[...]
        sl_ref[...] = hl_scr[0:1, :]
        scv_ref[...] = xcv_scr[8 - (CONV_WIDTH - 1):8, :]
        spl_ref[...] = xpl_scr[PAD + tc - POOL_BUF:PAD + tc, :]


def _mixer_prompt(u, lb_logits, vecs, wr_bd, wi_bd, wp_bd, cos_t, sin_t, layer, *, batch, seq, tc, pos0):
    nt = seq // tc
    ind = jnp.asarray(_head_block_ones(), BF16)
    dsum, pair, causal = _level_tables()
    qdec, kdec, sdec, dmat = _retention_tables()
    depth = lb_logits.shape[0]

    def const(shape):
        return pl.BlockSpec(shape, lambda b, t: tuple(0 for _ in shape))

    row_spec = pl.BlockSpec((tc, WIDTH), lambda b, t: (b * nt + t, 0))
    tab_spec = pl.BlockSpec((tc, WIDTH), lambda b, t: (t, 0))

    def state_spec(rows):
        return pl.BlockSpec((None, rows, WIDTH), lambda b, t: (b, 0, 0))

    mat_tiles = (WIDTH // 128, 128, 128)
    mat_spec = pl.BlockSpec((None,) + mat_tiles, lambda b, t: (b, 0, 0, 0))
    n = batch * seq
    outs = pl.pallas_call(
        functools.partial(_mixer_prompt_kernel, layer=layer, tc=tc, pos0=pos0),
        grid=(batch, nt),
        in_specs=[
            pl.BlockSpec((tc, MIX_COLS), lambda b, t: (b * nt + t, 0)),
            const((depth, WIDTH)),
            pl.BlockSpec((None, N_VEC_ROWS, WIDTH), lambda b, t: (layer, 0, 0)),
            pl.BlockSpec((None, WIDTH, WIDTH), lambda b, t: (layer, 0, 0)),
            pl.BlockSpec((None, WIDTH, WIDTH), lambda b, t: (layer, 0, 0)),
            pl.BlockSpec((None, WIDTH, WIDTH), lambda b, t: (layer, 0, 0)),
            tab_spec, tab_spec,
            const((WIDTH, WIDTH)), const(dsum.shape), const(pair.shape), const(causal.shape),
            const((CHUNK, WIDTH)), const((CHUNK, WIDTH)), const((1, WIDTH)),
            const((CHUNK, N_HEADS * CHUNK)),
        ],
        out_specs=[row_spec, row_spec, row_spec, row_spec,
                   mat_spec, state_spec(1), state_spec(CONV_WIDTH - 1), mat_spec,
                   state_spec(POOL_BUF)],
        out_shape=[jax.ShapeDtypeStruct((n, WIDTH), BF16)] * 4 + [
            jax.ShapeDtypeStruct((batch,) + mat_tiles, F32),
            jax.ShapeDtypeStruct((batch, 1, WIDTH), F32),
            jax.ShapeDtypeStruct((batch, CONV_WIDTH - 1, WIDTH), F32),
            jax.ShapeDtypeStruct((batch,) + mat_tiles, F32),
            jax.ShapeDtypeStruct((batch, POOL_BUF, WIDTH), F32),
        ],
        scratch_shapes=[
            pltpu.VMEM(mat_tiles, F32), pltpu.VMEM(mat_tiles, F32), pltpu.VMEM((8, WIDTH), F32),
            pltpu.VMEM((8, WIDTH), F32), pltpu.VMEM((PAD + tc, WIDTH), F32),
            pltpu.VMEM((tc, WIDTH), F32), pltpu.VMEM((tc, WIDTH), F32),
            pltpu.VMEM((tc, N_HEADS * CHUNK), BF16),
        ],
        compiler_params=_cparams("parallel", "arbitrary"),
        name="mixer_prompt",
    )(u, lb_logits, vecs, wr_bd, wi_bd, wp_bd, cos_t, sin_t, ind, dsum, pair, causal, qdec, kdec, sdec, dmat)
    return outs


def _diag_blocks(s):
    per_tile = 128 // HEAD_DIM
    return jnp.stack([s[:, h // per_tile,
                        (h % per_tile) * HEAD_DIM:(h % per_tile + 1) * HEAD_DIM,
                        (h % per_tile) * HEAD_DIM:(h % per_tile + 1) * HEAD_DIM]
                      for h in range(N_HEADS)], axis=1)


def _sample_step_kernel(u_ref, lblt_ref, vec_ref, vect_ref, wr_ref, wi_ref, wp_ref, cost_ref, sint_ref,
                        sh_ref, sl_ref, scv_ref, sr_ref, spl_ref,
                        oa_ref, ob_ref, oc_ref, od_ref, nh_ref, nl_ref, ncv_ref, nr_ref, npl_ref,
                        fac_scr, ot_scr, *, layer, pos0):
    step = pl.program_id(0)
    F_, K_, Q_, V_, RQ_, RK_, RV_ = range(7)

    def slot_t(s):
        return u_ref[:, s * WIDTH:(s + 1) * WIDTH].T

    def vcol(r):
        return vect_ref[:, r:r + 1]

    @pl.when(step == 0)
    def _():
        lblt = lblt_ref[...]
        e = jnp.exp(lblt - jnp.max(lblt, axis=1, keepdims=True))
        soft = e / jnp.sum(e, axis=1, keepdims=True)
        acc = soft[:, 0:1]
        for l in range(1, layer + 1):
            acc = acc + soft[:, l:l + 1]
        lb = acc - soft[:, 0:1]
        sg = _sigmoid(slot_t(1))
        fac_scr[F_] = jnp.maximum(lb + (1.0 - lb) * sg, F_FLOOR)
        fac_scr[K_] = (1.0 - lb) * (1.0 - sg)
        fac_scr[Q_] = _silu(slot_t(0))
        fac_scr[V_] = slot_t(2)

        def swap_rows(x):
            half = HEAD_DIM // 2
            parts = []
            for h in range(N_HEADS):
                parts.append(x[h * HEAD_DIM + half:(h + 1) * HEAD_DIM])
                parts.append(x[h * HEAD_DIM:h * HEAD_DIM + half])
            return jnp.concatenate(parts, axis=0)

        cost, sint = cost_ref[...], sint_ref[...]
        cq, ck = slot_t(6), slot_t(7)
        fac_scr[RQ_] = cq * cost + swap_rows(cq) * sint
        fac_scr[RK_] = (ck * cost + swap_rows(ck) * sint) * (HEAD_DIM ** -0.5)
        fac_scr[RV_] = slot_t(8)

    @pl.when(step < N_HEADS)
    def _():
        base = pl.multiple_of(step * HEAD_DIM, HEAD_DIM)
        gam = [float(np.exp(g)) for g in _log_gamma()]
        gamma = jnp.float32(gam[-1])
        for h in range(N_HEADS - 2, -1, -1):
            gamma = jnp.where(step == h, jnp.float32(gam[h]), gamma)
        v_t = fac_scr[V_, pl.ds(base, HEAD_DIM), :]
        rv_t = fac_scr[RV_, pl.ds(base, HEAD_DIM), :]
        acc_a = jnp.zeros((HEAD_DIM, v_t.shape[1]), F32)
        acc_c = jnp.zeros((HEAD_DIM, v_t.shape[1]), F32)
        for k in range(HEAD_DIM):
            def row(which):
                return fac_scr[which, pl.ds(base + k, 1), :]
            s_new = row(F_) * sh_ref[k] + row(K_) * v_t
            nh_ref[k] = s_new
            acc_a = acc_a + row(Q_) * s_new
            r_new = gamma * sr_ref[k] + row(RK_) * rv_t
            nr_ref[k] = r_new
            acc_c = acc_c + row(RQ_) * r_new
        ot_scr[0, pl.ds(base, HEAD_DIM), :] = acc_a
        ot_scr[1, pl.ds(base, HEAD_DIM), :] = acc_c

    @pl.when(step == N_HEADS)
    def _():
        def vec(r):
            return vec_ref[r:r + 1, :]

        def slot(s):
            return u_ref[:, s * WIDTH:(s + 1) * WIDTH]

        def per_head(x, fn):
            return jnp.concatenate([fn(x[h * HEAD_DIM:(h + 1) * HEAD_DIM]) for h in range(N_HEADS)], axis=0)

        def rms_head(o):
            return o * lax.rsqrt(jnp.mean(o * o, axis=0, keepdims=True) + EPS)

        def group_head(o):
            dev = o - jnp.mean(o, axis=0, keepdims=True)
            return dev * lax.rsqrt(jnp.mean(dev * dev, axis=0, keepdims=True) + EPS)

        o_a = per_head(ot_scr[0], rms_head) * vcol(V_HNORM) * _silu(slot_t(3))
        oa_ref[...] = o_a.T
        o_c = per_head(ot_scr[1], group_head) * vcol(V_RNORM) * _silu(slot_t(9))
        oc_ref[...] = o_c.T

        ux = slot(4)
        xc = vec(V_CB) + vec(V_CW0 + CONV_WIDTH - 1) * ux
        for j in range(CONV_WIDTH - 1):
            xc = xc + vec(V_CW0 + j) * scv_ref[j]
        xc_bf = xc.astype(BF16)
        rg = _sigmoid(_dot(xc_bf, wr_ref[...]) + vec(V_BR))
        ig = _sigmoid(_dot(xc_bf, wi_ref[...]) + vec(V_BI))
        log_a = -LRU_C * rg * _softplus(-vec(V_LAM))
        b_in = jnp.sqrt(jnp.maximum(_neg_expm1(2.0 * log_a), 0.0)) * (ig * xc)
        hnew = jnp.exp(log_a) * sl_ref[...] + b_in
        nl_ref[...] = hnew
        ob_ref[...] = hnew * jax.nn.gelu(slot(5))
        for j in range(CONV_WIDTH - 2):
            ncv_ref[j] = scv_ref[j + 1]
        ncv_ref[CONV_WIDTH - 2] = ux

        ud = slot(10)
        wsum = ud
        sums = []
        nxt = 1
        for win in POOL_WINDOWS:
            while nxt < win:
                wsum = wsum + spl_ref[POOL_BUF - nxt]
                nxt += 1
            sums.append(wsum)
        lane = lax.broadcasted_iota(jnp.int32, (1, WIDTH), 1)
        grp = lane // (WIDTH // len(POOL_WINDOWS))
        sel = sums[-1]
        winl = jnp.full((1, WIDTH), float(POOL_WINDOWS[-1]), F32)
        for gi in range(len(POOL_WINDOWS) - 2, -1, -1):
            sel = jnp.where(grp == gi, sums[gi], sel)
            winl = jnp.where(grp == gi, float(POOL_WINDOWS[gi]), winl)
        pooled = sel / jnp.minimum(winl, float(pos0) + 1.0)
        od_ref[...] = _dot((pooled - ud).astype(BF16), wp_ref[...]) * vec(V_PSCALE)
        for j in range(POOL_BUF - 1):
            npl_ref[j] = spl_ref[j + 1]
        npl_ref[POOL_BUF - 1] = ud


def _sample_step(u, lb_logits, vecs, vecs_t, wr_bd, wi_bd, wp_bd, cos_t, sin_t, s_hgrn, s_lru, s_conv,
                 s_ret, s_pool, layer, *, pos0):
    nb = u.shape[0]
    depth = lb_logits.shape[0]
    cost = jnp.broadcast_to(cos_t.reshape(WIDTH, 1), (WIDTH, nb))
    sint = jnp.broadcast_to(sin_t.reshape(WIDTH, 1), (WIDTH, nb))

    def const(shape):
        return pl.BlockSpec(shape, lambda i: tuple(0 for _ in shape))

    def layer_block(shape):
        return pl.BlockSpec((None,) + shape, lambda i: (layer,) + tuple(0 for _ in shape))

    def head(i):
        return jnp.minimum(i, N_HEADS - 1)

    mat = (HEAD_DIM, HEAD_DIM, nb)
    mat_in = pl.BlockSpec((None, None) + mat, lambda i: (layer, head(i), 0, 0, 0))
    mat_out = pl.BlockSpec((None,) + mat, lambda i: (head(i), 0, 0, 0))
    rows = const((nb, WIDTH))
    return pl.pallas_call(
        functools.partial(_sample_step_kernel, layer=layer, pos0=pos0),
        grid=(N_HEADS + 1,),
        in_specs=[
            const((nb, MIX_COLS)), const((WIDTH, depth)),
            layer_block((N_VEC_ROWS, WIDTH)), layer_block((WIDTH, N_VEC_ROWS)),
            layer_block((WIDTH, WIDTH)), layer_block((WIDTH, WIDTH)), layer_block((WIDTH, WIDTH)),
            const((WIDTH, nb)), const((WIDTH, nb)),
            mat_in, layer_block((nb, WIDTH)), layer_block((CONV_WIDTH - 1, nb, WIDTH)), mat_in,
            layer_block((POOL_BUF, nb, WIDTH)),
        ],
        out_specs=[rows, rows, rows, rows, mat_out, rows, const((CONV_WIDTH - 1, nb, WIDTH)), mat_out,
                   const((POOL_BUF, nb, WIDTH))],
        out_shape=[jax.ShapeDtypeStruct((nb, WIDTH), F32)] * 4 + [
            jax.ShapeDtypeStruct((N_HEADS,) + mat, F32), jax.ShapeDtypeStruct((nb, WIDTH), F32),
            jax.ShapeDtypeStruct((CONV_WIDTH - 1, nb, WIDTH), F32), jax.ShapeDtypeStruct((N_HEADS,) + mat, F32),
            jax.ShapeDtypeStruct((POOL_BUF, nb, WIDTH), F32),
        ],
        scratch_shapes=[pltpu.VMEM((7, WIDTH, nb), F32), pltpu.VMEM((2, WIDTH, nb), F32)],
        compiler_params=_cparams("arbitrary"),
        name="sample_step",
    )(u, lb_logits.T, vecs, vecs_t, wr_bd, wi_bd, wp_bd, cost, sint, s_hgrn, s_lru, s_conv, s_ret, s_pool)


def _pick_tile(n, pref):
    t = min(n, pref)
    while n % t:
        t //= 2
    return t


def kernel(x_prompt, x_sample, state_hgrn, state_rglru, state_conv, state_retention, state_pool, lb_logits, ffn1_norm, ffn1_up, ffn1_down, mix_norm, w_in, hgrn_norm, conv_w, conv_b, w_rgate, b_rgate, w_igate, b_igate, lru_lambda, ret_norm, w_pool, pool_scale, w_branch, w_o, ffn2_norm, ffn2_up, ffn2_down, final_norm):
    batch, seq, d = x_prompt.shape
    nb, dec_seq, _ = x_sample.shape
    assert dec_seq == 1
    depth = w_in.shape[0]
    past_len = PAST_LEN

    w_in_bf = w_in.astype(BF16)
    w_br, w_out = w_branch.astype(BF16), w_o.astype(BF16)
    wr_bd = jax.vmap(_block_diag)(w_rgate).astype(BF16)
    wi_bd = jax.vmap(_block_diag)(w_igate).astype(BF16)
    wp_bd = jax.vmap(_block_diag)(w_pool).astype(BF16)
    vecs = jnp.concatenate(
        [hgrn_norm[:, None], conv_w, conv_b[:, None], b_rgate[:, None], b_igate[:, None],
         lru_lambda[:, None], ret_norm[:, None], pool_scale[:, None],
         jnp.zeros((depth, N_VEC_ROWS - 11, WIDTH), F32)], axis=1)
    vecs_t = jnp.swapaxes(vecs, 1, 2)
    hgrn_t = jnp.transpose(state_hgrn, (0, 2, 3, 4, 1))
    ret_t = jnp.transpose(state_retention, (0, 2, 3, 4, 1))
    conv_t = jnp.transpose(state_conv, (0, 2, 1, 3))
    pool_t = jnp.transpose(state_pool, (0, 2, 1, 3))
    n1 = ffn1_norm[:, None, :]
    n2 = ffn2_norm[:, None, :]
    nm = mix_norm[:, None, :]
    fg = final_norm[None, :]

    cos_p, sin_p = _rope_tables(jnp.arange(seq, dtype=F32))
    cos_s, sin_s = _rope_tables(past_len + jnp.arange(dec_seq, dtype=F32))

    xp = x_prompt.reshape(batch * seq, d)
    xs = x_sample.reshape(nb * dec_seq, d)
    tm_p = _pick_tile(batch * seq, DENSE_ROWS)
    tm_w = _pick_tile(batch * seq, WIDE_ROWS)
    tc = _pick_tile(seq, MIXER_ROWS)

    st_p = [[] for _ in range(5)]
    st_s = [[] for _ in range(5)]
    for l in range(depth):
        last = l == depth - 1
        xp, xs = _ffn(xp, xs, n1, ffn1_up, ffn1_down, l, fg, final_norm=False, tm=tm_p)
        u_p, u_s = _inproj(xp, xs, nm, w_in_bf, l, tm=tm_w)
        *br_p, sh, sl, scv, sr, spl = _mixer_prompt(
            u_p, lb_logits, vecs, wr_bd, wi_bd, wp_bd, cos_p, sin_p, l, batch=batch, seq=seq, tc=tc, pos0=0)
        *br_s, nh, nl, ncv, nr, npl = _sample_step(
            u_s, lb_logits, vecs, vecs_t, wr_bd, wi_bd, wp_bd, cos_s, sin_s, hgrn_t, state_rglru,
            conv_t, ret_t, pool_t, l, pos0=past_len)
        xp, xs = _merge(xp, xs, nm, br_p, br_s, w_in_bf, w_br, w_out, l, tm=tm_w)
        xp, xs = _ffn(xp, xs, n2, ffn2_up, ffn2_down, l, fg, final_norm=last, tm=tm_p)
        for lst, s in zip(st_p, (jnp.swapaxes(_diag_blocks(sh), 2, 3), sl[:, 0], scv, _diag_blocks(sr), spl)):
            lst.append(s)
        for lst, s in zip(st_s, (nh, nl, ncv, nr, npl)):
            lst.append(s)

    y_p = xp.reshape(batch, seq, d)
    y_s = xs.reshape(nb, dec_seq, d)
    nh, nl, ncv, nr, npl = (jnp.stack(s) for s in st_s)
    sample_states = (jnp.transpose(nh, (0, 4, 1, 2, 3)), nl, jnp.transpose(ncv, (0, 2, 1, 3)),
                     jnp.transpose(nr, (0, 4, 1, 2, 3)), jnp.transpose(npl, (0, 2, 1, 3)))
    return (y_p, y_s) + tuple(jnp.stack(s) for s in st_p) + sample_states
```

```python
import functools

import numpy as np
import jax
import jax.numpy as jnp
from jax import lax
from jax.experimental import pallas as pl
from jax.experimental.pallas import tpu as pltpu

F32 = jnp.float32
BF16 = jnp.bfloat16

N_HEADS = 4
HEAD_DIM = 64
WIDTH = N_HEADS * HEAD_DIM
N_MIX_SLOTS = 11
MIX_COLS = N_MIX_SLOTS * WIDTH
CHUNK = 64
N_LEVELS = 6
PAD = 16
CONV_WIDTH = 4
LRU_C = 8.0
POOL_WINDOWS = (2, 4, 8, 16)
POOL_BUF = 15
ROPE_BASE = 10000.0
EPS = 1e-6
F_FLOOR = 1e-30
SAFE_LOG = 80.0
PAST_LEN = 16384
VMEM_LIMIT = 56 * 1024 * 1024
DENSE_ROWS = 1024
WIDE_ROWS = 1024
FFN_UP_CHUNK = 512
FFN_DOWN_CHUNK = 256
MIXER_ROWS = 1024


def _cparams(*sem):
    return pltpu.CompilerParams(dimension_semantics=sem, vmem_limit_bytes=VMEM_LIMIT)


def _dot(a, b):
    return jnp.dot(a, b, preferred_element_type=F32)


def _dot_nt(a, b):
    return lax.dot_general(a, b, (((1,), (1,)), ((), ())), preferred_element_type=F32)


def _dot_tn(a, b):
    return lax.dot_general(a, b, (((0,), (0,)), ((), ())), preferred_element_type=F32)


def _rms(x, g):
    return x * lax.rsqrt(jnp.mean(x * x, axis=-1, keepdims=True) + EPS) * g


def _sigmoid(x):
    return 0.5 * jnp.tanh(0.5 * x) + 0.5


def _silu(x):
    hx = 0.5 * x
    return hx * jnp.tanh(hx) + hx


def _softplus(x):
    return jnp.maximum(x, 0.0) + jnp.log1p(jnp.exp(-jnp.abs(x)))


def _neg_expm1(y):
    t = jnp.tanh(0.5 * y)
    return -2.0 * t / (1.0 - t)


def _split2(x):
    hi = x.astype(BF16)
    lo = (x - hi.astype(F32)).astype(BF16)
    return hi, lo


def _split3(x):
    hi = x.astype(BF16)
    r = x - hi.astype(F32)
    mid = r.astype(BF16)
    lo = (r - mid.astype(F32)).astype(BF16)
    return hi, mid, lo


def _lower_bound(lbl, layer):
    m = jnp.max(lbl, axis=0, keepdims=True)
    e = jnp.exp(lbl - m)
    soft = e / jnp.sum(e, axis=0, keepdims=True)
    acc = soft[0:1]
    for l in range(1, layer + 1):
        acc = acc + soft[l:l + 1]
    return acc - soft[0:1]


def _resident(shape, index_map):
    return pl.BlockSpec(shape, index_map, pipeline_mode=pl.Buffered(1))


def _two_group_kernel(*refs, body, n_rows, n_tiles):
    p_refs, s_refs = refs[:n_rows], refs[n_rows:2 * n_rows]
    params, (op_ref, os_ref) = refs[2 * n_rows:-2], refs[-2:]
    i = pl.program_id(0)

    @pl.when(i < n_tiles)
    def _():
        op_ref[...] = body([r[...] for r in p_refs], params)

    @pl.when(i == n_tiles)
    def _():
        os_ref[...] = body([r[...] for r in s_refs], params)


def _two_group_call(body, rows_p, rows_s, params, param_specs, out_cols, tm, name):
    n_p, n_s = rows_p[0].shape[0], rows_s[0].shape[0]
    n_tiles = n_p // tm

    def tile(i):
        return (jnp.minimum(i, n_tiles - 1), 0)

    return pl.pallas_call(
        functools.partial(_two_group_kernel, body=body, n_rows=len(rows_p), n_tiles=n_tiles),
        grid=(n_tiles + 1,),
        in_specs=[pl.BlockSpec((tm, a.shape[1]), tile) for a in rows_p]
        + [pl.BlockSpec(a.shape, lambda i: (0, 0)) for a in rows_s] + list(param_specs),
        out_specs=[pl.BlockSpec((tm, out_cols), tile), pl.BlockSpec((n_s, out_cols), lambda i: (0, 0))],
        out_shape=[jax.ShapeDtypeStruct((n_p, out_cols), F32), jax.ShapeDtypeStruct((n_s, out_cols), F32)],
        compiler_params=_cparams("arbitrary"),
        name=name,
    )(*rows_p, *rows_s, *params)


def _ffn_body(rows, params, *, final_norm):
    (x,), (g_ref, wu_ref, wd_ref, fg_ref) = rows, params
    dff = wd_ref.shape[0]
    h = _rms(x, g_ref[...]).astype(BF16)
    a = _dot(h, wu_ref[:, :dff])
    b = _dot(h, wu_ref[:, dff:])
    act = (_silu(a) * b).astype(BF16)
    y = x + 0.5 * _dot(act, wd_ref[...])
    return _rms(y, fg_ref[...]) if final_norm else y


def _cast_chunks(n_chunks, src_chunk, dst_chunk, stage, sem):
    def copy(c):
        return pltpu.make_async_copy(src_chunk(c), stage.at[c % 2], sem.at[c % 2])

    copy(0).start()
    for c in range(n_chunks):
        if c + 1 < n_chunks:
            copy(c + 1).start()
        copy(c).wait()
        dst_chunk(c)[...] = stage[c % 2].astype(BF16)


def _ffn_kernel(xp_ref, xs_ref, g_ref, wu_hbm, wd_hbm, fg_ref, op_ref, os_ref,
                wu_bf, wd_bf, stage_u, stage_d, sem_u, sem_d, *, layer, n_tiles, final_norm):
    i = pl.program_id(0)
    cw, rw = stage_u.shape[2], stage_d.shape[1]

    @pl.when(i == 0)
    def _():
        _cast_chunks(wu_bf.shape[1] // cw,
                     lambda c: wu_hbm.at[layer, :, pl.ds(c * cw, cw)],
                     lambda c: wu_bf.at[:, pl.ds(c * cw, cw)], stage_u, sem_u)
        _cast_chunks(wd_bf.shape[0] // rw,
                     lambda c: wd_hbm.at[layer, pl.ds(c * rw, rw), :],
                     lambda c: wd_bf.at[pl.ds(c * rw, rw), :], stage_d, sem_d)

    body = functools.partial(_ffn_body, params=(g_ref, wu_bf, wd_bf, fg_ref), final_norm=final_norm)

    @pl.when(i < n_tiles)
    def _():
        half = op_ref.shape[0] // 2
        for r in range(2):
            op_ref[r * half:(r + 1) * half, :] = body([xp_ref[r * half:(r + 1) * half, :]])

    @pl.when(i == n_tiles)
    def _():
        os_ref[...] = body([xs_ref[...]])


def _ffn(xp, xs, g, w_up, w_down, layer, final_g, *, final_norm, tm):
    (n_p, d), n_s = xp.shape, xs.shape[0]
    dff = w_down.shape[1]
    n_tiles = n_p // tm
    assert (2 * dff) % FFN_UP_CHUNK == 0 and dff % FFN_DOWN_CHUNK == 0

    def tile(i):
        return (jnp.minimum(i, n_tiles - 1), 0)

    return pl.pallas_call(
        functools.partial(_ffn_kernel, layer=layer, n_tiles=n_tiles, final_norm=final_norm),
        grid=(n_tiles + 1,),
        in_specs=[
            pl.BlockSpec((tm, d), tile),
            pl.BlockSpec((n_s, d), lambda i: (0, 0)),
            _resident((None, 1, d), lambda i: (layer, 0, 0)),
            pl.BlockSpec(memory_space=pl.ANY),
            pl.BlockSpec(memory_space=pl.ANY),
            _resident((1, d), lambda i: (0, 0)),
        ],
        out_specs=[pl.BlockSpec((tm, d), tile), pl.BlockSpec((n_s, d), lambda i: (0, 0))],
        out_shape=[jax.ShapeDtypeStruct((n_p, d), F32), jax.ShapeDtypeStruct((n_s, d), F32)],
        scratch_shapes=[
            pltpu.VMEM((d, 2 * dff), BF16), pltpu.VMEM((dff, d), BF16),
            pltpu.VMEM((2, d, FFN_UP_CHUNK), F32), pltpu.VMEM((2, FFN_DOWN_CHUNK, d), F32),
            pltpu.SemaphoreType.DMA((2,)), pltpu.SemaphoreType.DMA((2,)),
        ],
        compiler_params=_cparams("arbitrary"),
        name="ffn",
    )(xp, xs, g, w_up, w_down, final_g)


def _inproj_body(rows, params):
    (x,), (g_ref, w_ref) = rows, params
    return _dot(_rms(x, g_ref[...]).astype(BF16), w_ref[...])


def _inproj(xp, xs, g, w_in, layer, *, tm):
    d = xp.shape[1]
    specs = [
        _resident((None, 1, d), lambda i: (layer, 0, 0)),
        _resident((None, d, MIX_COLS), lambda i: (layer, 0, 0)),
    ]
    return _two_group_call(_inproj_body, [xp], [xs], (g, w_in), specs, MIX_COLS, tm, "inproj")


def _merge_body(rows, params):
    x, branches = rows[0], rows[1:]
    g_ref, win_ref, wb_ref, wo_ref = params
    d = x.shape[1]
    h = _rms(x, g_ref[...]).astype(BF16)
    merged = None
    for b, o_b in enumerate(branches):
        gate = _sigmoid(_dot(h, win_ref[:, MIX_COLS + b * d:MIX_COLS + (b + 1) * d]))
        y = _dot(o_b.astype(BF16), wb_ref[b])
        merged = gate * y if merged is None else merged + gate * y
    return x + _dot(merged.astype(BF16), wo_ref[...])


def _merge(xp, xs, g, branches_p, branches_s, w_in, w_branch, w_o, layer, *, tm):
    d = xp.shape[1]
    w = branches_p[0].shape[1]
    nb = len(branches_p)
    assert w_in.shape[2] == MIX_COLS + nb * d and MIX_COLS % 128 == 0
    specs = [
        _resident((None, 1, d), lambda i: (layer, 0, 0)),
        _resident((None, d, w_in.shape[2]), lambda i: (layer, 0, 0)),
        _resident((None, nb, w, d), lambda i: (layer, 0, 0, 0)),
        _resident((None, d, d), lambda i: (layer, 0, 0)),
    ]
    return _two_group_call(_merge_body, [xp, *branches_p], [xs, *branches_s],
                           (g, w_in, w_branch, w_o), specs, d, tm, "merge")


def _head_block_ones():
    h = np.arange(WIDTH) // HEAD_DIM
    return (h[:, None] == h[None, :]).astype(np.float32)


def _log_gamma():
    return np.log1p(-(2.0 ** (-5.0 - np.arange(N_HEADS, dtype=np.float64))))


def _retention_tables():
    lg = np.repeat(_log_gamma(), HEAD_DIM)[None, :]
    t = np.arange(CHUNK, dtype=np.float64)[:, None]
    q_dec = np.exp((t + 1.0) * lg)
    k_dec = np.exp((CHUNK - 1.0 - t) * lg)
    s_dec = np.exp(CHUNK * lg)
    dt = t.T - t
    dmat = np.concatenate(
        [np.where(dt >= 0, np.exp(dt * g), 0.0) for g in _log_gamma()], axis=1)
    return (jnp.asarray(q_dec, F32), jnp.asarray(k_dec, F32), jnp.asarray(s_dec, F32),
            jnp.asarray(dmat, F32))


def _level_tables():
    t = np.arange(CHUNK)[:, None]
    r = np.arange(CHUNK)[None, :]
    blocks = [r <= t, r > t]
    pairs = []
    for i in range(N_LEVELS):
        m = 1 << i
        mid = t - (t % (2 * m)) + m
        up = (t % (2 * m)) >= m
        blocks.append(np.where(up, (r >= mid) & (r <= t), (r > t) & (r <= mid - 1)))
        same = (t // (2 * m)) == (r // (2 * m))
        pr = same & ((t % (2 * m)) < m) & ((r % (2 * m)) >= m)
        pairs.append(np.tile(pr, (1, N_HEADS)))
    dsum = np.tile(np.concatenate(blocks, axis=0), (1, 3))
    causal = np.tile(t < r, (1, N_HEADS))
    assert (np.sum(np.stack(pairs), axis=0) == causal).all()
    return jnp.asarray(dsum, BF16), jnp.asarray(np.stack(pairs), F32), jnp.asarray(causal, F32)


def _rope_tables(pos):
    half = HEAD_DIM // 2
    freq = ROPE_BASE ** (-jnp.arange(half, dtype=F32) / half)
    ang = pos[:, None] * freq[None, :]
    cos, sin = jnp.cos(ang), jnp.sin(ang)
    cos_h = jnp.concatenate([cos, cos], axis=-1)
    sin_h = jnp.concatenate([-sin, sin], axis=-1)
    return jnp.tile(cos_h, (1, N_HEADS)), jnp.tile(sin_h, (1, N_HEADS))


def _block_diag(w):
    h, dh, _ = w.shape
    eye = jnp.eye(h, dtype=w.dtype)
    return (eye[:, None, :, None] * w[:, :, None, :]).reshape(h * dh, h * dh)


V_HNORM, V_CW0, V_CB, V_BR, V_BI, V_LAM, V_RNORM, V_PSCALE = 0, 1, 5, 6, 7, 8, 9, 10
N_VEC_ROWS = 16


def _swap_halves(x):
    half = HEAD_DIM // 2
    lane = lax.broadcasted_iota(jnp.int32, (1, 128), 1)
    first = (lane % HEAD_DIM) < half
    parts = []
    for c in range(x.shape[1] // 128):
        xc = x[:, c * 128:(c + 1) * 128]
        parts.append(jnp.where(first, pltpu.roll(xc, 128 - half, 1), pltpu.roll(xc, half, 1)))
    return jnp.concatenate(parts, axis=1)


def _mixer_prompt_kernel(u_ref, lbl_ref, vec_ref, wr_ref, wi_ref, wp_ref, cos_ref, sin_ref,
                         ind_ref, dsum_ref, pair_ref, causal_ref, qdec_ref, kdec_ref, sdec_ref, dmat_ref,
                         oa_ref, ob_ref, oc_ref, od_ref, sh_ref, sl_ref, scv_ref, sr_ref, spl_ref,
                         hst_scr, rst_scr, hl_scr, xcv_scr, xpl_scr, oh_scr, orr_scr, sct_scr,
                         *, layer, tc, pos0):
    t_idx = pl.program_id(1)
    nchunk = tc // CHUNK

    @pl.when(t_idx == 0)
    def _():
        hst_scr[...] = jnp.zeros_like(hst_scr)
        rst_scr[...] = jnp.zeros_like(rst_scr)
        hl_scr[...] = jnp.zeros_like(hl_scr)
        xcv_scr[...] = jnp.zeros_like(xcv_scr)
        xpl_scr[0:PAD, :] = jnp.zeros((PAD, WIDTH), F32)

    def slot(s):
        return u_ref[:, s * WIDTH:(s + 1) * WIDTH]

    def vec(r):
        return vec_ref[r:r + 1, :]

    ind = ind_ref[...]
    lane = lax.broadcasted_iota(jnp.int32, (1, WIDTH), 1)
    hm_bf = [(lane // HEAD_DIM == h).astype(F32).astype(BF16) for h in range(N_HEADS)]
    lane_t = lax.broadcasted_iota(jnp.int32, (1, 128), 1)
    row = lax.broadcasted_iota(jnp.int32, (tc, WIDTH), 0)
    chunks = [slice(n * CHUNK, (n + 1) * CHUNK) for n in range(nchunk)]

    def head_scores(keys, queries):
        stacked = jnp.concatenate([queries * hm_bf[h] for h in range(N_HEADS)], axis=0)
        return _dot_nt(keys, stacked)

    def scores_times_values(sct, val):
        assert HEAD_DIM == 64 and WIDTH % 128 == 0
        tiles = []
        for c in range(WIDTH // 128):
            pair = _dot_tn(sct[:, 2 * c * CHUNK:(2 * c + 2) * CHUNK], val[:, c * 128:(c + 1) * 128])
            tiles.append(jnp.where(lane_t < HEAD_DIM, pair[0:CHUNK], pair[CHUNK:2 * CHUNK]))
        return jnp.concatenate(tiles, axis=1)

    n_lt = WIDTH // 128
    ind_t = ind[0:128, 0:128]

    def lane_tile(x, c):
        return x[:, c * 128:(c + 1) * 128]

    def tile_outer(a, b):
        return [_dot_tn(lane_tile(a, c), lane_tile(b, c)) for c in range(n_lt)]

    ind2 = jnp.concatenate([ind, ind], axis=0)

    def head_sum(x):
        return _dot(jnp.concatenate(_split2(x), axis=1), ind2)

    lb = _lower_bound(lbl_ref[...], layer)
    uq, z, v = slot(0), slot(1), slot(2)
    q = _silu(uq)
    sg = _sigmoid(z)
    f = lb + (1.0 - lb) * sg
    logf = jnp.log(jnp.maximum(f, F_FLOOR))
    k = (1.0 - lb) * (1.0 - sg)
    l_hi, l_mid, l_lo = _split3(logf)
    v_bf = v.astype(BF16)
    dsum = dsum_ref[...]
    rowc = lax.broadcasted_iota(jnp.int32, (CHUNK, WIDTH), 0)
    upper = [((rowc >> i) & 1) == 1 for i in range(N_LEVELS)]
    lsplit = [jnp.concatenate([l_hi[rs], l_mid[rs], l_lo[rs]], axis=0) for rs in chunks]
    logs = [_dot(dsum[0:2 * CHUNK], lsplit[n]) for n in range(nchunk)]
    dec = [jnp.exp(jnp.minimum(lg, 0.0)) for lg in logs]
    qe = [(q[rs] * dec[n][0:CHUNK]).astype(BF16) for n, rs in enumerate(chunks)]
    total_decay = logs[0][CHUNK - 1:CHUNK]
    for lg in logs[1:]:
        total_decay = jnp.minimum(total_decay, lg[CHUNK - 1:CHUNK])
    bounded = jnp.min(total_decay) > -SAFE_LOG

    @pl.when(bounded)
    def _():
        for n, rs in enumerate(chunks):
            k_up = (k[rs] * jnp.exp(-logs[n][0:CHUNK])).astype(BF16)
            sct_scr[rs, :] = (head_scores(k_up, qe[n]) * causal_ref[...]).astype(BF16)

    @pl.when(jnp.logical_not(bounded))
    def _():
        sct = [None] * nchunk
        for n, rs in enumerate(chunks):
            lev = jnp.exp(jnp.minimum(_dot(dsum[2 * CHUNK:], lsplit[n]), 0.0))
            for i in range(N_LEVELS):
                zz = (jnp.where(upper[i], q[rs], k[rs]) * lev[i * CHUNK:(i + 1) * CHUNK]).astype(BF16)
                term = head_scores(zz, zz) * pair_ref[i]
                sct[n] = term if sct[n] is None else sct[n] + term
            sct_scr[rs, :] = sct[n].astype(BF16)

    o_diag = _dot((q * k).astype(BF16), ind) * v
    upd = [tile_outer(v_bf[rs], (k[rs] * dec[n][CHUNK:2 * CHUNK]).astype(BF16)) for n, rs in enumerate(chunks)]
    o_intra = [scores_times_values(sct_scr[rs, :], v_bf[rs]) for rs in chunks]

    seg_len = tc // 8
    ux_seg = jnp.swapaxes(slot(4).reshape(8, seg_len, WIDTH), 0, 1)
    slabs = [ux_seg[i] for i in range(seg_len)]
    sub0 = lax.broadcasted_iota(jnp.int32, (8, WIDTH), 0) == 0
    front = [jnp.where(sub0, xcv_scr[8 + k:9 + k, :], pltpu.roll(slabs[seg_len + k], 1, 0))
             for k in range(-(CONV_WIDTH - 1), 0)]
    ext = front + slabs
    cw = [vec(V_CW0 + j) for j in range(CONV_WIDTH)]
    xc = jnp.concatenate(
        [vec(V_CB) + sum(cw[j] * ext[i + j] for j in range(CONV_WIDTH)) for i in range(seg_len)], axis=0)
    xc_bf = xc.astype(BF16)
    rg = _sigmoid(_dot(xc_bf, wr_ref[...]) + vec(V_BR))
    ig = _sigmoid(_dot(xc_bf, wi_ref[...]) + vec(V_BI))
    log_a = -LRU_C * rg * _softplus(-vec(V_LAM))
    a_all = jnp.exp(log_a)
    b_all = jnp.sqrt(jnp.maximum(_neg_expm1(2.0 * log_a), 0.0)) * (ig * xc)
    h_loc, a_loc = [b_all[0:8]], [a_all[0:8]]
    for i in range(1, seg_len):
        a_i = a_all[8 * i:8 * i + 8]
        h_loc.append(a_i * h_loc[-1] + b_all[8 * i:8 * i + 8])
        a_loc.append(a_i * a_loc[-1])
    carry = [hl_scr[0:1, :]]
    for s in range(8):
        carry.append(a_loc[-1][s:s + 1] * carry[-1] + h_loc[-1][s:s + 1])
    hl_scr[0:1, :] = carry[8]
    carry8 = jnp.concatenate(carry[0:8], axis=0)
    hperm = jnp.concatenate([h_loc[i] + a_loc[i] * carry8 for i in range(seg_len)], axis=0)
    hseq = jnp.swapaxes(hperm.reshape(seg_len, 8, WIDTH), 0, 1).reshape(tc, WIDTH)
    ob_ref[...] = (hseq * jax.nn.gelu(slot(5))).astype(ob_ref.dtype)
    xcv_scr[...] = u_ref[tc - 8:tc, 4 * WIDTH:5 * WIDTH]

    cosv, sinv = cos_ref[...], sin_ref[...]
    cq, ck = slot(6), slot(7)
    qr = cq * cosv + _swap_halves(cq) * sinv
    kr = (ck * cosv + _swap_halves(ck) * sinv) * (HEAD_DIM ** -0.5)
    rv_bf = slot(8).astype(BF16)
    qdec, kdec, sdec, dmat = qdec_ref[...], kdec_ref[...], sdec_ref[...], dmat_ref[...]
    qr_bf, kr_bf = qr.astype(BF16), kr.astype(BF16)
    rsc = []
    for rs in chunks:
        rsc.append((head_scores(kr_bf[rs], qr_bf[rs]) * dmat).astype(BF16))
    r_intra = [scores_times_values(rsc[n], rv_bf[rs]) for n, rs in enumerate(chunks)]
    rupd = [tile_outer((kr[rs] * kdec).astype(BF16), rv_bf[rs]) for rs in chunks]

    ud = slot(10)
    xpl_scr[PAD:PAD + tc, :] = ud
    assert POOL_WINDOWS == (2, 4, 8, 16) and WIDTH // len(POOL_WINDOWS) == 64 and PAD >= POOL_BUF
    halves = []
    for c in range(2):
        acc = xpl_scr[:, c * 128:(c + 1) * 128]
        built = {}
        for sh in (1, 2, 4, 8):
            acc = acc + pltpu.roll(acc, sh, 0)
            built[2 * sh] = acc
        lo, hi = POOL_WINDOWS[2 * c], POOL_WINDOWS[2 * c + 1]
        halves.append(jnp.where(lane_t < 64, built[lo], built[hi])[PAD:PAD + tc])
    sel = jnp.concatenate(halves, axis=1)
    grp = lane // (WIDTH // len(POOL_WINDOWS))
    winl = jnp.full((1, WIDTH), float(POOL_WINDOWS[-1]), F32)
    for gi in range(len(POOL_WINDOWS) - 2, -1, -1):
        winl = jnp.where(grp == gi, float(POOL_WINDOWS[gi]), winl)
    pos = (row + t_idx * tc).astype(F32) + float(pos0)
    pooled = sel / jnp.minimum(winl, pos + 1.0)
    od = _dot((pooled - ud).astype(BF16), wp_ref[...]) * vec(V_PSCALE)
    od_ref[...] = od.astype(od_ref.dtype)
    xpl_scr[0:PAD, :] = xpl_scr[tc:tc + PAD, :]

    st = [hst_scr[c] for c in range(n_lt)]
    s_ret = [rst_scr[c] for c in range(n_lt)]
    for n, rs in enumerate(chunks):
        o_c = jnp.concatenate(
            [_dot_nt(lane_tile(qe[n], c), st[c].astype(BF16) * ind_t) for c in range(n_lt)], axis=1)
        st = [st[c] * lane_tile(dec[n][CHUNK - 1:CHUNK], c) + upd[n][c] for c in range(n_lt)]
        oh_scr[rs, :] = o_c + o_diag[rs] + o_intra[n]
        rqe = (qr[rs] * qdec).astype(BF16)
        o_c = jnp.concatenate(
            [_dot(lane_tile(rqe, c), s_ret[c].astype(BF16) * ind_t) for c in range(n_lt)], axis=1)
        s_ret = [s_ret[c] * lane_tile(sdec, c) + rupd[n][c] for c in range(n_lt)]
        orr_scr[rs, :] = o_c + r_intra[n]
    for c in range(n_lt):
        hst_scr[c] = st[c]
        rst_scr[c] = s_ret[c]
    o_h = oh_scr[...]
    ms = head_sum(o_h * o_h) * (1.0 / HEAD_DIM)
    oa_ref[...] = (o_h * lax.rsqrt(ms + EPS) * vec(V_HNORM) * _silu(slot(3))).astype(oa_ref.dtype)
    o_r = orr_scr[...]
    mu = head_sum(o_r) * (1.0 / HEAD_DIM)
    dev = o_r - mu
    var = head_sum(dev * dev) * (1.0 / HEAD_DIM)
    oc_ref[...] = (dev * lax.rsqrt(var + EPS) * vec(V_RNORM) * _silu(slot(9))).astype(oc_ref.dtype)

    @pl.when(t_idx == pl.num_programs(1) - 1)
    def _():
        sh_ref[...] = hst_scr[...]
        sr_ref[...] = rst_scr[...]
        sl_ref[...] = hl_scr[0:1, :]
        scv_ref[...] = xcv_scr[8 - (CONV_WIDTH - 1):8, :]
        spl_ref[...] = xpl_scr[PAD + tc - POOL_BUF:PAD + tc, :]


def _mixer_prompt(u, lb_logits, vecs, wr_bd, wi_bd, wp_bd, cos_t, sin_t, layer, *, batch, seq, tc, pos0):
    nt = seq // tc
    ind = jnp.asarray(_head_block_ones(), BF16)
    dsum, pair, causal = _level_tables()
    qdec, kdec, sdec, dmat = _retention_tables()
    depth = lb_logits.shape[0]

    def const(shape):
        return pl.BlockSpec(shape, lambda b, t: tuple(0 for _ in shape))

    row_spec = pl.BlockSpec((tc, WIDTH), lambda b, t: (b * nt + t, 0))
    tab_spec = pl.BlockSpec((tc, WIDTH), lambda b, t: (t, 0))

    def state_spec(rows):
        return pl.BlockSpec((None, rows, WIDTH), lambda b, t: (b, 0, 0))

    mat_tiles = (WIDTH // 128, 128, 128)
    mat_spec = pl.BlockSpec((None,) + mat_tiles, lambda b, t: (b, 0, 0, 0))
    n = batch * seq
    outs = pl.pallas_call(
        functools.partial(_mixer_prompt_kernel, layer=layer, tc=tc, pos0=pos0),
        grid=(batch, nt),
        in_specs=[
            pl.BlockSpec((tc, MIX_COLS), lambda b, t: (b * nt + t, 0)),
            const((depth, WIDTH)),
            pl.BlockSpec((None, N_VEC_ROWS, WIDTH), lambda b, t: (layer, 0, 0)),
            pl.BlockSpec((None, WIDTH, WIDTH), lambda b, t: (layer, 0, 0)),
            pl.BlockSpec((None, WIDTH, WIDTH), lambda b, t: (layer, 0, 0)),
            pl.BlockSpec((None, WIDTH, WIDTH), lambda b, t: (layer, 0, 0)),
            tab_spec, tab_spec,
            const((WIDTH, WIDTH)), const(dsum.shape), const(pair.shape), const(causal.shape),
            const((CHUNK, WIDTH)), const((CHUNK, WIDTH)), const((1, WIDTH)),
            const((CHUNK, N_HEADS * CHUNK)),
        ],
        out_specs=[row_spec, row_spec, row_spec, row_spec,
                   mat_spec, state_spec(1), state_spec(CONV_WIDTH - 1), mat_spec,
                   state_spec(POOL_BUF)],
        out_shape=[jax.ShapeDtypeStruct((n, WIDTH), BF16)] * 4 + [
            jax.ShapeDtypeStruct((batch,) + mat_tiles, F32),
            jax.ShapeDtypeStruct((batch, 1, WIDTH), F32),
            jax.ShapeDtypeStruct((batch, CONV_WIDTH - 1, WIDTH), F32),
            jax.ShapeDtypeStruct((batch,) + mat_tiles, F32),
            jax.ShapeDtypeStruct((batch, POOL_BUF, WIDTH), F32),
        ],
        scratch_shapes=[
            pltpu.VMEM(mat_tiles, F32), pltpu.VMEM(mat_tiles, F32), pltpu.VMEM((8, WIDTH), F32),
            pltpu.VMEM((8, WIDTH), F32), pltpu.VMEM((PAD + tc, WIDTH), F32),
            pltpu.VMEM((tc, WIDTH), F32), pltpu.VMEM((tc, WIDTH), F32),
            pltpu.VMEM((tc, N_HEADS * CHUNK), BF16),
        ],
        compiler_params=_cparams("parallel", "arbitrary"),
        name="mixer_prompt",
    )(u, lb_logits, vecs, wr_bd, wi_bd, wp_bd, cos_t, sin_t, ind, dsum, pair, causal, qdec, kdec, sdec, dmat)
    return outs


def _diag_blocks(s):
    per_tile = 128 // HEAD_DIM
    return jnp.stack([s[:, h // per_tile,
                        (h % per_tile) * HEAD_DIM:(h % per_tile + 1) * HEAD_DIM,
                        (h % per_tile) * HEAD_DIM:(h % per_tile + 1) * HEAD_DIM]
                      for h in range(N_HEADS)], axis=1)


def _sample_step_kernel(u_ref, lblt_ref, vec_ref, vect_ref, wr_ref, wi_ref, wp_ref, cost_ref, sint_ref,
                        sh_ref, sl_ref, scv_ref, sr_ref, spl_ref,
                        oa_ref, ob_ref, oc_ref, od_ref, nh_ref, nl_ref, ncv_ref, nr_ref, npl_ref,
                        fac_scr, ot_scr, *, layer, pos0):
    step = pl.program_id(0)
    F_, K_, Q_, V_, RQ_, RK_, RV_ = range(7)

    def slot_t(s):
        return u_ref[:, s * WIDTH:(s + 1) * WIDTH].T

    def vcol(r):
        return vect_ref[:, r:r + 1]

    @pl.when(step == 0)
    def _():
        lblt = lblt_ref[...]
        e = jnp.exp(lblt - jnp.max(lblt, axis=1, keepdims=True))
        soft = e / jnp.sum(e, axis=1, keepdims=True)
        acc = soft[:, 0:1]
        for l in range(1, layer + 1):
            acc = acc + soft[:, l:l + 1]
        lb = acc - soft[:, 0:1]
        sg = _sigmoid(slot_t(1))
        fac_scr[F_] = jnp.maximum(lb + (1.0 - lb) * sg, F_FLOOR)
        fac_scr[K_] = (1.0 - lb) * (1.0 - sg)
        fac_scr[Q_] = _silu(slot_t(0))
        fac_scr[V_] = slot_t(2)

        def swap_rows(x):
            half = HEAD_DIM // 2
            parts = []
            for h in range(N_HEADS):
                parts.append(x[h * HEAD_DIM + half:(h + 1) * HEAD_DIM])
                parts.append(x[h * HEAD_DIM:h * HEAD_DIM + half])
            return jnp.concatenate(parts, axis=0)

        cost, sint = cost_ref[...], sint_ref[...]
        cq, ck = slot_t(6), slot_t(7)
        fac_scr[RQ_] = cq * cost + swap_rows(cq) * sint
        fac_scr[RK_] = (ck * cost + swap_rows(ck) * sint) * (HEAD_DIM ** -0.5)
        fac_scr[RV_] = slot_t(8)

    @pl.when(step < N_HEADS)
    def _():
        base = pl.multiple_of(step * HEAD_DIM, HEAD_DIM)
        gam = [float(np.exp(g)) for g in _log_gamma()]
        gamma = jnp.float32(gam[-1])
        for h in range(N_HEADS - 2, -1, -1):
            gamma = jnp.where(step == h, jnp.float32(gam[h]), gamma)
        v_t = fac_scr[V_, pl.ds(base, HEAD_DIM), :]
        rv_t = fac_scr[RV_, pl.ds(base, HEAD_DIM), :]
        acc_a = jnp.zeros((HEAD_DIM, v_t.shape[1]), F32)
        acc_c = jnp.zeros((HEAD_DIM, v_t.shape[1]), F32)
        for k in range(HEAD_DIM):
            def row(which):
                return fac_scr[which, pl.ds(base + k, 1), :]
            s_new = row(F_) * sh_ref[k] + row(K_) * v_t
            nh_ref[k] = s_new
            acc_a = acc_a + row(Q_) * s_new
            r_new = gamma * sr_ref[k] + row(RK_) * rv_t
            nr_ref[k] = r_new
            acc_c = acc_c + row(RQ_) * r_new
        ot_scr[0, pl.ds(base, HEAD_DIM), :] = acc_a
        ot_scr[1, pl.ds(base, HEAD_DIM), :] = acc_c

    @pl.when(step == N_HEADS)
    def _():
        def vec(r):
            return vec_ref[r:r + 1, :]

        def slot(s):
            return u_ref[:, s * WIDTH:(s + 1) * WIDTH]

        def per_head(x, fn):
            return jnp.concatenate([fn(x[h * HEAD_DIM:(h + 1) * HEAD_DIM]) for h in range(N_HEADS)], axis=0)

        def rms_head(o):
            return o * lax.rsqrt(jnp.mean(o * o, axis=0, keepdims=True) + EPS)

        def group_head(o):
            dev = o - jnp.mean(o, axis=0, keepdims=True)
            return dev * lax.rsqrt(jnp.mean(dev * dev, axis=0, keepdims=True) + EPS)

        o_a = per_head(ot_scr[0], rms_head) * vcol(V_HNORM) * _silu(slot_t(3))
        oa_ref[...] = o_a.T
        o_c = per_head(ot_scr[1], group_head) * vcol(V_RNORM) * _silu(slot_t(9))
        oc_ref[...] = o_c.T

        ux = slot(4)
        xc = vec(V_CB) + vec(V_CW0 + CONV_WIDTH - 1) * ux
        for j in range(CONV_WIDTH - 1):
            xc = xc + vec(V_CW0 + j) * scv_ref[j]
        xc_bf = xc.astype(BF16)
        rg = _sigmoid(_dot(xc_bf, wr_ref[...]) + vec(V_BR))
        ig = _sigmoid(_dot(xc_bf, wi_ref[...]) + vec(V_BI))
        log_a = -LRU_C * rg * _softplus(-vec(V_LAM))
        b_in = jnp.sqrt(jnp.maximum(_neg_expm1(2.0 * log_a), 0.0)) * (ig * xc)
        hnew = jnp.exp(log_a) * sl_ref[...] + b_in
        nl_ref[...] = hnew
        ob_ref[...] = hnew * jax.nn.gelu(slot(5))
        for j in range(CONV_WIDTH - 2):
            ncv_ref[j] = scv_ref[j + 1]
        ncv_ref[CONV_WIDTH - 2] = ux

        ud = slot(10)
        wsum = ud
        sums = []
        nxt = 1
        for win in POOL_WINDOWS:
            while nxt < win:
                wsum = wsum + spl_ref[POOL_BUF - nxt]
                nxt += 1
            sums.append(wsum)
        lane = lax.broadcasted_iota(jnp.int32, (1, WIDTH), 1)
        grp = lane // (WIDTH // len(POOL_WINDOWS))
        sel = sums[-1]
        winl = jnp.full((1, WIDTH), float(POOL_WINDOWS[-1]), F32)
        for gi in range(len(POOL_WINDOWS) - 2, -1, -1):
            sel = jnp.where(grp == gi, sums[gi], sel)
            winl = jnp.where(grp == gi, float(POOL_WINDOWS[gi]), winl)
        pooled = sel / jnp.minimum(winl, float(pos0) + 1.0)
        od_ref[...] = _dot((pooled - ud).astype(BF16), wp_ref[...]) * vec(V_PSCALE)
        for j in range(POOL_BUF - 1):
            npl_ref[j] = spl_ref[j + 1]
        npl_ref[POOL_BUF - 1] = ud


def _sample_step(u, lb_logits, vecs, vecs_t, wr_bd, wi_bd, wp_bd, cos_t, sin_t, s_hgrn, s_lru, s_conv,
                 s_ret, s_pool, layer, *, pos0):
    nb = u.shape[0]
    depth = lb_logits.shape[0]
    cost = jnp.broadcast_to(cos_t.reshape(WIDTH, 1), (WIDTH, nb))
    sint = jnp.broadcast_to(sin_t.reshape(WIDTH, 1), (WIDTH, nb))

    def const(shape):
        return pl.BlockSpec(shape, lambda i: tuple(0 for _ in shape))

    def layer_block(shape):
        return pl.BlockSpec((None,) + shape, lambda i: (layer,) + tuple(0 for _ in shape))

    def head(i):
        return jnp.minimum(i, N_HEADS - 1)

    mat = (HEAD_DIM, HEAD_DIM, nb)
    mat_in = pl.BlockSpec((None, None) + mat, lambda i: (layer, head(i), 0, 0, 0))
    mat_out = pl.BlockSpec((None,) + mat, lambda i: (head(i), 0, 0, 0))
    rows = const((nb, WIDTH))
    return pl.pallas_call(
        functools.partial(_sample_step_kernel, layer=layer, pos0=pos0),
        grid=(N_HEADS + 1,),
        in_specs=[
            const((nb, MIX_COLS)), const((WIDTH, depth)),
            layer_block((N_VEC_ROWS, WIDTH)), layer_block((WIDTH, N_VEC_ROWS)),
            layer_block((WIDTH, WIDTH)), layer_block((WIDTH, WIDTH)), layer_block((WIDTH, WIDTH)),
            const((WIDTH, nb)), const((WIDTH, nb)),
            mat_in, layer_block((nb, WIDTH)), layer_block((CONV_WIDTH - 1, nb, WIDTH)), mat_in,
            layer_block((POOL_BUF, nb, WIDTH)),
        ],
        out_specs=[rows, rows, rows, rows, mat_out, rows, const((CONV_WIDTH - 1, nb, WIDTH)), mat_out,
                   const((POOL_BUF, nb, WIDTH))],
        out_shape=[jax.ShapeDtypeStruct((nb, WIDTH), F32)] * 4 + [
            jax.ShapeDtypeStruct((N_HEADS,) + mat, F32), jax.ShapeDtypeStruct((nb, WIDTH), F32),
            jax.ShapeDtypeStruct((CONV_WIDTH - 1, nb, WIDTH), F32), jax.ShapeDtypeStruct((N_HEADS,) + mat, F32),
            jax.ShapeDtypeStruct((POOL_BUF, nb, WIDTH), F32),
        ],
        scratch_shapes=[pltpu.VMEM((7, WIDTH, nb), F32), pltpu.VMEM((2, WIDTH, nb), F32)],
        compiler_params=_cparams("arbitrary"),
        name="sample_step",
    )(u, lb_logits.T, vecs, vecs_t, wr_bd, wi_bd, wp_bd, cost, sint, s_hgrn, s_lru, s_conv, s_ret, s_pool)


def _pick_tile(n, pref):
    t = min(n, pref)
    while n % t:
        t //= 2
    return t


def kernel(x_prompt, x_sample, state_hgrn, state_rglru, state_conv, state_retention, state_pool, lb_logits, ffn1_norm, ffn1_up, ffn1_down, mix_norm, w_in, hgrn_norm, conv_w, conv_b, w_rgate, b_rgate, w_igate, b_igate, lru_lambda, ret_norm, w_pool, pool_scale, w_branch, w_o, ffn2_norm, ffn2_up, ffn2_down, final_norm):
    batch, seq, d = x_prompt.shape
    nb, dec_seq, _ = x_sample.shape
    assert dec_seq == 1
    depth = w_in.shape[0]
    past_len = PAST_LEN

    w_in_bf = w_in.astype(BF16)
    w_br, w_out = w_branch.astype(BF16), w_o.astype(BF16)
    wr_bd = jax.vmap(_block_diag)(w_rgate).astype(BF16)
    wi_bd = jax.vmap(_block_diag)(w_igate).astype(BF16)
    wp_bd = jax.vmap(_block_diag)(w_pool).astype(BF16)
    vecs = jnp.concatenate(
        [hgrn_norm[:, None], conv_w, conv_b[:, None], b_rgate[:, None], b_igate[:, None],
         lru_lambda[:, None], ret_norm[:, None], pool_scale[:, None],
         jnp.zeros((depth, N_VEC_ROWS - 11, WIDTH), F32)], axis=1)
    vecs_t = jnp.swapaxes(vecs, 1, 2)
    hgrn_t = jnp.transpose(state_hgrn, (0, 2, 3, 4, 1))
    ret_t = jnp.transpose(state_retention, (0, 2, 3, 4, 1))
    conv_t = jnp.transpose(state_conv, (0, 2, 1, 3))
    pool_t = jnp.transpose(state_pool, (0, 2, 1, 3))
    n1 = ffn1_norm[:, None, :]
    n2 = ffn2_norm[:, None, :]
    nm = mix_norm[:, None, :]
    fg = final_norm[None, :]

    cos_p, sin_p = _rope_tables(jnp.arange(seq, dtype=F32))
    cos_s, sin_s = _rope_tables(past_len + jnp.arange(dec_seq, dtype=F32))

    xp = x_prompt.reshape(batch * seq, d)
    xs = x_sample.reshape(nb * dec_seq, d)
    tm_p = _pick_tile(batch * seq, DENSE_ROWS)
    tm_w = _pick_tile(batch * seq, WIDE_ROWS)
    tc = _pick_tile(seq, MIXER_ROWS)

    st_p = [[] for _ in range(5)]
    st_s = [[] for _ in range(5)]
    for l in range(depth):
        last = l == depth - 1
        xp, xs = _ffn(xp, xs, n1, ffn1_up, ffn1_down, l, fg, final_norm=False, tm=tm_p)
        u_p, u_s = _inproj(xp, xs, nm, w_in_bf, l, tm=tm_w)
        *br_p, sh, sl, scv, sr, spl = _mixer_prompt(
            u_p, lb_logits, vecs, wr_bd, wi_bd, wp_bd, cos_p, sin_p, l, batch=batch, seq=seq, tc=tc, pos0=0)
        *br_s, nh, nl, ncv, nr, npl = _sample_step(
            u_s, lb_logits, vecs, vecs_t, wr_bd, wi_bd, wp_bd, cos_s, sin_s, hgrn_t, state_rglru,
            conv_t, ret_t, pool_t, l, pos0=past_len)
        xp, xs = _merge(xp, xs, nm, br_p, br_s, w_in_bf, w_br, w_out, l, tm=tm_w)
        xp, xs = _ffn(xp, xs, n2, ffn2_up, ffn2_down, l, fg, final_norm=last, tm=tm_p)
        for lst, s in zip(st_p, (jnp.swapaxes(_diag_blocks(sh), 2, 3), sl[:, 0], scv, _diag_blocks(sr), spl)):
            lst.append(s)
        for lst, s in zip(st_s, (nh, nl, ncv, nr, npl)):
            lst.append(s)

    y_p = xp.reshape(batch, seq, d)
    y_s = xs.reshape(nb, dec_seq, d)
    nh, nl, ncv, nr, npl = (jnp.stack(s) for s in st_s)
    sample_states = (jnp.transpose(nh, (0, 4, 1, 2, 3)), nl, jnp.transpose(ncv, (0, 2, 1, 3)),
                     jnp.transpose(nr, (0, 4, 1, 2, 3)), jnp.transpose(npl, (0, 2, 1, 3)))
    return (y_p, y_s) + tuple(jnp.stack(s) for s in st_p) + sample_states
```

```python
import functools

import numpy as np
import jax
import jax.numpy as jnp
from jax import lax
from jax.experimental import pallas as pl
from jax.experimental.pallas import tpu as pltpu

F32 = jnp.float32
BF16 = jnp.bfloat16

N_HEADS = 4
HEAD_DIM = 64
WIDTH = N_HEADS * HEAD_DIM
N_MIX_SLOTS = 11
MIX_COLS = N_MIX_SLOTS * WIDTH
CHUNK = 64
N_LEVELS = 6
PAD = 16
CONV_WIDTH = 4
LRU_C = 8.0
POOL_WINDOWS = (2, 4, 8, 16)
POOL_BUF = 15
ROPE_BASE = 10000.0
EPS = 1e-6
F_FLOOR = 1e-30
SAFE_LOG = 80.0
PAST_LEN = 16384
VMEM_LIMIT = 56 * 1024 * 1024
DENSE_ROWS = 1024
WIDE_ROWS = 1024
MERGE_COLS = 256
FFN_UP_CHUNK = 512
FFN_DOWN_CHUNK = 256
MIXER_ROWS = 1024


def _cparams(*sem):
    return pltpu.CompilerParams(dimension_semantics=sem, vmem_limit_bytes=VMEM_LIMIT)


def _dot(a, b):
    return jnp.dot(a, b, preferred_element_type=F32)


def _dot_nt(a, b):
    return lax.dot_general(a, b, (((1,), (1,)), ((), ())), preferred_element_type=F32)


def _dot_tn(a, b):
    return lax.dot_general(a, b, (((0,), (0,)), ((), ())), preferred_element_type=F32)


def _rms(x, g):
    return x * lax.rsqrt(jnp.mean(x * x, axis=-1, keepdims=True) + EPS) * g


def _sigmoid(x):
    return 0.5 * jnp.tanh(0.5 * x) + 0.5


def _silu(x):
    hx = 0.5 * x
    return hx * jnp.tanh(hx) + hx


def _softplus(x):
    return jnp.maximum(x, 0.0) + jnp.log1p(jnp.exp(-jnp.abs(x)))


def _neg_expm1(y):
    t = jnp.tanh(0.5 * y)
    return -2.0 * t / (1.0 - t)


def _split2(x):
    hi = x.astype(BF16)
    lo = (x - hi.astype(F32)).astype(BF16)
    return hi, lo


def _split3(x):
    hi = x.astype(BF16)
    r = x - hi.astype(F32)
    mid = r.astype(BF16)
    lo = (r - mid.astype(F32)).astype(BF16)
    return hi, mid, lo


def _lower_bound(lbl, layer):
    m = jnp.max(lbl, axis=0, keepdims=True)
    e = jnp.exp(lbl - m)
    soft = e / jnp.sum(e, axis=0, keepdims=True)
    acc = soft[0:1]
    for l in range(1, layer + 1):
        acc = acc + soft[l:l + 1]
    return acc - soft[0:1]


def _resident(shape, index_map):
    return pl.BlockSpec(shape, index_map, pipeline_mode=pl.Buffered(1))


def _two_group_kernel(*refs, body, n_rows, n_tiles):
    p_refs, s_refs = refs[:n_rows], refs[n_rows:2 * n_rows]
    params, (op_ref, os_ref) = refs[2 * n_rows:-2], refs[-2:]
    i = pl.program_id(0)

    @pl.when(i < n_tiles)
    def _():
        op_ref[...] = body([r[...] for r in p_refs], params)

    @pl.when(i == n_tiles)
    def _():
        os_ref[...] = body([r[...] for r in s_refs], params)


def _two_group_call(body, rows_p, rows_s, params, param_specs, out_cols, tm, name):
    n_p, n_s = rows_p[0].shape[0], rows_s[0].shape[0]
    n_tiles = n_p // tm

    def tile(i):
        return (jnp.minimum(i, n_tiles - 1), 0)

    return pl.pallas_call(
        functools.partial(_two_group_kernel, body=body, n_rows=len(rows_p), n_tiles=n_tiles),
        grid=(n_tiles + 1,),
        in_specs=[pl.BlockSpec((tm, a.shape[1]), tile) for a in rows_p]
        + [pl.BlockSpec(a.shape, lambda i: (0, 0)) for a in rows_s] + list(param_specs),
        out_specs=[pl.BlockSpec((tm, out_cols), tile), pl.BlockSpec((n_s, out_cols), lambda i: (0, 0))],
        out_shape=[jax.ShapeDtypeStruct((n_p, out_cols), F32), jax.ShapeDtypeStruct((n_s, out_cols), F32)],
        compiler_params=_cparams("arbitrary"),
        name=name,
    )(*rows_p, *rows_s, *params)


def _ffn_body(rows, params, *, final_norm):
    (x,), (g_ref, wu_ref, wd_ref, fg_ref) = rows, params
    dff = wd_ref.shape[0]
    h = _rms(x, g_ref[...]).astype(BF16)
    a = _dot(h, wu_ref[:, :dff])
    b = _dot(h, wu_ref[:, dff:])
    act = (_silu(a) * b).astype(BF16)
    y = x + 0.5 * _dot(act, wd_ref[...])
    return _rms(y, fg_ref[...]) if final_norm else y


def _cast_chunks(n_chunks, src_chunk, dst_chunk, stage, sem):
    def copy(c):
        return pltpu.make_async_copy(src_chunk(c), stage.at[c % 2], sem.at[c % 2])

    copy(0).start()
    for c in range(n_chunks):
        if c + 1 < n_chunks:
            copy(c + 1).start()
        copy(c).wait()
        dst_chunk(c)[...] = stage[c % 2].astype(BF16)


def _ffn_kernel(xp_ref, xs_ref, g_ref, wu_hbm, wd_hbm, fg_ref, op_ref, os_ref,
                wu_bf, wd_bf, stage_u, stage_d, sem_u, sem_d, *, layer, n_tiles, final_norm):
    i = pl.program_id(0)
    cw, rw = stage_u.shape[2], stage_d.shape[1]

    @pl.when(i == 0)
    def _():
        _cast_chunks(wu_bf.shape[1] // cw,
                     lambda c: wu_hbm.at[layer, :, pl.ds(c * cw, cw)],
                     lambda c: wu_bf.at[:, pl.ds(c * cw, cw)], stage_u, sem_u)
        _cast_chunks(wd_bf.shape[0] // rw,
                     lambda c: wd_hbm.at[layer, pl.ds(c * rw, rw), :],
                     lambda c: wd_bf.at[pl.ds(c * rw, rw), :], stage_d, sem_d)

    body = functools.partial(_ffn_body, params=(g_ref, wu_bf, wd_bf, fg_ref), final_norm=final_norm)

    @pl.when(i < n_tiles)
    def _():
        half = op_ref.shape[0] // 2
        for r in range(2):
            op_ref[r * half:(r + 1) * half, :] = body([xp_ref[r * half:(r + 1) * half, :]])

    @pl.when(i == n_tiles)
    def _():
        os_ref[...] = body([xs_ref[...]])


def _ffn(xp, xs, g, w_up, w_down, layer, final_g, *, final_norm, tm):
    (n_p, d), n_s = xp.shape, xs.shape[0]
    dff = w_down.shape[1]
    n_tiles = n_p // tm
    assert (2 * dff) % FFN_UP_CHUNK == 0 and dff % FFN_DOWN_CHUNK == 0

    def tile(i):
        return (jnp.minimum(i, n_tiles - 1), 0)

    return pl.pallas_call(
        functools.partial(_ffn_kernel, layer=layer, n_tiles=n_tiles, final_norm=final_norm),
        grid=(n_tiles + 1,),
        in_specs=[
            pl.BlockSpec((tm, d), tile),
            pl.BlockSpec((n_s, d), lambda i: (0, 0)),
            _resident((None, 1, d), lambda i: (layer, 0, 0)),
            pl.BlockSpec(memory_space=pl.ANY),
            pl.BlockSpec(memory_space=pl.ANY),
            _resident((1, d), lambda i: (0, 0)),
        ],
        out_specs=[pl.BlockSpec((tm, d), tile), pl.BlockSpec((n_s, d), lambda i: (0, 0))],
        out_shape=[jax.ShapeDtypeStruct((n_p, d), F32), jax.ShapeDtypeStruct((n_s, d), F32)],
        scratch_shapes=[
            pltpu.VMEM((d, 2 * dff), BF16), pltpu.VMEM((dff, d), BF16),
            pltpu.VMEM((2, d, FFN_UP_CHUNK), F32), pltpu.VMEM((2, FFN_DOWN_CHUNK, d), F32),
            pltpu.SemaphoreType.DMA((2,)), pltpu.SemaphoreType.DMA((2,)),
        ],
        compiler_params=_cparams("arbitrary"),
        name="ffn",
    )(xp, xs, g, w_up, w_down, final_g)


def _inproj_body(rows, params):
    (x,), (g_ref, w_ref) = rows, params
    return _dot(_rms(x, g_ref[...]).astype(BF16), w_ref[...])


def _inproj(xp, xs, g, w_in, layer, *, tm):
    d = xp.shape[1]
    specs = [
        _resident((None, 1, d), lambda i: (layer, 0, 0)),
        _resident((None, d, MIX_COLS), lambda i: (layer, 0, 0)),
    ]
    return _two_group_call(_inproj_body, [xp], [xs], (g, w_in), specs, MIX_COLS, tm, "inproj")


def _merge_body(rows, params):
    x, branches = rows[0], rows[1:]
    g_ref, win_ref, wb_ref, wo_ref = params
    d = x.shape[1]
    h = _rms(x, g_ref[...]).astype(BF16)
    o_bf = [o_b.astype(BF16) for o_b in branches]
    blocks = []
    for c0 in range(0, d, MERGE_COLS):
        acc = None
        for b in range(len(branches)):
            gcol = MIX_COLS + b * d + c0
            gate = _sigmoid(_dot(h, win_ref[:, gcol:gcol + MERGE_COLS]))
            y = gate * _dot(o_bf[b], wb_ref[b, :, c0:c0 + MERGE_COLS])
            acc = y if acc is None else acc + y
        blocks.append(acc.astype(BF16))
    return x + _dot(jnp.concatenate(blocks, axis=1), wo_ref[...])


def _merge(xp, xs, g, branches_p, branches_s, w_in, w_branch, w_o, layer, *, tm):
    d = xp.shape[1]
    w = branches_p[0].shape[1]
    nb = len(branches_p)
    assert w_in.shape[2] == MIX_COLS + nb * d and MIX_COLS % 128 == 0
    specs = [
        _resident((None, 1, d), lambda i: (layer, 0, 0)),
        _resident((None, d, w_in.shape[2]), lambda i: (layer, 0, 0)),
        _resident((None, nb, w, d), lambda i: (layer, 0, 0, 0)),
        _resident((None, d, d), lambda i: (layer, 0, 0)),
    ]
    return _two_group_call(_merge_body, [xp, *branches_p], [xs, *branches_s],
                           (g, w_in, w_branch, w_o), specs, d, tm, "merge")


def _head_block_ones():
    h = np.arange(WIDTH) // HEAD_DIM
    return (h[:, None] == h[None, :]).astype(np.float32)


def _log_gamma():
    return np.log1p(-(2.0 ** (-5.0 - np.arange(N_HEADS, dtype=np.float64))))


def _retention_tables():
    lg = np.repeat(_log_gamma(), HEAD_DIM)[None, :]
    t = np.arange(CHUNK, dtype=np.float64)[:, None]
    q_dec = np.exp((t + 1.0) * lg)
    k_dec = np.exp((CHUNK - 1.0 - t) * lg)
    s_dec = np.exp(CHUNK * lg)
    dt = t.T - t
    dmat = np.concatenate(
        [np.where(dt >= 0, np.exp(dt * g), 0.0) for g in _log_gamma()], axis=1)
    return (jnp.asarray(q_dec, F32), jnp.asarray(k_dec, F32), jnp.asarray(s_dec, F32),
            jnp.asarray(dmat, F32))


def _level_tables():
    t = np.arange(CHUNK)[:, None]
    r = np.arange(CHUNK)[None, :]
    blocks = [r <= t, r > t]
    pairs = []
    for i in range(N_LEVELS):
        m = 1 << i
        mid = t - (t % (2 * m)) + m
        up = (t % (2 * m)) >= m
        blocks.append(np.where(up, (r >= mid) & (r <= t), (r > t) & (r <= mid - 1)))
        same = (t // (2 * m)) == (r // (2 * m))
        pr = same & ((t % (2 * m)) < m) & ((r % (2 * m)) >= m)
        pairs.append(np.tile(pr, (1, N_HEADS)))
    dsum = np.tile(np.concatenate(blocks, axis=0), (1, 3))
    causal = np.tile(t < r, (1, N_HEADS))
    assert (np.sum(np.stack(pairs), axis=0) == causal).all()
    return jnp.asarray(dsum, BF16), jnp.asarray(np.stack(pairs), F32), jnp.asarray(causal, F32)


def _rope_tables(pos):
    half = HEAD_DIM // 2
    freq = ROPE_BASE ** (-jnp.arange(half, dtype=F32) / half)
    ang = pos[:, None] * freq[None, :]
    cos, sin = jnp.cos(ang), jnp.sin(ang)
    cos_h = jnp.concatenate([cos, cos], axis=-1)
    sin_h = jnp.concatenate([-sin, sin], axis=-1)
    return jnp.tile(cos_h, (1, N_HEADS)), jnp.tile(sin_h, (1, N_HEADS))


def _block_diag(w):
    h, dh, _ = w.shape
    eye = jnp.eye(h, dtype=w.dtype)
    return (eye[:, None, :, None] * w[:, :, None, :]).reshape(h * dh, h * dh)


V_HNORM, V_CW0, V_CB, V_BR, V_BI, V_LAM, V_RNORM, V_PSCALE = 0, 1, 5, 6, 7, 8, 9, 10
N_VEC_ROWS = 16


def _swap_halves(x):
    half = HEAD_DIM // 2
    lane = lax.broadcasted_iota(jnp.int32, (1, 128), 1)
    first = (lane % HEAD_DIM) < half
    parts = []
    for c in range(x.shape[1] // 128):
        xc = x[:, c * 128:(c + 1) * 128]
        parts.append(jnp.where(first, pltpu.roll(xc, 128 - half, 1), pltpu.roll(xc, half, 1)))
    return jnp.concatenate(parts, axis=1)


def _mixer_prompt_kernel(u_ref, lbl_ref, vec_ref, wr_ref, wi_ref, wp_ref, cos_ref, sin_ref,
                         ind_ref, dsum_ref, pair_ref, causal_ref, qdec_ref, kdec_ref, sdec_ref, dmat_ref,
                         oa_ref, ob_ref, oc_ref, od_ref, sh_ref, sl_ref, scv_ref, sr_ref, spl_ref,
                         hst_scr, rst_scr, hl_scr, xcv_scr, xpl_scr, oh_scr, orr_scr, sct_scr,
                         *, layer, tc, pos0):
    t_idx = pl.program_id(1)
    nchunk = tc // CHUNK

    @pl.when(t_idx == 0)
    def _():
        hst_scr[...] = jnp.zeros_like(hst_scr)
        rst_scr[...] = jnp.zeros_like(rst_scr)
        hl_scr[...] = jnp.zeros_like(hl_scr)
        xcv_scr[...] = jnp.zeros_like(xcv_scr)
        xpl_scr[0:PAD, :] = jnp.zeros((PAD, WIDTH), F32)

    def slot(s):
        return u_ref[:, s * WIDTH:(s + 1) * WIDTH]

    def vec(r):
        return vec_ref[r:r + 1, :]

    ind = ind_ref[...]
    lane = lax.broadcasted_iota(jnp.int32, (1, WIDTH), 1)
    hm_bf = [(lane // HEAD_DIM == h).astype(F32).astype(BF16) for h in range(N_HEADS)]
    lane_t = lax.broadcasted_iota(jnp.int32, (1, 128), 1)
    row = lax.broadcasted_iota(jnp.int32, (tc, WIDTH), 0)
    chunks = [slice(n * CHUNK, (n + 1) * CHUNK) for n in range(nchunk)]

    def head_scores(keys, queries):
        stacked = jnp.concatenate([queries * hm_bf[h] for h in range(N_HEADS)], axis=0)
        return _dot_nt(keys, stacked)

    def scores_times_values(sct, val):
        assert HEAD_DIM == 64 and WIDTH % 128 == 0
        tiles = []
        for c in range(WIDTH // 128):
            pair = _dot_tn(sct[:, 2 * c * CHUNK:(2 * c + 2) * CHUNK], val[:, c * 128:(c + 1) * 128])
            tiles.append(jnp.where(lane_t < HEAD_DIM, pair[0:CHUNK], pair[CHUNK:2 * CHUNK]))
        return jnp.concatenate(tiles, axis=1)

    n_lt = WIDTH // 128
    ind_t = ind[0:128, 0:128]

    def lane_tile(x, c):
        return x[:, c * 128:(c + 1) * 128]

    def tile_outer(a, b):
        return [_dot_tn(lane_tile(a, c), lane_tile(b, c)) for c in range(n_lt)]

    ind2 = jnp.concatenate([ind, ind], axis=0)

    def head_sum(x):
        return _dot(jnp.concatenate(_split2(x), axis=1), ind2)

    lb = _lower_bound(lbl_ref[...], layer)
    uq, z, v = slot(0), slot(1), slot(2)
    q = _silu(uq)
    sg = _sigmoid(z)
    f = lb + (1.0 - lb) * sg
    logf = jnp.log(jnp.maximum(f, F_FLOOR))
    k = (1.0 - lb) * (1.0 - sg)
    l_hi, l_mid, l_lo = _split3(logf)
    v_bf = v.astype(BF16)
    dsum = dsum_ref[...]
    rowc = lax.broadcasted_iota(jnp.int32, (CHUNK, WIDTH), 0)
    upper = [((rowc >> i) & 1) == 1 for i in range(N_LEVELS)]
    lsplit = [jnp.concatenate([l_hi[rs], l_mid[rs], l_lo[rs]], axis=0) for rs in chunks]
    logs = [_dot(dsum[0:2 * CHUNK], lsplit[n]) for n in range(nchunk)]
    dec = [jnp.exp(jnp.minimum(lg, 0.0)) for lg in logs]
    qe = [(q[rs] * dec[n][0:CHUNK]).astype(BF16) for n, rs in enumerate(chunks)]
    total_decay = logs[0][CHUNK - 1:CHUNK]
    for lg in logs[1:]:
        total_decay = jnp.minimum(total_decay, lg[CHUNK - 1:CHUNK])
    bounded = jnp.min(total_decay) > -SAFE_LOG

    @pl.when(bounded)
    def _():
        for n, rs in enumerate(chunks):
            k_up = (k[rs] * jnp.exp(-logs[n][0:CHUNK])).astype(BF16)
            sct_scr[rs, :] = (head_scores(k_up, qe[n]) * causal_ref[...]).astype(BF16)

    @pl.when(jnp.logical_not(bounded))
    def _():
        sct = [None] * nchunk
        for n, rs in enumerate(chunks):
            lev = jnp.exp(jnp.minimum(_dot(dsum[2 * CHUNK:], lsplit[n]), 0.0))
            for i in range(N_LEVELS):
                zz = (jnp.where(upper[i], q[rs], k[rs]) * lev[i * CHUNK:(i + 1) * CHUNK]).astype(BF16)
                term = head_scores(zz, zz) * pair_ref[i]
                sct[n] = term if sct[n] is None else sct[n] + term
            sct_scr[rs, :] = sct[n].astype(BF16)

    o_diag = _dot((q * k).astype(BF16), ind) * v
    upd = [tile_outer(v_bf[rs], (k[rs] * dec[n][CHUNK:2 * CHUNK]).astype(BF16)) for n, rs in enumerate(chunks)]
    o_intra = [scores_times_values(sct_scr[rs, :], v_bf[rs]) for rs in chunks]

    seg_len = tc // 8
    ux_seg = jnp.swapaxes(slot(4).reshape(8, seg_len, WIDTH), 0, 1)
    slabs = [ux_seg[i] for i in range(seg_len)]
    sub0 = lax.broadcasted_iota(jnp.int32, (8, WIDTH), 0) == 0
    front = [jnp.where(sub0, xcv_scr[8 + k:9 + k, :], pltpu.roll(slabs[seg_len + k], 1, 0))
             for k in range(-(CONV_WIDTH - 1), 0)]
    ext = front + slabs
    cw = [vec(V_CW0 + j) for j in range(CONV_WIDTH)]
    xc = jnp.concatenate(
        [vec(V_CB) + sum(cw[j] * ext[i + j] for j in range(CONV_WIDTH)) for i in range(seg_len)], axis=0)
    xc_bf = xc.astype(BF16)
    rg = _sigmoid(_dot(xc_bf, wr_ref[...]) + vec(V_BR))
    ig = _sigmoid(_dot(xc_bf, wi_ref[...]) + vec(V_BI))
    log_a = -LRU_C * rg * _softplus(-vec(V_LAM))
    a_all = jnp.exp(log_a)
    b_all = jnp.sqrt(jnp.maximum(_neg_expm1(2.0 * log_a), 0.0)) * (ig * xc)
    h_loc, a_loc = [b_all[0:8]], [a_all[0:8]]
    for i in range(1, seg_len):
        a_i = a_all[8 * i:8 * i + 8]
        h_loc.append(a_i * h_loc[-1] + b_all[8 * i:8 * i + 8])
        a_loc.append(a_i * a_loc[-1])
    carry = [hl_scr[0:1, :]]
    for s in range(8):
        carry.append(a_loc[-1][s:s + 1] * carry[-1] + h_loc[-1][s:s + 1])
    hl_scr[0:1, :] = carry[8]
    carry8 = jnp.concatenate(carry[0:8], axis=0)
    hperm = jnp.concatenate([h_loc[i] + a_loc[i] * carry8 for i in range(seg_len)], axis=0)
    hseq = jnp.swapaxes(hperm.reshape(seg_len, 8, WIDTH), 0, 1).reshape(tc, WIDTH)
    ob_ref[...] = (hseq * jax.nn.gelu(slot(5))).astype(ob_ref.dtype)
    xcv_scr[...] = u_ref[tc - 8:tc, 4 * WIDTH:5 * WIDTH]

    cosv, sinv = cos_ref[...], sin_ref[...]
    cq, ck = slot(6), slot(7)
    qr = cq * cosv + _swap_halves(cq) * sinv
    kr = (ck * cosv + _swap_halves(ck) * sinv) * (HEAD_DIM ** -0.5)
    rv_bf = slot(8).astype(BF16)
    qdec, kdec, sdec, dmat = qdec_ref[...], kdec_ref[...], sdec_ref[...], dmat_ref[...]
    qr_bf, kr_bf = qr.astype(BF16), kr.astype(BF16)
    rsc = []
    for rs in chunks:
        rsc.append((head_scores(kr_bf[rs], qr_bf[rs]) * dmat).astype(BF16))
    r_intra = [scores_times_values(rsc[n], rv_bf[rs]) for n, rs in enumerate(chunks)]
    rupd = [tile_outer((kr[rs] * kdec).astype(BF16), rv_bf[rs]) for rs in chunks]

    ud = slot(10)
    xpl_scr[PAD:PAD + tc, :] = ud
    assert POOL_WINDOWS == (2, 4, 8, 16) and WIDTH // len(POOL_WINDOWS) == 64 and PAD >= POOL_BUF
    halves = []
    for c in range(2):
        acc = xpl_scr[:, c * 128:(c + 1) * 128]
        built = {}
        for sh in (1, 2, 4, 8):
            acc = acc + pltpu.roll(acc, sh, 0)
            built[2 * sh] = acc
        lo, hi = POOL_WINDOWS[2 * c], POOL_WINDOWS[2 * c + 1]
        halves.append(jnp.where(lane_t < 64, built[lo], built[hi])[PAD:PAD + tc])
    sel = jnp.concatenate(halves, axis=1)
    grp = lane // (WIDTH // len(POOL_WINDOWS))
    winl = jnp.full((1, WIDTH), float(POOL_WINDOWS[-1]), F32)
    for gi in range(len(POOL_WINDOWS) - 2, -1, -1):
        winl = jnp.where(grp == gi, float(POOL_WINDOWS[gi]), winl)
    pos = (row + t_idx * tc).astype(F32) + float(pos0)
    pooled = sel / jnp.minimum(winl, pos + 1.0)
    od = _dot((pooled - ud).astype(BF16), wp_ref[...]) * vec(V_PSCALE)
    od_ref[...] = od.astype(od_ref.dtype)
    xpl_scr[0:PAD, :] = xpl_scr[tc:tc + PAD, :]

    st = [hst_scr[c] for c in range(n_lt)]
    s_ret = [rst_scr[c] for c in range(n_lt)]
    for n, rs in enumerate(chunks):
        o_c = jnp.concatenate(
            [_dot_nt(lane_tile(qe[n], c), st[c].astype(BF16) * ind_t) for c in range(n_lt)], axis=1)
        st = [st[c] * lane_tile(dec[n][CHUNK - 1:CHUNK], c) + upd[n][c] for c in range(n_lt)]
        oh_scr[rs, :] = o_c + o_diag[rs] + o_intra[n]
        rqe = (qr[rs] * qdec).astype(BF16)
        o_c = jnp.concatenate(
            [_dot(lane_tile(rqe, c), s_ret[c].astype(BF16) * ind_t) for c in range(n_lt)], axis=1)
        s_ret = [s_ret[c] * lane_tile(sdec, c) + rupd[n][c] for c in range(n_lt)]
        orr_scr[rs, :] = o_c + r_intra[n]
    for c in range(n_lt):
        hst_scr[c] = st[c]
        rst_scr[c] = s_ret[c]
    o_h = oh_scr[...]
    ms = head_sum(o_h * o_h) * (1.0 / HEAD_DIM)
    oa_ref[...] = (o_h * lax.rsqrt(ms + EPS) * vec(V_HNORM) * _silu(slot(3))).astype(oa_ref.dtype)
    o_r = orr_scr[...]
    mu = head_sum(o_r) * (1.0 / HEAD_DIM)
    dev = o_r - mu
    var = head_sum(dev * dev) * (1.0 / HEAD_DIM)
    oc_ref[...] = (dev * lax.rsqrt(var + EPS) * vec(V_RNORM) * _silu(slot(9))).astype(oc_ref.dtype)

    @pl.when(t_idx == pl.num_programs(1) - 1)
    def _():
        sh_ref[...] = hst_scr[...]
        sr_ref[...] = rst_scr[...]
        sl_ref[...] = hl_scr[0:1, :]
        scv_ref[...] = xcv_scr[8 - (CONV_WIDTH - 1):8, :]
        spl_ref[...] = xpl_scr[PAD + tc - POOL_BUF:PAD + tc, :]


def _mixer_prompt(u, lb_logits, vecs, wr_bd, wi_bd, wp_bd, cos_t, sin_t, layer, *, batch, seq, tc, pos0):
    nt = seq // tc
    ind = jnp.asarray(_head_block_ones(), BF16)
    dsum, pair, causal = _level_tables()
    qdec, kdec, sdec, dmat = _retention_tables()
    depth = lb_logits.shape[0]

    def const(shape):
        return pl.BlockSpec(shape, lambda b, t: tuple(0 for _ in shape))

    row_spec = pl.BlockSpec((tc, WIDTH), lambda b, t: (b * nt + t, 0))
    tab_spec = pl.BlockSpec((tc, WIDTH), lambda b, t: (t, 0))

    def state_spec(rows):
        return pl.BlockSpec((None, rows, WIDTH), lambda b, t: (b, 0, 0))

    mat_tiles = (WIDTH // 128, 128, 128)
    mat_spec = pl.BlockSpec((None,) + mat_tiles, lambda b, t: (b, 0, 0, 0))
    n = batch * seq
    outs = pl.pallas_call(
        functools.partial(_mixer_prompt_kernel, layer=layer, tc=tc, pos0=pos0),
        grid=(batch, nt),
        in_specs=[
            pl.BlockSpec((tc, MIX_COLS), lambda b, t: (b * nt + t, 0)),
            const((depth, WIDTH)),
            pl.BlockSpec((None, N_VEC_ROWS, WIDTH), lambda b, t: (layer, 0, 0)),
            pl.BlockSpec((None, WIDTH, WIDTH), lambda b, t: (layer, 0, 0)),
            pl.BlockSpec((None, WIDTH, WIDTH), lambda b, t: (layer, 0, 0)),
            pl.BlockSpec((None, WIDTH, WIDTH), lambda b, t: (layer, 0, 0)),
            tab_spec, tab_spec,
            const((WIDTH, WIDTH)), const(dsum.shape), const(pair.shape), const(causal.shape),
            const((CHUNK, WIDTH)), const((CHUNK, WIDTH)), const((1, WIDTH)),
            const((CHUNK, N_HEADS * CHUNK)),
        ],
        out_specs=[row_spec, row_spec, row_spec, row_spec,
                   mat_spec, state_spec(1), state_spec(CONV_WIDTH - 1), mat_spec,
                   state_spec(POOL_BUF)],
        out_shape=[jax.ShapeDtypeStruct((n, WIDTH), BF16)] * 4 + [
            jax.ShapeDtypeStruct((batch,) + mat_tiles, F32),
            jax.ShapeDtypeStruct((batch, 1, WIDTH), F32),
            jax.ShapeDtypeStruct((batch, CONV_WIDTH - 1, WIDTH), F32),
            jax.ShapeDtypeStruct((batch,) + mat_tiles, F32),
            jax.ShapeDtypeStruct((batch, POOL_BUF, WIDTH), F32),
        ],
        scratch_shapes=[
            pltpu.VMEM(mat_tiles, F32), pltpu.VMEM(mat_tiles, F32), pltpu.VMEM((8, WIDTH), F32),
            pltpu.VMEM((8, WIDTH), F32), pltpu.VMEM((PAD + tc, WIDTH), F32),
            pltpu.VMEM((tc, WIDTH), F32), pltpu.VMEM((tc, WIDTH), F32),
            pltpu.VMEM((tc, N_HEADS * CHUNK), BF16),
        ],
        compiler_params=_cparams("parallel", "arbitrary"),
        name="mixer_prompt",
    )(u, lb_logits, vecs, wr_bd, wi_bd, wp_bd, cos_t, sin_t, ind, dsum, pair, causal, qdec, kdec, sdec, dmat)
    return outs


def _diag_blocks(s):
    per_tile = 128 // HEAD_DIM
    return jnp.stack([s[:, h // per_tile,
                        (h % per_tile) * HEAD_DIM:(h % per_tile + 1) * HEAD_DIM,
                        (h % per_tile) * HEAD_DIM:(h % per_tile + 1) * HEAD_DIM]
                      for h in range(N_HEADS)], axis=1)


def _sample_step_kernel(u_ref, lblt_ref, vec_ref, vect_ref, wr_ref, wi_ref, wp_ref, cost_ref, sint_ref,
                        sh_ref, sl_ref, scv_ref, sr_ref, spl_ref,
                        oa_ref, ob_ref, oc_ref, od_ref, nh_ref, nl_ref, ncv_ref, nr_ref, npl_ref,
                        fac_scr, ot_scr, *, layer, pos0):
    step = pl.program_id(0)
    F_, K_, Q_, V_, RQ_, RK_, RV_ = range(7)

    def slot_t(s):
        return u_ref[:, s * WIDTH:(s + 1) * WIDTH].T

    def vcol(r):
        return vect_ref[:, r:r + 1]

    @pl.when(step == 0)
    def _():
        lblt = lblt_ref[...]
        e = jnp.exp(lblt - jnp.max(lblt, axis=1, keepdims=True))
        soft = e / jnp.sum(e, axis=1, keepdims=True)
        acc = soft[:, 0:1]
        for l in range(1, layer + 1):
            acc = acc + soft[:, l:l + 1]
        lb = acc - soft[:, 0:1]
        sg = _sigmoid(slot_t(1))
        fac_scr[F_] = jnp.maximum(lb + (1.0 - lb) * sg, F_FLOOR)
        fac_scr[K_] = (1.0 - lb) * (1.0 - sg)
        fac_scr[Q_] = _silu(slot_t(0))
        fac_scr[V_] = slot_t(2)

        def swap_rows(x):
            half = HEAD_DIM // 2
            parts = []
            for h in range(N_HEADS):
                parts.append(x[h * HEAD_DIM + half:(h + 1) * HEAD_DIM])
                parts.append(x[h * HEAD_DIM:h * HEAD_DIM + half])
            return jnp.concatenate(parts, axis=0)

        cost, sint = cost_ref[...], sint_ref[...]
        cq, ck = slot_t(6), slot_t(7)
        fac_scr[RQ_] = cq * cost + swap_rows(cq) * sint
        fac_scr[RK_] = (ck * cost + swap_rows(ck) * sint) * (HEAD_DIM ** -0.5)
        fac_scr[RV_] = slot_t(8)

    @pl.when(step < N_HEADS)
    def _():
        base = pl.multiple_of(step * HEAD_DIM, HEAD_DIM)
        gam = [float(np.exp(g)) for g in _log_gamma()]
        gamma = jnp.float32(gam[-1])
        for h in range(N_HEADS - 2, -1, -1):
            gamma = jnp.where(step == h, jnp.float32(gam[h]), gamma)
        v_t = fac_scr[V_, pl.ds(base, HEAD_DIM), :]
        rv_t = fac_scr[RV_, pl.ds(base, HEAD_DIM), :]
        acc_a = jnp.zeros((HEAD_DIM, v_t.shape[1]), F32)
        acc_c = jnp.zeros((HEAD_DIM, v_t.shape[1]), F32)
        for k in range(HEAD_DIM):
            def row(which):
                return fac_scr[which, pl.ds(base + k, 1), :]
            s_new = row(F_) * sh_ref[k] + row(K_) * v_t
            nh_ref[k] = s_new
            acc_a = acc_a + row(Q_) * s_new
            r_new = gamma * sr_ref[k] + row(RK_) * rv_t
            nr_ref[k] = r_new
            acc_c = acc_c + row(RQ_) * r_new
        ot_scr[0, pl.ds(base, HEAD_DIM), :] = acc_a
        ot_scr[1, pl.ds(base, HEAD_DIM), :] = acc_c

    @pl.when(step == N_HEADS)
    def _():
        def vec(r):
            return vec_ref[r:r + 1, :]

        def slot(s):
            return u_ref[:, s * WIDTH:(s + 1) * WIDTH]

        def per_head(x, fn):
            return jnp.concatenate([fn(x[h * HEAD_DIM:(h + 1) * HEAD_DIM]) for h in range(N_HEADS)], axis=0)

        def rms_head(o):
            return o * lax.rsqrt(jnp.mean(o * o, axis=0, keepdims=True) + EPS)

        def group_head(o):
            dev = o - jnp.mean(o, axis=0, keepdims=True)
            return dev * lax.rsqrt(jnp.mean(dev * dev, axis=0, keepdims=True) + EPS)

        o_a = per_head(ot_scr[0], rms_head) * vcol(V_HNORM) * _silu(slot_t(3))
        oa_ref[...] = o_a.T
        o_c = per_head(ot_scr[1], group_head) * vcol(V_RNORM) * _silu(slot_t(9))
        oc_ref[...] = o_c.T

        ux = slot(4)
        xc = vec(V_CB) + vec(V_CW0 + CONV_WIDTH - 1) * ux
        for j in range(CONV_WIDTH - 1):
            xc = xc + vec(V_CW0 + j) * scv_ref[j]
        xc_bf = xc.astype(BF16)
        rg = _sigmoid(_dot(xc_bf, wr_ref[...]) + vec(V_BR))
        ig = _sigmoid(_dot(xc_bf, wi_ref[...]) + vec(V_BI))
        log_a = -LRU_C * rg * _softplus(-vec(V_LAM))
        b_in = jnp.sqrt(jnp.maximum(_neg_expm1(2.0 * log_a), 0.0)) * (ig * xc)
        hnew = jnp.exp(log_a) * sl_ref[...] + b_in
        nl_ref[...] = hnew
        ob_ref[...] = hnew * jax.nn.gelu(slot(5))
        for j in range(CONV_WIDTH - 2):
            ncv_ref[j] = scv_ref[j + 1]
        ncv_ref[CONV_WIDTH - 2] = ux

        ud = slot(10)
        wsum = ud
        sums = []
        nxt = 1
        for win in POOL_WINDOWS:
            while nxt < win:
                wsum = wsum + spl_ref[POOL_BUF - nxt]
                nxt += 1
            sums.append(wsum)
        lane = lax.broadcasted_iota(jnp.int32, (1, WIDTH), 1)
        grp = lane // (WIDTH // len(POOL_WINDOWS))
        sel = sums[-1]
        winl = jnp.full((1, WIDTH), float(POOL_WINDOWS[-1]), F32)
        for gi in range(len(POOL_WINDOWS) - 2, -1, -1):
            sel = jnp.where(grp == gi, sums[gi], sel)
            winl = jnp.where(grp == gi, float(POOL_WINDOWS[gi]), winl)
        pooled = sel / jnp.minimum(winl, float(pos0) + 1.0)
        od_ref[...] = _dot((pooled - ud).astype(BF16), wp_ref[...]) * vec(V_PSCALE)
        for j in range(POOL_BUF - 1):
            npl_ref[j] = spl_ref[j + 1]
        npl_ref[POOL_BUF - 1] = ud


def _sample_step(u, lb_logits, vecs, vecs_t, wr_bd, wi_bd, wp_bd, cos_t, sin_t, s_hgrn, s_lru, s_conv,
                 s_ret, s_pool, layer, *, pos0):
    nb = u.shape[0]
    depth = lb_logits.shape[0]
    cost = jnp.broadcast_to(cos_t.reshape(WIDTH, 1), (WIDTH, nb))
    sint = jnp.broadcast_to(sin_t.reshape(WIDTH, 1), (WIDTH, nb))

    def const(shape):
        return pl.BlockSpec(shape, lambda i: tuple(0 for _ in shape))

    def layer_block(shape):
        return pl.BlockSpec((None,) + shape, lambda i: (layer,) + tuple(0 for _ in shape))

    def head(i):
        return jnp.minimum(i, N_HEADS - 1)

    mat = (HEAD_DIM, HEAD_DIM, nb)
    mat_in = pl.BlockSpec((None, None) + mat, lambda i: (layer, head(i), 0, 0, 0))
    mat_out = pl.BlockSpec((None,) + mat, lambda i: (head(i), 0, 0, 0))
    rows = const((nb, WIDTH))
    return pl.pallas_call(
        functools.partial(_sample_step_kernel, layer=layer, pos0=pos0),
        grid=(N_HEADS + 1,),
        in_specs=[
            const((nb, MIX_COLS)), const((WIDTH, depth)),
            layer_block((N_VEC_ROWS, WIDTH)), layer_block((WIDTH, N_VEC_ROWS)),
            layer_block((WIDTH, WIDTH)), layer_block((WIDTH, WIDTH)), layer_block((WIDTH, WIDTH)),
            const((WIDTH, nb)), const((WIDTH, nb)),
            mat_in, layer_block((nb, WIDTH)), layer_block((CONV_WIDTH - 1, nb, WIDTH)), mat_in,
            layer_block((POOL_BUF, nb, WIDTH)),
        ],
        out_specs=[rows, rows, rows, rows, mat_out, rows, const((CONV_WIDTH - 1, nb, WIDTH)), mat_out,
                   const((POOL_BUF, nb, WIDTH))],
        out_shape=[jax.ShapeDtypeStruct((nb, WIDTH), F32)] * 4 + [
            jax.ShapeDtypeStruct((N_HEADS,) + mat, F32), jax.ShapeDtypeStruct((nb, WIDTH), F32),
            jax.ShapeDtypeStruct((CONV_WIDTH - 1, nb, WIDTH), F32), jax.ShapeDtypeStruct((N_HEADS,) + mat, F32),
            jax.ShapeDtypeStruct((POOL_BUF, nb, WIDTH), F32),
        ],
        scratch_shapes=[pltpu.VMEM((7, WIDTH, nb), F32), pltpu.VMEM((2, WIDTH, nb), F32)],
        compiler_params=_cparams("arbitrary"),
        name="sample_step",
    )(u, lb_logits.T, vecs, vecs_t, wr_bd, wi_bd, wp_bd, cost, sint, s_hgrn, s_lru, s_conv, s_ret, s_pool)


def _pick_tile(n, pref):
    t = min(n, pref)
    while n % t:
        t //= 2
    return t


def kernel(x_prompt, x_sample, state_hgrn, state_rglru, state_conv, state_retention, state_pool, lb_logits, ffn1_norm, ffn1_up, ffn1_down, mix_norm, w_in, hgrn_norm, conv_w, conv_b, w_rgate, b_rgate, w_igate, b_igate, lru_lambda, ret_norm, w_pool, pool_scale, w_branch, w_o, ffn2_norm, ffn2_up, ffn2_down, final_norm):
    batch, seq, d = x_prompt.shape
    nb, dec_seq, _ = x_sample.shape
    assert dec_seq == 1
    depth = w_in.shape[0]
    past_len = PAST_LEN

    w_in_bf = w_in.astype(BF16)
    w_br, w_out = w_branch.astype(BF16), w_o.astype(BF16)
    wr_bd = jax.vmap(_block_diag)(w_rgate).astype(BF16)
    wi_bd = jax.vmap(_block_diag)(w_igate).astype(BF16)
    wp_bd = jax.vmap(_block_diag)(w_pool).astype(BF16)
    vecs = jnp.concatenate(
        [hgrn_norm[:, None], conv_w, conv_b[:, None], b_rgate[:, None], b_igate[:, None],
         lru_lambda[:, None], ret_norm[:, None], pool_scale[:, None],
         jnp.zeros((depth, N_VEC_ROWS - 11, WIDTH), F32)], axis=1)
    vecs_t = jnp.swapaxes(vecs, 1, 2)
    hgrn_t = jnp.transpose(state_hgrn, (0, 2, 3, 4, 1))
    ret_t = jnp.transpose(state_retention, (0, 2, 3, 4, 1))
    conv_t = jnp.transpose(state_conv, (0, 2, 1, 3))
    pool_t = jnp.transpose(state_pool, (0, 2, 1, 3))
    n1 = ffn1_norm[:, None, :]
    n2 = ffn2_norm[:, None, :]
    nm = mix_norm[:, None, :]
    fg = final_norm[None, :]

    cos_p, sin_p = _rope_tables(jnp.arange(seq, dtype=F32))
    cos_s, sin_s = _rope_tables(past_len + jnp.arange(dec_seq, dtype=F32))

    xp = x_prompt.reshape(batch * seq, d)
    xs = x_sample.reshape(nb * dec_seq, d)
    tm_p = _pick_tile(batch * seq, DENSE_ROWS)
    tm_w = _pick_tile(batch * seq, WIDE_ROWS)
    tc = _pick_tile(seq, MIXER_ROWS)

    st_p = [[] for _ in range(5)]
    st_s = [[] for _ in range(5)]
    for l in range(depth):
        last = l == depth - 1
        xp, xs = _ffn(xp, xs, n1, ffn1_up, ffn1_down, l, fg, final_norm=False, tm=tm_p)
        u_p, u_s = _inproj(xp, xs, nm, w_in_bf, l, tm=tm_w)
        *br_p, sh, sl, scv, sr, spl = _mixer_prompt(
            u_p, lb_logits, vecs, wr_bd, wi_bd, wp_bd, cos_p, sin_p, l, batch=batch, seq=seq, tc=tc, pos0=0)
        *br_s, nh, nl, ncv, nr, npl = _sample_step(
            u_s, lb_logits, vecs, vecs_t, wr_bd, wi_bd, wp_bd, cos_s, sin_s, hgrn_t, state_rglru,
            conv_t, ret_t, pool_t, l, pos0=past_len)
        xp, xs = _merge(xp, xs, nm, br_p, br_s, w_in_bf, w_br, w_out, l, tm=tm_w)
        xp, xs = _ffn(xp, xs, n2, ffn2_up, ffn2_down, l, fg, final_norm=last, tm=tm_p)
        for lst, s in zip(st_p, (jnp.swapaxes(_diag_blocks(sh), 2, 3), sl[:, 0], scv, _diag_blocks(sr), spl)):
            lst.append(s)
        for lst, s in zip(st_s, (nh, nl, ncv, nr, npl)):
            lst.append(s)

    y_p = xp.reshape(batch, seq, d)
    y_s = xs.reshape(nb, dec_seq, d)
    nh, nl, ncv, nr, npl = (jnp.stack(s) for s in st_s)
    sample_states = (jnp.transpose(nh, (0, 4, 1, 2, 3)), nl, jnp.transpose(ncv, (0, 2, 1, 3)),
                     jnp.transpose(nr, (0, 4, 1, 2, 3)), jnp.transpose(npl, (0, 2, 1, 3)))
    return (y_p, y_s) + tuple(jnp.stack(s) for s in st_p) + sample_states
```
